```python
import math
import jax, jax.numpy as jnp
from jax import lax
import numpy as np

D_MODEL = 2048
BATCH = 4
SEQ = 8192
DEPTH = 1
DEC_BATCH = 2
DEC_SEQ = 16384
PAST_LEN = 128

HEAD_DIM = 64
N_Q_HEADS = 16
N_KV_HEADS = 4
Q_PER_KV = N_Q_HEADS // N_KV_HEADS
ATTN_WIDTH = N_Q_HEADS * HEAD_DIM
KV_WIDTH = N_KV_HEADS * HEAD_DIM
WINDOW = 128
BLOCK_Q = 128
ROT_DIM = HEAD_DIM // 4
ROPE_THETA = 500000.0
HY_WIDTH = D_MODEL - ATTN_WIDTH
HY_ORDER = 2
SHORT_CONV = 3
HY_EMB_DIM = 33
HY_FILTER_HIDDEN = 64
HY_DECAY_TARGET = 1e-2
HY_FAST_DECAY_PCT = 0.3
HY_SLOW_DECAY_PCT = 1.5
IN_WIDTH = ATTN_WIDTH + 2 * KV_WIDTH + (HY_ORDER + 1) * HY_WIDTH
N_EXPERT_GROUPS = 4
EXPERTS_PER_GROUP = 8
N_EXPERTS = N_EXPERT_GROUPS * EXPERTS_PER_GROUP
TOP_K = 2
D_EXPERT = 1024
MOE_BLOCK = 128
EPS = 1e-6

kernel_name = 'hybrid_hyena_swa_hmoe_encoder'


def rms_norm(x, w):
    xf = x.astype(jnp.float32)
    y = xf * lax.rsqrt(jnp.mean(xf * xf, axis=-1, keepdims=True) + EPS)
    return (y * w.astype(jnp.float32)).astype(x.dtype)


def partial_rope(x):
    L = x.shape[1]
    half = ROT_DIM // 2
    inv_freq = jnp.power(ROPE_THETA, -jnp.arange(half, dtype=jnp.float32) * 2.0 / ROT_DIM)
    ang = jnp.arange(L, dtype=jnp.float32)[:, None] * inv_freq[None, :]
    cos = jnp.cos(ang)[None, :, None, :]
    sin = jnp.sin(ang)[None, :, None, :]
    xf = x.astype(jnp.float32)
    x1 = xf[..., :half]
    x2 = xf[..., half:ROT_DIM]
    out = jnp.concatenate([x1 * cos - x2 * sin, x2 * cos + x1 * sin, xf[..., ROT_DIM:]], axis=-1)
    return out.astype(x.dtype)


def windowed_gqa(q, k, v, sink):
    B, L = q.shape[0], q.shape[1]
    nb = L // BLOCK_Q
    span = BLOCK_Q + 2 * WINDOW
    qb = q.reshape(B, nb, BLOCK_Q, N_KV_HEADS, Q_PER_KV, HEAD_DIM).transpose(1, 0, 2, 3, 4, 5)
    pad = ((0, 0), (WINDOW, WINDOW), (0, 0), (0, 0))
    kp = jnp.pad(k, pad)
    vp = jnp.pad(v, pad)
    sink_g = sink.astype(jnp.float32).reshape(N_KV_HEADS, Q_PER_KV)[None, :, :, None]
    scale = HEAD_DIM ** -0.5

    def one_block(args):
        i, qi = args
        start = i * BLOCK_Q
        ki = lax.dynamic_slice_in_dim(kp, start, span, axis=1)
        vi = lax.dynamic_slice_in_dim(vp, start, span, axis=1)
        s = jnp.einsum('bqkgd,bskd->bkgqs', qi, ki).astype(jnp.float32) * scale
        qpos = start + jnp.arange(BLOCK_Q)
        kpos = start - WINDOW + jnp.arange(span)
        ok = (jnp.abs(kpos[None, :] - qpos[:, None]) <= WINDOW) & ((kpos >= 0) & (kpos < L))[None, :]
        s = jnp.where(ok, s, -jnp.inf)
        m = jnp.maximum(jnp.max(s, axis=-1), sink_g)
        p = jnp.exp(s - m[..., None])
        denom = jnp.sum(p, axis=-1) + jnp.exp(sink_g - m)
        p = (p / denom[..., None]).astype(vi.dtype)
        return jnp.einsum('bkgqs,bskd->bqkgd', p, vi)

    out = lax.map(one_block, (jnp.arange(nb), qb))
    return out.transpose(1, 0, 2, 3, 4, 5).reshape(B, L, ATTN_WIDTH)


def hyena_filter_fft(L, w1, b1, w2, b2, w3, b3, w4, freq):
    f32 = jnp.float32
    t = jnp.linspace(0.0, 1.0, L, dtype=f32)[:, None]
    bands = (HY_EMB_DIM - 1) // 2
    fr = jnp.linspace(1e-4, bands - 1, bands, dtype=f32)
    w = 2.0 * math.pi * jnp.arange(L, dtype=f32) / L
    ang = w[:, None] * fr[None, :]
    z = jnp.concatenate([t, jnp.cos(ang), -jnp.sin(ang)], axis=-1)
    a = freq.astype(f32)
    h = jnp.sin(a * (z @ w1.astype(f32) + b1.astype(f32)))
    h = jnp.sin(a * (h @ w2.astype(f32) + b2.astype(f32)))
    h = jnp.sin(a * (h @ w3.astype(f32) + b3.astype(f32)))
    h = (h @ w4.astype(f32)).reshape(L, 2, HY_ORDER, HY_WIDTH)
    max_decay = math.log(HY_DECAY_TARGET) / HY_FAST_DECAY_PCT
    min_decay = math.log(HY_DECAY_TARGET) / HY_SLOW_DECAY_PCT
    deltas = jnp.linspace(min_decay, max_decay, HY_WIDTH, dtype=f32)
    decay = jnp.exp(-t * jnp.abs(deltas)[None, :])
    h = h * decay[:, None, None, :]
    h_fwd = h[:, 0]
    h_bwd = h[:, 1]
    g = jnp.concatenate([h_fwd, jnp.zeros((1, HY_ORDER, HY_WIDTH), f32), h_bwd[1:][::-1]], axis=0)
    g = g / jnp.sum(jnp.abs(g), axis=0, keepdims=True)
    return jnp.fft.rfft(g, axis=0)


def hyena_mixer(u, conv_w, conv_b, filt_fft, hy_bias):
    L = u.shape[1]
    up = jnp.pad(u, ((0, 0), (1, 1), (0, 0)))
    uc = conv_w[0] * up[:, :-2] + conv_w[1] * up[:, 1:-1] + conv_w[2] * up[:, 2:] + conv_b
    gates = (uc[..., :HY_WIDTH], uc[..., HY_WIDTH:2 * HY_WIDTH])
    z = uc[..., 2 * HY_WIDTH:].astype(jnp.float32)
    for n in range(HY_ORDER):
        zf = jnp.fft.rfft(z, n=2 * L, axis=1)
        conv = jnp.fft.irfft(zf * filt_fft[None, :, n, :], n=2 * L, axis=1)[:, :L]
        z = gates[n].astype(jnp.float32) * (conv + z * hy_bias[n].astype(jnp.float32))
    return z.astype(u.dtype)


def hier_moe(x2d, w_rg, b_rg, w_re, b_re, w_gate, w_up, w_down):
    T = x2d.shape[0]
    f32 = jnp.float32
    xf = x2d.astype(f32)
    g_logits = xf @ w_rg.astype(f32) + b_rg.astype(f32)
    g_prob = jax.nn.softmax(g_logits, axis=-1)
    g_idx = jnp.argmax(g_logits, axis=-1)
    g_gate = jnp.take_along_axis(g_prob, g_idx[:, None], axis=1)
    e_logits = (xf @ w_re.astype(f32) + b_re.astype(f32)).reshape(T, N_EXPERT_GROUPS, EXPERTS_PER_GROUP)
    e_logits = jnp.take_along_axis(e_logits, g_idx[:, None, None], axis=1)[:, 0]
    top_val, top_loc = lax.top_k(e_logits, TOP_K)
    weights = g_gate * jax.nn.softmax(top_val, axis=-1)
    experts = g_idx[:, None] * EXPERTS_PER_GROUP + top_loc

    A = T * TOP_K
    flat_e = experts.reshape(-1)
    flat_w = weights.reshape(-1)
    order = jnp.argsort(flat_e)
    sorted_e = flat_e[order]
    tok = (order // TOP_K).astype(jnp.int32)
    counts = jnp.bincount(flat_e, length=N_EXPERTS)
    seg_start = jnp.cumsum(counts) - counts
    padded = ((counts + MOE_BLOCK - 1) // MOE_BLOCK) * MOE_BLOCK
    pend = jnp.cumsum(padded)
    pstart = pend - padded
    dest = pstart[sorted_e] + (jnp.arange(A) - seg_start[sorted_e])
    n_blocks = (A + MOE_BLOCK - 1) // MOE_BLOCK + N_EXPERTS
    P = n_blocks * MOE_BLOCK
    buf_tok = jnp.full((P,), T, jnp.int32).at[dest].set(tok)
    buf_w = jnp.zeros((P,), f32).at[dest].set(flat_w[order])
    block_expert = jnp.minimum(jnp.searchsorted(pend, jnp.arange(n_blocks) * MOE_BLOCK, side='right'), N_EXPERTS - 1)
    x_pad = jnp.concatenate([x2d, jnp.zeros((1, x2d.shape[1]), x2d.dtype)], axis=0)

    def expert_block(b):
        rows = lax.dynamic_slice_in_dim(buf_tok, b * MOE_BLOCK, MOE_BLOCK)
        e = block_expert[b]
        xb = x_pad[rows]
        hid = jax.nn.silu(xb @ w_gate[e]) * (xb @ w_up[e])
        return hid @ w_down[e]

    yb = lax.map(expert_block, jnp.arange(n_blocks)).reshape(P, -1)
    out = jnp.zeros((T + 1, x2d.shape[1]), f32).at[buf_tok].add(yb.astype(f32) * buf_w[:, None])
    return out[:T].astype(x2d.dtype)


def encoder_layer(x, norm1_w, w_in, q_norm_w, k_norm_w, attn_sink, conv_w, conv_b,
                  filt_w1, filt_b1, filt_w2, filt_b2, filt_w3, filt_b3, filt_w4, filt_freq, hy_bias,
                  attn_out_norm_w, hy_out_norm_w, w_out, norm2_w,
                  w_route_group, b_route_group, w_route_expert, b_route_expert, w_gate, w_up, w_down):
    B, L, D = x.shape
    h = rms_norm(x, norm1_w)
    proj = h @ w_in
    o1 = ATTN_WIDTH
    o2 = o1 + KV_WIDTH
    o3 = o2 + KV_WIDTH
    q = proj[..., :o1].reshape(B, L, N_Q_HEADS, HEAD_DIM)
    k = proj[..., o1:o2].reshape(B, L, N_KV_HEADS, HEAD_DIM)
    v = proj[..., o2:o3].reshape(B, L, N_KV_HEADS, HEAD_DIM)
    u = proj[..., o3:]
    q = partial_rope(rms_norm(q, q_norm_w))
    k = partial_rope(rms_norm(k, k_norm_w))
    y_attn = windowed_gqa(q, k, v, attn_sink)
    filt_fft = hyena_filter_fft(L, filt_w1, filt_b1, filt_w2, filt_b2, filt_w3, filt_b3, filt_w4, filt_freq)
    y_hy = hyena_mixer(u, conv_w, conv_b, filt_fft, hy_bias)
    mixed = jnp.concatenate([rms_norm(y_attn, attn_out_norm_w), rms_norm(y_hy, hy_out_norm_w)], axis=-1)
    x = x + mixed @ w_out
    h2 = rms_norm(x, norm2_w).reshape(B * L, D)
    x = x + hier_moe(h2, w_route_group, b_route_group, w_route_expert, b_route_expert,
                     w_gate, w_up, w_down).reshape(B, L, D)
    return x


def setup_inputs(seed: int = 0) -> dict:
    key = jax.random.key(seed)
    ks = jax.random.split(key, 32)

    def nrm(k, shape, s):
        return jax.random.normal(k, shape, jnp.float32) * s

    FW = 2 * HY_ORDER * HY_WIDTH
    return {
        'x_prompt': nrm(ks[0], (BATCH, SEQ, D_MODEL), 1.0),
        'x_sample': nrm(ks[1], (DEC_BATCH, DEC_SEQ, D_MODEL), 1.0),
        'norm1_w': 1.0 + nrm(ks[2], (DEPTH, D_MODEL), 0.01),
        'w_in': nrm(ks[3], (DEPTH, D_MODEL, IN_WIDTH), D_MODEL ** -0.5),
        'q_norm_w': 1.0 + nrm(ks[4], (DEPTH, HEAD_DIM), 0.01),
        'k_norm_w': 1.0 + nrm(ks[5], (DEPTH, HEAD_DIM), 0.01),
        'attn_sink': nrm(ks[6], (DEPTH, N_Q_HEADS), 0.5),
        'conv_w': nrm(ks[7], (DEPTH, SHORT_CONV, 3 * HY_WIDTH), SHORT_CONV ** -0.5),
        'conv_b': nrm(ks[8], (DEPTH, 3 * HY_WIDTH), 0.02),
        'filt_w1': nrm(ks[9], (DEPTH, HY_EMB_DIM, HY_FILTER_HIDDEN), HY_EMB_DIM ** -0.5),
        'filt_b1': nrm(ks[10], (DEPTH, HY_FILTER_HIDDEN), 0.02),
        'filt_w2': nrm(ks[11], (DEPTH, HY_FILTER_HIDDEN, HY_FILTER_HIDDEN), HY_FILTER_HIDDEN ** -0.5),
        'filt_b2': nrm(ks[12], (DEPTH, HY_FILTER_HIDDEN), 0.02),
        'filt_w3': nrm(ks[13], (DEPTH, HY_FILTER_HIDDEN, HY_FILTER_HIDDEN), HY_FILTER_HIDDEN ** -0.5),
        'filt_b3': nrm(ks[14], (DEPTH, HY_FILTER_HIDDEN), 0.02),
        'filt_w4': nrm(ks[15], (DEPTH, HY_FILTER_HIDDEN, FW), HY_FILTER_HIDDEN ** -0.5),
        'filt_freq': 1.0 + nrm(ks[16], (DEPTH, HY_FILTER_HIDDEN), 0.01),
        'hy_bias': nrm(ks[17], (DEPTH, HY_ORDER, HY_WIDTH), 1.0),
        'attn_out_norm_w': 1.0 + nrm(ks[18], (DEPTH, ATTN_WIDTH), 0.01),
        'hy_out_norm_w': 1.0 + nrm(ks[19], (DEPTH, HY_WIDTH), 0.01),
        'w_out': nrm(ks[20], (DEPTH, D_MODEL, D_MODEL), D_MODEL ** -0.5),
        'norm2_w': 1.0 + nrm(ks[21], (DEPTH, D_MODEL), 0.01),
        'w_route_group': nrm(ks[22], (DEPTH, D_MODEL, N_EXPERT_GROUPS), D_MODEL ** -0.5),
        'b_route_group': nrm(ks[23], (DEPTH, N_EXPERT_GROUPS), 0.01),
        'w_route_expert': nrm(ks[24], (DEPTH, D_MODEL, N_EXPERTS), D_MODEL ** -0.5),
        'b_route_expert': nrm(ks[25], (DEPTH, N_EXPERTS), 0.01),
        'w_gate': nrm(ks[26], (DEPTH, N_EXPERTS, D_MODEL, D_EXPERT), D_MODEL ** -0.5),
        'w_up': nrm(ks[27], (DEPTH, N_EXPERTS, D_MODEL, D_EXPERT), D_MODEL ** -0.5),
        'w_down': nrm(ks[28], (DEPTH, N_EXPERTS, D_EXPERT, D_MODEL), D_EXPERT ** -0.5),
    }


def reference(x_prompt, x_sample, norm1_w, w_in, q_norm_w, k_norm_w, attn_sink, conv_w, conv_b,
              filt_w1, filt_b1, filt_w2, filt_b2, filt_w3, filt_b3, filt_w4, filt_freq, hy_bias,
              attn_out_norm_w, hy_out_norm_w, w_out, norm2_w,
              w_route_group, b_route_group, w_route_expert, b_route_expert, w_gate, w_up, w_down):
    def trunk(x):
        for l in range(DEPTH):
            x = encoder_layer(x, norm1_w[l], w_in[l], q_norm_w[l], k_norm_w[l], attn_sink[l],
                              conv_w[l], conv_b[l], filt_w1[l], filt_b1[l], filt_w2[l], filt_b2[l],
                              filt_w3[l], filt_b3[l], filt_w4[l], filt_freq[l], hy_bias[l],
                              attn_out_norm_w[l], hy_out_norm_w[l], w_out[l], norm2_w[l],
                              w_route_group[l], b_route_group[l], w_route_expert[l], b_route_expert[l],
                              w_gate[l], w_up[l], w_down[l])
        return x

    y_prompt = trunk(x_prompt)
    y_sample = trunk(x_sample)
    return (y_prompt, y_sample)
```

```python
import functools
import math

import jax
import jax.numpy as jnp
from jax import lax
from jax.experimental import pallas as pl
from jax.experimental.pallas import tpu as pltpu

F32 = jnp.float32
BF16 = jnp.bfloat16

D_MODEL = 2048
HEAD_DIM = 64
N_Q_HEADS = 16
N_KV_HEADS = 4
Q_PER_KV = N_Q_HEADS // N_KV_HEADS
ATTN_WIDTH = N_Q_HEADS * HEAD_DIM
KV_WIDTH = N_KV_HEADS * HEAD_DIM
QK_WIDTH = ATTN_WIDTH + KV_WIDTH
QKV_WIDTH = ATTN_WIDTH + 2 * KV_WIDTH
WINDOW = 128
ROT_DIM = HEAD_DIM // 4
ROPE_THETA = 500000.0
HY_WIDTH = D_MODEL - ATTN_WIDTH
HY_ORDER = 2
HY_EMB_DIM = 33
HY_FILTER_HIDDEN = 64
HY_DECAY_TARGET = 1e-2
HY_FAST_DECAY_PCT = 0.3
HY_SLOW_DECAY_PCT = 1.5
N_EXPERT_GROUPS = 4
EXPERTS_PER_GROUP = 8
N_EXPERTS = N_EXPERT_GROUPS * EXPERTS_PER_GROUP
TOP_K = 2
D_EXPERT = 1024
EPS = 1e-6

LANES = 128
FFT_N2 = 128
MOE_BLOCK_ROWS = 256
ROUTER_PAD = 128
VMEM_LIMIT = 56 * 1024 * 1024
NEG_BIG = -1e30


def _cparams(*sem):
    return pltpu.CompilerParams(dimension_semantics=sem, vmem_limit_bytes=VMEM_LIMIT)


def _split_bf16(a):
    hi = a.astype(BF16)
    lo = (a - hi.astype(F32)).astype(BF16)
    return hi, lo


def _dot(a, b):
    return jnp.dot(a, b, preferred_element_type=F32)


def _dot3(a, b):
    a_hi, a_lo = _split_bf16(a)
    b_hi, b_lo = _split_bf16(b)
    return _dot(a_hi, b_hi) + _dot(a_lo, b_hi) + _dot(a_hi, b_lo)


def _qkv_kernel(x_ref, nw_ref, w_ref, seg_ref, hw_ref, rc_ref, rs1_ref, rs2_ref,
                q_ref, k_ref, v_ref):
    x = x_ref[...]
    inv = lax.rsqrt(jnp.mean(x * x, axis=-1, keepdims=True) + EPS)
    h = (x * inv * nw_ref[...]).astype(BF16)
    acc = _dot(h, w_ref[...])
    qk = acc[:, :QK_WIDTH]
    hi, lo = _split_bf16(qk * qk)
    seg = seg_ref[...]
    w = seg.shape[0]
    ms = jnp.concatenate(
        [_dot(hi[:, c * w:(c + 1) * w], seg) + _dot(lo[:, c * w:(c + 1) * w], seg)
         for c in range(QK_WIDTH // w)], axis=-1)
    xn = qk * lax.rsqrt(ms + EPS) * hw_ref[...]
    reps = QK_WIDTH // LANES
    rc = jnp.tile(rc_ref[...], (1, reps))
    rs1 = jnp.tile(rs1_ref[...], (1, reps))
    rs2 = jnp.tile(rs2_ref[...], (1, reps))
    half = ROT_DIM // 2
    y = xn * rc + pltpu.roll(xn, QK_WIDTH - half, 1) * rs1 + pltpu.roll(xn, half, 1) * rs2
    q_ref[...] = (y[:, :ATTN_WIDTH] * (HEAD_DIM ** -0.5)).astype(BF16)
    k_ref[...] = y[:, ATTN_WIDTH:].astype(BF16)
    v_ref[...] = acc[:, QK_WIDTH:].astype(BF16)


def _rope_tables(L):
    half = ROT_DIM // 2
    inv_freq = jnp.power(ROPE_THETA, -jnp.arange(half, dtype=F32) * 2.0 / ROT_DIM)
    ang = jnp.arange(L, dtype=F32)[:, None] * inv_freq[None, :]
    d = jnp.arange(LANES) % HEAD_DIM
    ang_l = ang[:, d % half]
    cos, sin = jnp.cos(ang_l), jnp.sin(ang_l)
    rc = jnp.where(d < ROT_DIM, cos, 1.0)
    rs1 = jnp.where(d < half, -sin, 0.0)
    rs2 = jnp.where((d >= half) & (d < ROT_DIM), sin, 0.0)
    return rc.astype(F32), rs1.astype(F32), rs2.astype(F32)


def _qkv_proj(x2d, L, norm1_w, w_qkv, q_norm_w, k_norm_w, tm=256):
    T = x2d.shape[0]
    seg_w = 256
    seg = (jnp.arange(seg_w)[:, None] // HEAD_DIM == jnp.arange(seg_w)[None, :] // HEAD_DIM)
    seg = (seg.astype(F32) / HEAD_DIM).astype(BF16)
    hw = jnp.concatenate([jnp.tile(q_norm_w, N_Q_HEADS), jnp.tile(k_norm_w, N_KV_HEADS)])[None, :]
    rc, rs1, rs2 = _rope_tables(L)
    lt = L // tm
    row = lambda i: (i, 0)
    const = lambda i: (0, 0)
    pos = lambda i: (i % lt, 0)
    return pl.pallas_call(
        _qkv_kernel,
        grid=(T // tm,),
        in_specs=[
            pl.BlockSpec((tm, D_MODEL), row),
            pl.BlockSpec((1, D_MODEL), const),
            pl.BlockSpec((D_MODEL, QKV_WIDTH), const),
            pl.BlockSpec((seg_w, seg_w), const),
            pl.BlockSpec((1, QK_WIDTH), const),
            pl.BlockSpec((tm, LANES), pos),
            pl.BlockSpec((tm, LANES), pos),
            pl.BlockSpec((tm, LANES), pos),
        ],
        out_specs=[
            pl.BlockSpec((tm, ATTN_WIDTH), row),
            pl.BlockSpec((tm, KV_WIDTH), row),
            pl.BlockSpec((tm, KV_WIDTH), row),
        ],
        out_shape=[
            jax.ShapeDtypeStruct((T, ATTN_WIDTH), BF16),
            jax.ShapeDtypeStruct((T, KV_WIDTH), BF16),
            jax.ShapeDtypeStruct((T, KV_WIDTH), BF16),
        ],
        compiler_params=_cparams("parallel"),
        name="qkv_proj",
    )(x2d, norm1_w[None, :], w_qkv, seg, hw, rc, rs1, rs2)


def _uproj_kernel(x_ref, nw_ref, w_ref, o_ref, h_scr):
    @pl.when(pl.program_id(1) == 0)
    def _():
        x = x_ref[...]
        inv = lax.rsqrt(jnp.mean(x * x, axis=-1, keepdims=True) + EPS)
        h_scr[...] = (x * inv * nw_ref[...]).astype(BF16)

    o_ref[...] = _dot(h_scr[...], w_ref[...])


def _u_proj(x2d, norm1_w, w_u, tm=1024, tn=1024):
    T = x2d.shape[0]
    n = w_u.shape[1]
    return pl.pallas_call(
        _uproj_kernel,
        grid=(T // tm, n // tn),
        in_specs=[
            pl.BlockSpec((tm, D_MODEL), lambda i, j: (i, 0)),
            pl.BlockSpec((1, D_MODEL), lambda i, j: (0, 0)),
            pl.BlockSpec((D_MODEL, tn), lambda i, j: (0, j)),
        ],
        out_specs=pl.BlockSpec((tm, tn), lambda i, j: (i, j)),
        out_shape=jax.ShapeDtypeStruct((T, n), F32),
        scratch_shapes=[pltpu.VMEM((tm, D_MODEL), BF16)],
        compiler_params=_cparams("parallel", "arbitrary"),
        name="u_proj",
    )(x2d, norm1_w[None, :], w_u)


def _sconv_kernel(u_ref, up_ref, un_ref, w_ref, b_ref, o_ref):
    i = pl.program_id(1)
    last = pl.num_programs(1) - 1
    u = u_ref[0]
    tl = u.shape[0]
    prev_row = jnp.where(i > 0, up_ref[0, 7:8, :], 0.0)
    next_row = jnp.where(i < last, un_ref[0, 0:1, :], 0.0)
    rid = lax.broadcasted_iota(jnp.int32, u.shape, 0)
    um = jnp.where(rid == 0, prev_row, pltpu.roll(u, 1, 0))
    up = jnp.where(rid == tl - 1, next_row, pltpu.roll(u, tl - 1, 0))
    w = w_ref[...]
    o_ref[0, 0] = w[0:1] * um + w[1:2] * u + w[2:3] * up + b_ref[...]


def _short_conv(u3d, conv_w, conv_b, tl=1024):
    B, L, W = u3d.shape
    C = HY_WIDTH
    nparts = W // C
    r8 = tl // 8
    return pl.pallas_call(
        _sconv_kernel,
        grid=(B, L // tl, nparts),
        in_specs=[
            pl.BlockSpec((1, tl, C), lambda b, i, c: (b, i, c)),
            pl.BlockSpec((1, 8, C), lambda b, i, c: (b, jnp.maximum(i * r8 - 1, 0), c)),
            pl.BlockSpec((1, 8, C), lambda b, i, c: (b, jnp.minimum((i + 1) * r8, L // 8 - 1), c)),
            pl.BlockSpec((3, C), lambda b, i, c: (0, c)),
            pl.BlockSpec((1, C), lambda b, i, c: (0, c)),
        ],
        out_specs=pl.BlockSpec((1, 1, tl, C), lambda b, i, c: (c, b, i, 0)),
        out_shape=jax.ShapeDtypeStruct((nparts, B, L, C), F32),
        compiler_params=_cparams("parallel", "parallel", "parallel"),
        name="short_conv",
    )(u3d, u3d, u3d, conv_w, conv_b[None, :])


def _fft_tables(L):
    N = 2 * L
    N2 = FFT_N2
    N1 = N // N2
    H = N1 // 2
    k1 = jnp.arange(N1, dtype=jnp.int32)
    n1 = jnp.arange(N1, dtype=jnp.int32)
    n2 = jnp.arange(N2, dtype=jnp.int32)
    n = N2 * n1[None, None, :] + n2[:, None, None]
    idx = (k1[None, :, None] * n) % N
    th = idx.astype(F32) * (2.0 * math.pi / N)
    c, s = jnp.cos(th), jnp.sin(th)
    m_filt = jnp.concatenate([c, -s], axis=1).astype(BF16)
    ch, sh = c[:, :, :H], s[:, :, :H]
    m_fwd = jnp.concatenate([jnp.concatenate([ch, sh], axis=2),
                             jnp.concatenate([-sh, ch], axis=2)], axis=1)
    m_inv = jnp.swapaxes(m_fwd, 1, 2)
    k2 = jnp.arange(N2, dtype=jnp.int32)
    ph = ((k2[:, None] * n2[None, :]) % N2).astype(F32) * (2.0 * math.pi / N2)
    c2, s2 = jnp.cos(ph), jnp.sin(ph)
    m2_fwd = jnp.concatenate([jnp.concatenate([c2, s2], axis=1),
                              jnp.concatenate([-s2, c2], axis=1)], axis=0)
    m2_inv = m2_fwd.T
    return dict(N1=N1, N2=N2, m_filt=m_filt, m_fwd=m_fwd.astype(BF16), m_inv=m_inv.astype(BF16),
                m2_fwd=m2_fwd.astype(BF16), m2_inv=m2_inv.astype(BF16))


def _filt1_kernel(w1_ref, b1_ref, w2_ref, b2_ref, w3_ref, b3_ref, w4_ref, a_ref, frl_ref,
                  dl_ref, m_ref, o_ref, sabs_ref, *, L, N1, N2, nb):
    j = pl.program_id(0)
    C2 = HY_ORDER * HY_WIDTH
    H = N1 // 2
    rows = nb * N1
    ridx = lax.broadcasted_iota(jnp.int32, (rows, LANES), 0)
    lane = lax.broadcasted_iota(jnp.int32, (rows, LANES), 1)
    n2 = j * nb + ridx // N1
    n1 = ridx % N1
    r = n1 * N2 + n2
    lag = jnp.where(r < L, r, 2 * L - r)
    valid = (r != L).astype(F32)
    lagf = jnp.minimum(lag, L - 1).astype(F32)
    t = lagf / (L - 1)
    wpos = lagf * (2.0 * math.pi / L)
    phase = wpos * frl_ref[...]
    bands = (HY_EMB_DIM - 1) // 2
    zemb = jnp.where(lane == 0, t,
                     jnp.where(lane <= bands, jnp.cos(phase),
                               jnp.where(lane <= 2 * bands, -jnp.sin(phase), 0.0)))
    a = a_ref[...]
    h = jnp.sin(a * (_dot3(zemb, w1_ref[...]) + b1_ref[...]))
    h = jnp.sin(a * (_dot3(h, w2_ref[...]) + b2_ref[...]))
    h = jnp.sin(a * (_dot3(h, w3_ref[...]) + b3_ref[...]))
    h_hi, h_lo = _split_bf16(h)
    w4 = w4_ref[...]
    decay = jnp.exp(-jnp.tile(t, (1, HY_WIDTH // LANES)) * dl_ref[...])
    decay = decay * jnp.tile(valid, (1, HY_WIDTH // LANES))
    decay2 = jnp.tile(decay, (1, HY_ORDER))

    @pl.when(j == 0)
    def _():
        sabs_ref[...] = jnp.zeros_like(sabs_ref)

    tot = jnp.zeros((8, C2), F32)
    for l in range(nb):
        fs = slice(l * N1, l * N1 + H)
        bs = slice(l * N1 + H, (l + 1) * N1)
        gf = _dot(h_hi[fs], w4[:, :C2]) + _dot(h_lo[fs], w4[:, :C2])
        gb = _dot(h_hi[bs], w4[:, C2:]) + _dot(h_lo[bs], w4[:, C2:])
        g = jnp.concatenate([gf, gb], axis=0) * decay2[l * N1:(l + 1) * N1]
        tot = tot + jnp.sum(jnp.abs(g).reshape(N1 // 8, 8, C2), axis=0)
        y = _dot(m_ref[l], g.astype(BF16))
        o_ref[0, :, l * C2:(l + 1) * C2] = y[:N1]
        o_ref[1, :, l * C2:(l + 1) * C2] = y[N1:]
    sabs_ref[...] += tot


def _filt2_kernel(a_ref, m_ref, sc_ref, o_ref):
    kb = a_ref.shape[1]
    N2 = a_ref.shape[2]
    for kl in range(kb):
        a = jnp.concatenate([a_ref[0, kl], a_ref[1, kl]], axis=0).astype(BF16)
        x = _dot(m_ref[...], a) * sc_ref[...]
        o_ref[0, kl] = x[:N2]
        o_ref[1, kl] = x[N2:]


def _hyena_filter_spectrum(L, tabs, filt_w1, filt_b1, filt_w2, filt_b2, filt_w3, filt_b3,
                           filt_w4, filt_freq):
    N1, N2 = tabs["N1"], tabs["N2"]
    N = N1 * N2
    C2 = HY_ORDER * HY_WIDTH
    nb = max(1, 512 // N1)
    bands = (HY_EMB_DIM - 1) // 2
    fr = jnp.linspace(1e-4, bands - 1, bands, dtype=F32)
    frl = jnp.zeros((LANES,), F32).at[1:1 + bands].set(fr).at[1 + bands:1 + 2 * bands].set(fr)[None, :]
    w1p = jnp.zeros((LANES, HY_FILTER_HIDDEN), F32).at[:HY_EMB_DIM].set(filt_w1)
    max_decay = math.log(HY_DECAY_TARGET) / HY_FAST_DECAY_PCT
    min_decay = math.log(HY_DECAY_TARGET) / HY_SLOW_DECAY_PCT
    deltas = jnp.abs(jnp.linspace(min_decay, max_decay, HY_WIDTH, dtype=F32))[None, :]
    full = lambda shape: pl.BlockSpec(shape, lambda j: (0,) * len(shape))
    a_g, sabs = pl.pallas_call(
        functools.partial(_filt1_kernel, L=L, N1=N1, N2=N2, nb=nb),
        grid=(N2 // nb,),
        in_specs=[
            full((LANES, HY_FILTER_HIDDEN)), full((1, HY_FILTER_HIDDEN)),
            full((HY_FILTER_HIDDEN, HY_FILTER_HIDDEN)), full((1, HY_FILTER_HIDDEN)),
            full((HY_FILTER_HIDDEN, HY_FILTER_HIDDEN)), full((1, HY_FILTER_HIDDEN)),
            full((HY_FILTER_HIDDEN, 2 * C2)), full((1, HY_FILTER_HIDDEN)),
            full((1, LANES)), full((1, HY_WIDTH)),
            pl.BlockSpec((nb, 2 * N1, N1), lambda j: (j, 0, 0)),
        ],
        out_specs=[
            pl.BlockSpec((2, N1, nb * C2), lambda j: (0, 0, j)),
            pl.BlockSpec((8, C2), lambda j: (0, 0)),
        ],
        out_shape=[
            jax.ShapeDtypeStruct((2, N1, N2 * C2), F32),
            jax.ShapeDtypeStruct((8, C2), F32),
        ],
        compiler_params=_cparams("arbitrary"),
        name="hyena_filter_stage1",
    )(w1p, filt_b1[None, :], filt_w2, filt_b2[None, :], filt_w3, filt_b3[None, :],
      filt_w4.astype(BF16), filt_freq[None, :], frl, deltas, tabs["m_filt"])
    scale = (1.0 / (jnp.sum(sabs, axis=0) * N))[None, :]
    kb = 4
    tc = HY_WIDTH
    return pl.pallas_call(
        _filt2_kernel,
        grid=(N1 // kb, C2 // tc),
        in_specs=[
            pl.BlockSpec((2, kb, N2, tc), lambda i, c: (0, i, 0, c)),
            pl.BlockSpec((2 * N2, 2 * N2), lambda i, c: (0, 0)),
            pl.BlockSpec((1, tc), lambda i, c: (0, c)),
        ],
        out_specs=pl.BlockSpec((2, kb, N2, tc), lambda i, c: (0, i, 0, c)),
        out_shape=jax.ShapeDtypeStruct((2, N1, N2, C2), F32),
        compiler_params=_cparams("parallel", "parallel"),
        name="hyena_filter_stage2",
    )(a_g.reshape(2, N1, N2, C2), tabs["m2_fwd"], scale)


def _fwd1_kernel(x_ref, m_ref, o_ref):
    nb = m_ref.shape[0]
    N1 = m_ref.shape[2]
    C = HY_WIDTH
    for l in range(nb):
        xl = x_ref[0, :, :, l * C:(l + 1) * C].reshape(N1, C).astype(BF16)
        y = _dot(m_ref[l], xl)
        o_ref[0, 0, :, l * C:(l + 1) * C] = y[:N1]
        o_ref[0, 1, :, l * C:(l + 1) * C] = y[N1:]


def _hyena_stage1(z4d, which, tabs):
    N1, N2 = tabs["N1"], tabs["N2"]
    C = HY_WIDTH
    B = z4d.shape[1]
    nb = max(1, 1024 // N1)
    return pl.pallas_call(
        _fwd1_kernel,
        grid=(B // 2, N2 // nb),
        in_specs=[
            pl.BlockSpec((1, 2, N1 // 2, nb * C), lambda p, j: (which, p, 0, j)),
            pl.BlockSpec((nb, 2 * N1, N1), lambda p, j: (j, 0, 0)),
        ],
        out_specs=pl.BlockSpec((1, 2, N1, nb * C), lambda p, j: (p, 0, 0, j)),
        out_shape=jax.ShapeDtypeStruct((B // 2, 2, N1, N2 * C), F32),
        compiler_params=_cparams("parallel", "parallel"),
        name="hyena_stage1",
    )(z4d, tabs["m_fwd"])


def _mid_kernel(a_ref, g_ref, mf_ref, mi_ref, o_ref):
    kb = a_ref.shape[2]
    N2 = a_ref.shape[3]
    for kl in range(kb):
        a = jnp.concatenate([a_ref[0, 0, kl], a_ref[0, 1, kl]], axis=0).astype(BF16)
        x = _dot(mf_ref[...], a)
        xr, xi = x[:N2], x[N2:]
        gr, gi = g_ref[0, kl], g_ref[1, kl]
        y = jnp.concatenate([xr * gr - xi * gi, xr * gi + xi * gr], axis=0).astype(BF16)
        b = _dot(mi_ref[...], y)
        o_ref[0, 0, kl] = b[:N2]
        o_ref[0, 1, kl] = b[N2:]


def _hyena_stage2(a5d, g4d, order, tabs, kb=4):
    N1, N2 = tabs["N1"], tabs["N2"]
    C = HY_WIDTH
    P = a5d.shape[0]
    blk = (1, 2, kb, N2, C)
    return pl.pallas_call(
        _mid_kernel,
        grid=(N1 // kb, P),
        in_specs=[
            pl.BlockSpec(blk, lambda i, p: (p, 0, i, 0, 0)),
            pl.BlockSpec((2, kb, N2, C), lambda i, p: (0, i, 0, order)),
            pl.BlockSpec((2 * N2, 2 * N2), lambda i, p: (0, 0)),
            pl.BlockSpec((2 * N2, 2 * N2), lambda i, p: (0, 0)),
        ],
        out_specs=pl.BlockSpec(blk, lambda i, p: (p, 0, i, 0, 0)),
        out_shape=jax.ShapeDtypeStruct(a5d.shape, F32),
        compiler_params=_cparams("parallel", "arbitrary"),
        name="hyena_stage2",
    )(a5d, g4d, tabs["m2_fwd"], tabs["m2_inv"])


def _inv1_kernel(b_ref, m_ref, gate_ref, z_ref, bias_ref, nw_ref, o_ref, *, final):
    nb = m_ref.shape[0]
    N1 = m_ref.shape[1]
    C = HY_WIDTH
    for l in range(nb):
        cs = slice(l * C, (l + 1) * C)
        b = jnp.concatenate([b_ref[0, 0, :, cs], b_ref[0, 1, :, cs]], axis=0).astype(BF16)
        conv = _dot(m_ref[l], b).reshape(2, N1 // 2, C)
        out = gate_ref[0, :, :, cs] * (conv + z_ref[0, :, :, cs] * bias_ref[...])
        if final:
            inv = lax.rsqrt(jnp.mean(out * out, axis=-1, keepdims=True) + EPS)
            out = out * inv * nw_ref[...]
        o_ref[0, :, :, cs] = out.astype(o_ref.dtype)


def _hyena_stage3(b4d, gates4d, gate_idx, z4d, z_idx, bias, norm_w, tabs, final):
    N1, N2 = tabs["N1"], tabs["N2"]
    C = HY_WIDTH
    P = b4d.shape[0]
    nb = max(1, 512 // N1)
    half_blk = (1, 2, N1 // 2, nb * C)
    return pl.pallas_call(
        functools.partial(_inv1_kernel, final=final),
        grid=(P, N2 // nb),
        in_specs=[
            pl.BlockSpec((1, 2, N1, nb * C), lambda p, j: (p, 0, 0, j)),
            pl.BlockSpec((nb, N1, 2 * N1), lambda p, j: (j, 0, 0)),
            pl.BlockSpec(half_blk, lambda p, j: (gate_idx, p, 0, j)),
            pl.BlockSpec(half_blk, lambda p, j: (z_idx, p, 0, j)),
            pl.BlockSpec((1, C), lambda p, j: (0, 0)),
            pl.BlockSpec((1, C), lambda p, j: (0, 0)),
        ],
        out_specs=pl.BlockSpec(half_blk, lambda p, j: (0, p, 0, j)),
        out_shape=jax.ShapeDtypeStruct((1, 2 * P, N1 // 2, N2 * C), BF16 if final else F32),
        compiler_params=_cparams("parallel", "parallel"),
        name="hyena_stage3",
    )(b4d, tabs["m_inv"], gates4d, z4d, bias[None, :], norm_w[None, :])


def _hyena_mixer(uc, g_spec, hy_bias, hy_out_norm_w, tabs):
    _, B, L, C = uc.shape
    N1, N2 = tabs["N1"], tabs["N2"]
    uc4 = uc.reshape(3, B, N1 // 2, N2 * C)
    z4, z_idx = uc4, 2
    for order in range(HY_ORDER):
        a = _hyena_stage1(z4, z_idx, tabs)
        bq = _hyena_stage2(a.reshape(B // 2, 2, N1, N2, C), g_spec, order, tabs)
        z4 = _hyena_stage3(bq.reshape(B // 2, 2, N1, N2 * C), uc4, order, z4, z_idx,
                           hy_bias[order], hy_out_norm_w, tabs, final=(order == HY_ORDER - 1))
        z_idx = 0
    return z4.reshape(B, L, C)


def _attn_kernel(sink_ref, q_ref, kp_ref, kc_ref, kn_ref, vp_ref, vc_ref, vn_ref, nw_ref,
                 o_ref, y_scr, *, L):
    i = pl.program_id(1)
    tq = q_ref.shape[1]
    span = tq + 2 * WINDOW
    k = jnp.concatenate([kp_ref[0], kc_ref[0], kn_ref[0]], axis=0)
    v = jnp.concatenate([vp_ref[0], vc_ref[0], vn_ref[0]], axis=0)
    row = lax.broadcasted_iota(jnp.int32, (tq, span), 0)
    col = lax.broadcasted_iota(jnp.int32, (tq, span), 1)
    rel = col - row
    kpos = i * tq - WINDOW + col
    ok = (rel >= 0) & (rel <= 2 * WINDOW) & (kpos >= 0) & (kpos < L)
    bias = jnp.where(ok, 0.0, NEG_BIG).astype(F32)
    for h in range(N_KV_HEADS):
        kh = k[:, h * HEAD_DIM:(h + 1) * HEAD_DIM]
        vh = v[:, h * HEAD_DIM:(h + 1) * HEAD_DIM]
        for g in range(Q_PER_KV):
            qh_i = h * Q_PER_KV + g
            qh = q_ref[0, :, qh_i * HEAD_DIM:(qh_i + 1) * HEAD_DIM]
            s = lax.dot_general(qh, kh, (((1,), (1,)), ((), ())),
                                preferred_element_type=F32) + bias
            sink = sink_ref[qh_i]
            m = jnp.maximum(jnp.max(s, axis=-1, keepdims=True), sink)
            p = jnp.exp(s - m)
            denom = jnp.sum(p, axis=-1, keepdims=True) + jnp.exp(sink - m)
            o = _dot(p.astype(BF16), vh) / denom
            y_scr[:, qh_i * HEAD_DIM:(qh_i + 1) * HEAD_DIM] = o
    y = y_scr[...]
    inv = lax.rsqrt(jnp.mean(y * y, axis=-1, keepdims=True) + EPS)
    o_ref[0] = (y * inv * nw_ref[...]).astype(o_ref.dtype)


def _windowed_attention(q, k, v, sink, out_norm_w, tq=256):
    B, L, _ = q.shape
    rq = tq // WINDOW
    nwb = L // WINDOW
    cur = lambda b, i, s: (b, i, 0)
    prv = lambda b, i, s: (b, jnp.maximum(i * rq - 1, 0), 0)
    nxt = lambda b, i, s: (b, jnp.minimum((i + 1) * rq, nwb - 1), 0)
    grid_spec = pltpu.PrefetchScalarGridSpec(
        num_scalar_prefetch=1,
        grid=(B, L // tq),
        in_specs=[
            pl.BlockSpec((1, tq, ATTN_WIDTH), cur),
            pl.BlockSpec((1, WINDOW, KV_WIDTH), prv),
            pl.BlockSpec((1, tq, KV_WIDTH), cur),
            pl.BlockSpec((1, WINDOW, KV_WIDTH), nxt),
            pl.BlockSpec((1, WINDOW, KV_WIDTH), prv),
            pl.BlockSpec((1, tq, KV_WIDTH), cur),
            pl.BlockSpec((1, WINDOW, KV_WIDTH), nxt),
            pl.BlockSpec((1, ATTN_WIDTH), lambda b, i, s: (0, 0)),
        ],
        out_specs=pl.BlockSpec((1, tq, ATTN_WIDTH), cur),
        scratch_shapes=[pltpu.VMEM((tq, ATTN_WIDTH), F32)],
    )
    return pl.pallas_call(
        functools.partial(_attn_kernel, L=L),
        grid_spec=grid_spec,
        out_shape=jax.ShapeDtypeStruct((B, L, ATTN_WIDTH), BF16),
        compiler_params=_cparams("parallel", "parallel"),
        name="banded_attention",
    )(sink.astype(F32), q, k, k, k, v, v, v, out_norm_w[None, :])


def _outproj_kernel(x_ref, a_ref, hy_ref, wa_ref, wh_ref, nw_ref, wr_ref, x1_ref, h2_ref, lg_ref):
    x1 = x_ref[...] + _dot(a_ref[...], wa_ref[...]) + _dot(hy_ref[...], wh_ref[...])
    x1_ref[...] = x1
    inv = lax.rsqrt(jnp.mean(x1 * x1, axis=-1, keepdims=True) + EPS)
    h2 = x1 * inv * nw_ref[...]
    h2_ref[...] = h2.astype(BF16)
    h_hi, h_lo = _split_bf16(h2)
    lg_ref[...] = _dot(h_hi, wr_ref[0]) + _dot(h_lo, wr_ref[0]) + _dot(h_hi, wr_ref[1])


def _out_proj(x2d, attn_n, hy_n, w_out_a, w_out_h, norm2_w, w_router, tm=256):
    T = x2d.shape[0]
    row = lambda i: (i, 0)
    const = lambda i: (0, 0)
    return pl.pallas_call(
        _outproj_kernel,
        grid=(T // tm,),
        in_specs=[
            pl.BlockSpec((tm, D_MODEL), row),
            pl.BlockSpec((tm, ATTN_WIDTH), row),
            pl.BlockSpec((tm, HY_WIDTH), row),
            pl.BlockSpec((ATTN_WIDTH, D_MODEL), const),
            pl.BlockSpec((HY_WIDTH, D_MODEL), const),
            pl.BlockSpec((1, D_MODEL), const),
            pl.BlockSpec((2, D_MODEL, ROUTER_PAD), lambda i: (0, 0, 0)),
        ],
        out_specs=[
            pl.BlockSpec((tm, D_MODEL), row),
            pl.BlockSpec((tm, D_MODEL), row),
            pl.BlockSpec((tm, ROUTER_PAD), row),
        ],
        out_shape=[
            jax.ShapeDtypeStruct((T, D_MODEL), F32),
            jax.ShapeDtypeStruct((T, D_MODEL), BF16),
            jax.ShapeDtypeStruct((T, ROUTER_PAD), F32),
        ],
        compiler_params=_cparams("parallel"),
        name="out_proj",
    )(x2d, attn_n, hy_n, w_out_a, w_out_h, norm2_w[None, :], w_router)


def _moe_kernel(be_ref, nu_ref, x_ref, wg_ref, wu_ref, wd_ref, o_ref):
    @pl.when(pl.program_id(0) < nu_ref[0])
    def _():
        x = x_ref[...]
        g = _dot(x, wg_ref[0])
        u = _dot(x, wu_ref[0])
        hid = (g * jax.nn.sigmoid(g)) * u
        o_ref[...] = _dot(hid.astype(BF16), wd_ref[0])


def _expert_blocks(xs, block_expert, n_used, w_gate, w_up, w_down):
    P = xs.shape[0]
    bm = MOE_BLOCK_ROWS
    blk = lambda b, be, nu: (jnp.minimum(b, nu[0] - 1), 0)
    wsel = lambda b, be, nu: (be[jnp.minimum(b, nu[0] - 1)], 0, 0)
    grid_spec = pltpu.PrefetchScalarGridSpec(
        num_scalar_prefetch=2,
        grid=(P // bm,),
        in_specs=[
            pl.BlockSpec((bm, D_MODEL), blk),
            pl.BlockSpec((1, D_MODEL, D_EXPERT), wsel),
            pl.BlockSpec((1, D_MODEL, D_EXPERT), wsel),
            pl.BlockSpec((1, D_EXPERT, D_MODEL), wsel),
        ],
        out_specs=pl.BlockSpec((bm, D_MODEL), blk),
    )
    return pl.pallas_call(
        _moe_kernel,
        grid_spec=grid_spec,
        out_shape=jax.ShapeDtypeStruct((P, D_MODEL), F32),
        compiler_params=_cparams("arbitrary"),
        name="moe_experts",
    )(block_expert, n_used, xs, w_gate, w_up, w_down)


def _hier_moe(x1, h2, logits, b_rg, b_re, w_gate, w_up, w_down):
    T = h2.shape[0]
    g_logits = logits[:, :N_EXPERT_GROUPS] + b_rg
    e_logits = (logits[:, N_EXPERT_GROUPS:N_EXPERT_GROUPS + N_EXPERTS] + b_re)
    e_logits = e_logits.reshape(T, N_EXPERT_GROUPS, EXPERTS_PER_GROUP)
    g_idx = jnp.argmax(g_logits, axis=-1)
    g_gate = jnp.take_along_axis(jax.nn.softmax(g_logits, axis=-1), g_idx[:, None], axis=1)
    e_sel = jnp.take_along_axis(e_logits, g_idx[:, None, None], axis=1)[:, 0]
    top_val, top_loc = lax.top_k(e_sel, TOP_K)
    weights = g_gate * jax.nn.softmax(top_val, axis=-1)
    experts = (g_idx[:, None] * EXPERTS_PER_GROUP + top_loc).astype(jnp.int32)

    A = T * TOP_K
    bm = MOE_BLOCK_ROWS
    flat_e = experts.reshape(-1)
    onehot = (flat_e[:, None] == jnp.arange(N_EXPERTS, dtype=jnp.int32)[None, :]).astype(jnp.int32)
    csum = jnp.cumsum(onehot, axis=0)
    counts = csum[-1]
    rank = jnp.sum((csum - onehot) * onehot, axis=1)
    padded = ((counts + bm - 1) // bm) * bm
    pend = jnp.cumsum(padded)
    pstart = pend - padded
    dest = (pstart[flat_e] + rank).astype(jnp.int32)
    n_blocks = (A + bm - 1) // bm + N_EXPERTS
    P = n_blocks * bm
    buf_tok = jnp.zeros((P,), jnp.int32).at[dest].set(jnp.arange(A, dtype=jnp.int32) // TOP_K)
    block_expert = jnp.minimum(
        jnp.searchsorted(pend, jnp.arange(n_blocks, dtype=jnp.int32) * bm, side="right"),
        N_EXPERTS - 1).astype(jnp.int32)
    n_used = (pend[-1] // bm).astype(jnp.int32)[None]

    xs = jnp.take(h2, buf_tok, axis=0)
    yb = _expert_blocks(xs, block_expert, n_used, w_gate, w_up, w_down)
    d = dest.reshape(T, TOP_K)
    moe = weights[:, 0:1] * jnp.take(yb, d[:, 0], axis=0) + weights[:, 1:2] * jnp.take(yb, d[:, 1], axis=0)
    return x1 + moe


def _encoder_layer(x, p):
    B, L, D = x.shape
    T = B * L
    x2d = x.reshape(T, D)
    q, k, v = _qkv_proj(x2d, L, p["norm1_w"], p["w_qkv"], p["q_norm_w"], p["k_norm_w"])
    attn_n = _windowed_attention(q.reshape(B, L, ATTN_WIDTH), k.reshape(B, L, KV_WIDTH),
                                 v.reshape(B, L, KV_WIDTH), p["attn_sink"], p["attn_out_norm_w"])
    u = _u_proj(x2d, p["norm1_w"], p["w_u"])
    uc = _short_conv(u.reshape(B, L, 3 * HY_WIDTH), p["conv_w"], p["conv_b"])
    tabs = _fft_tables(L)
    g_spec = _hyena_filter_spectrum(L, tabs, p["filt_w1"], p["filt_b1"], p["filt_w2"], p["filt_b2"],
                                    p["filt_w3"], p["filt_b3"], p["filt_w4"], p["filt_freq"])
    hy_n = _hyena_mixer(uc, g_spec, p["hy_bias"], p["hy_out_norm_w"], tabs)
    x1, h2, logits = _out_proj(x2d, attn_n.reshape(T, ATTN_WIDTH), hy_n.reshape(T, HY_WIDTH),
                               p["w_out_a"], p["w_out_h"], p["norm2_w"], p["w_router"])
    out = _hier_moe(x1, h2, logits, p["b_route_group"], p["b_route_expert"],
                    p["w_gate"], p["w_up"], p["w_down"])
    return out.reshape(B, L, D)


def kernel(x_prompt, x_sample, norm1_w, w_in, q_norm_w, k_norm_w, attn_sink, conv_w, conv_b, filt_w1, filt_b1, filt_w2, filt_b2, filt_w3, filt_b3, filt_w4, filt_freq, hy_bias, attn_out_norm_w, hy_out_norm_w, w_out, norm2_w, w_route_group, b_route_group, w_route_expert, b_route_expert, w_gate, w_up, w_down):
    depth = norm1_w.shape[0]

    def layer_params(l):
        w_r = jnp.zeros((D_MODEL, ROUTER_PAD), F32)
        w_r = w_r.at[:, :N_EXPERT_GROUPS].set(w_route_group[l])
        w_r = w_r.at[:, N_EXPERT_GROUPS:N_EXPERT_GROUPS + N_EXPERTS].set(w_route_expert[l])
        r_hi, r_lo = _split_bf16(w_r)
        return dict(
            norm1_w=norm1_w[l], w_qkv=w_in[l][:, :QKV_WIDTH].astype(BF16),
            w_u=w_in[l][:, QKV_WIDTH:].astype(BF16), q_norm_w=q_norm_w[l], k_norm_w=k_norm_w[l],
            attn_sink=attn_sink[l], conv_w=conv_w[l], conv_b=conv_b[l],
            filt_w1=filt_w1[l], filt_b1=filt_b1[l], filt_w2=filt_w2[l], filt_b2=filt_b2[l],
            filt_w3=filt_w3[l], filt_b3=filt_b3[l], filt_w4=filt_w4[l], filt_freq=filt_freq[l],
            hy_bias=hy_bias[l], attn_out_norm_w=attn_out_norm_w[l], hy_out_norm_w=hy_out_norm_w[l],
            w_out_a=w_out[l][:ATTN_WIDTH].astype(BF16), w_out_h=w_out[l][ATTN_WIDTH:].astype(BF16),
            norm2_w=norm2_w[l], w_router=jnp.stack([r_hi, r_lo]),
            b_route_group=b_route_group[l], b_route_expert=b_route_expert[l],
            w_gate=w_gate[l].astype(BF16), w_up=w_up[l].astype(BF16), w_down=w_down[l].astype(BF16))

    params = [layer_params(l) for l in range(depth)]

    def trunk(x):
        for p in params:
            x = _encoder_layer(x, p)
        return x

    return (trunk(x_prompt), trunk(x_sample))
```

```python
import functools
import math

import jax
import jax.numpy as jnp
from jax import lax
from jax.experimental import pallas as pl
from jax.experimental.pallas import tpu as pltpu

F32 = jnp.float32
BF16 = jnp.bfloat16

D_MODEL = 2048
HEAD_DIM = 64
N_Q_HEADS = 16
N_KV_HEADS = 4
Q_PER_KV = N_Q_HEADS // N_KV_HEADS
ATTN_WIDTH = N_Q_HEADS * HEAD_DIM
KV_WIDTH = N_KV_HEADS * HEAD_DIM
QK_WIDTH = ATTN_WIDTH + KV_WIDTH
QKV_WIDTH = ATTN_WIDTH + 2 * KV_WIDTH
WINDOW = 128
ROT_DIM = HEAD_DIM // 4
ROPE_THETA = 500000.0
HY_WIDTH = D_MODEL - ATTN_WIDTH
HY_ORDER = 2
HY_EMB_DIM = 33
HY_FILTER_HIDDEN = 64
HY_DECAY_TARGET = 1e-2
HY_FAST_DECAY_PCT = 0.3
HY_SLOW_DECAY_PCT = 1.5
N_EXPERT_GROUPS = 4
EXPERTS_PER_GROUP = 8
N_EXPERTS = N_EXPERT_GROUPS * EXPERTS_PER_GROUP
TOP_K = 2
D_EXPERT = 1024
EPS = 1e-6

LANES = 128
FFT_N2 = 128
MOE_BLOCK_ROWS = 256
ROUTER_PAD = 128
VMEM_LIMIT = 56 * 1024 * 1024
NEG_BIG = -1e30


def _cparams(*sem):
    return pltpu.CompilerParams(dimension_semantics=sem, vmem_limit_bytes=VMEM_LIMIT)


def _split_bf16(a):
    hi = a.astype(BF16)
    lo = (a - hi.astype(F32)).astype(BF16)
    return hi, lo


def _dot(a, b):
    return jnp.dot(a, b, preferred_element_type=F32)


def _dot3(a, b):
    a_hi, a_lo = _split_bf16(a)
    b_hi, b_lo = _split_bf16(b)
    return _dot(a_hi, b_hi) + _dot(a_lo, b_hi) + _dot(a_hi, b_lo)


def _qkv_kernel(x_ref, nw_ref, w_ref, seg_ref, hw_ref, rc_ref, rs1_ref, rs2_ref,
                q_ref, k_ref, v_ref):
    x = x_ref[...]
    inv = lax.rsqrt(jnp.mean(x * x, axis=-1, keepdims=True) + EPS)
    h = (x * inv * nw_ref[...]).astype(BF16)
    acc = _dot(h, w_ref[...])
    qk = acc[:, :QK_WIDTH]
    hi, lo = _split_bf16(qk * qk)
    seg = seg_ref[...]
    w = seg.shape[0]
    ms = jnp.concatenate(
        [_dot(hi[:, c * w:(c + 1) * w], seg) + _dot(lo[:, c * w:(c + 1) * w], seg)
         for c in range(QK_WIDTH // w)], axis=-1)
    xn = qk * lax.rsqrt(ms + EPS) * hw_ref[...]
    reps = QK_WIDTH // LANES
    rc = jnp.tile(rc_ref[...], (1, reps))
    rs1 = jnp.tile(rs1_ref[...], (1, reps))
    rs2 = jnp.tile(rs2_ref[...], (1, reps))
    half = ROT_DIM // 2
    y = xn * rc + pltpu.roll(xn, QK_WIDTH - half, 1) * rs1 + pltpu.roll(xn, half, 1) * rs2
    q_ref[...] = (y[:, :ATTN_WIDTH] * (HEAD_DIM ** -0.5)).astype(BF16)
    k_ref[...] = y[:, ATTN_WIDTH:].astype(BF16)
    v_ref[...] = acc[:, QK_WIDTH:].astype(BF16)


def _rope_tables(L):
    half = ROT_DIM // 2
    inv_freq = jnp.power(ROPE_THETA, -jnp.arange(half, dtype=F32) * 2.0 / ROT_DIM)
    ang = jnp.arange(L, dtype=F32)[:, None] * inv_freq[None, :]
    d = jnp.arange(LANES) % HEAD_DIM
    ang_l = ang[:, d % half]
    cos, sin = jnp.cos(ang_l), jnp.sin(ang_l)
    rc = jnp.where(d < ROT_DIM, cos, 1.0)
    rs1 = jnp.where(d < half, -sin, 0.0)
    rs2 = jnp.where((d >= half) & (d < ROT_DIM), sin, 0.0)
    return rc.astype(F32), rs1.astype(F32), rs2.astype(F32)


def _qkv_proj(x2d, L, norm1_w, w_qkv, q_norm_w, k_norm_w, tm=256):
    T = x2d.shape[0]
    seg_w = 256
    seg = (jnp.arange(seg_w)[:, None] // HEAD_DIM == jnp.arange(seg_w)[None, :] // HEAD_DIM)
    seg = (seg.astype(F32) / HEAD_DIM).astype(BF16)
    hw = jnp.concatenate([jnp.tile(q_norm_w, N_Q_HEADS), jnp.tile(k_norm_w, N_KV_HEADS)])[None, :]
    rc, rs1, rs2 = _rope_tables(L)
    lt = L // tm
    row = lambda i: (i, 0)
    const = lambda i: (0, 0)
    pos = lambda i: (i % lt, 0)
    return pl.pallas_call(
        _qkv_kernel,
        grid=(T // tm,),
        in_specs=[
            pl.BlockSpec((tm, D_MODEL), row),
            pl.BlockSpec((1, D_MODEL), const),
            pl.BlockSpec((D_MODEL, QKV_WIDTH), const),
            pl.BlockSpec((seg_w, seg_w), const),
            pl.BlockSpec((1, QK_WIDTH), const),
            pl.BlockSpec((tm, LANES), pos),
            pl.BlockSpec((tm, LANES), pos),
            pl.BlockSpec((tm, LANES), pos),
        ],
        out_specs=[
            pl.BlockSpec((tm, ATTN_WIDTH), row),
            pl.BlockSpec((tm, KV_WIDTH), row),
            pl.BlockSpec((tm, KV_WIDTH), row),
        ],
        out_shape=[
            jax.ShapeDtypeStruct((T, ATTN_WIDTH), BF16),
            jax.ShapeDtypeStruct((T, KV_WIDTH), BF16),
            jax.ShapeDtypeStruct((T, KV_WIDTH), BF16),
        ],
        compiler_params=_cparams("parallel"),
        name="qkv_proj",
    )(x2d, norm1_w[None, :], w_qkv, seg, hw, rc, rs1, rs2)


def _uproj_kernel(x_ref, nw_ref, w_ref, o_ref, h_scr):
    @pl.when(pl.program_id(1) == 0)
    def _():
        x = x_ref[...]
        inv = lax.rsqrt(jnp.mean(x * x, axis=-1, keepdims=True) + EPS)
        h_scr[...] = (x * inv * nw_ref[...]).astype(BF16)

    o_ref[...] = _dot(h_scr[...], w_ref[...])


def _u_proj(x2d, norm1_w, w_u, tm=1024, tn=1024):
    T = x2d.shape[0]
    n = w_u.shape[1]
    return pl.pallas_call(
        _uproj_kernel,
        grid=(T // tm, n // tn),
        in_specs=[
            pl.BlockSpec((tm, D_MODEL), lambda i, j: (i, 0)),
            pl.BlockSpec((1, D_MODEL), lambda i, j: (0, 0)),
            pl.BlockSpec((D_MODEL, tn), lambda i, j: (0, j)),
        ],
        out_specs=pl.BlockSpec((tm, tn), lambda i, j: (i, j)),
        out_shape=jax.ShapeDtypeStruct((T, n), F32),
        scratch_shapes=[pltpu.VMEM((tm, D_MODEL), BF16)],
        compiler_params=_cparams("parallel", "arbitrary"),
        name="u_proj",
    )(x2d, norm1_w[None, :], w_u)


def _sconv_kernel(u_ref, up_ref, un_ref, w_ref, b_ref, o_ref):
    i = pl.program_id(1)
    last = pl.num_programs(1) - 1
    u = u_ref[0]
    tl = u.shape[0]
    prev_row = jnp.where(i > 0, up_ref[0, 7:8, :], 0.0)
    next_row = jnp.where(i < last, un_ref[0, 0:1, :], 0.0)
    rid = lax.broadcasted_iota(jnp.int32, u.shape, 0)
    um = jnp.where(rid == 0, prev_row, pltpu.roll(u, 1, 0))
    up = jnp.where(rid == tl - 1, next_row, pltpu.roll(u, tl - 1, 0))
    w = w_ref[...]
    o_ref[0, 0] = w[0:1] * um + w[1:2] * u + w[2:3] * up + b_ref[...]


def _short_conv(u3d, conv_w, conv_b, tl=1024):
    B, L, W = u3d.shape
    C = HY_WIDTH
    nparts = W // C
    r8 = tl // 8
    return pl.pallas_call(
        _sconv_kernel,
        grid=(B, L // tl, nparts),
        in_specs=[
            pl.BlockSpec((1, tl, C), lambda b, i, c: (b, i, c)),
            pl.BlockSpec((1, 8, C), lambda b, i, c: (b, jnp.maximum(i * r8 - 1, 0), c)),
            pl.BlockSpec((1, 8, C), lambda b, i, c: (b, jnp.minimum((i + 1) * r8, L // 8 - 1), c)),
            pl.BlockSpec((3, C), lambda b, i, c: (0, c)),
            pl.BlockSpec((1, C), lambda b, i, c: (0, c)),
        ],
        out_specs=pl.BlockSpec((1, 1, tl, C), lambda b, i, c: (c, b, i, 0)),
        out_shape=jax.ShapeDtypeStruct((nparts, B, L, C), F32),
        compiler_params=_cparams("parallel", "parallel", "parallel"),
        name="short_conv",
    )(u3d, u3d, u3d, conv_w, conv_b[None, :])


def _fft_tables(L):
    N = 2 * L
    N2 = FFT_N2
    N1 = N // N2
    H = N1 // 2
    k1 = jnp.arange(N1, dtype=jnp.int32)
    n2 = jnp.arange(N2, dtype=jnp.int32)
    a1 = ((k1[:, None] * k1[None, :]) % N1).astype(F32) * (2.0 * math.pi / N1)
    c1, s1 = jnp.cos(a1), jnp.sin(a1)
    at = (n2[:, None] * k1[None, :]).astype(F32) * (2.0 * math.pi / N)
    ct, st = jnp.cos(at), jnp.sin(at)
    c = c1[None] * ct[:, :, None] - s1[None] * st[:, :, None]
    s = s1[None] * ct[:, :, None] + c1[None] * st[:, :, None]
    m_filt = jnp.concatenate([c, -s], axis=1).astype(BF16)
    ch, sh = c[:, :, :H], s[:, :, :H]
    m_fwd = jnp.concatenate([jnp.concatenate([ch, sh], axis=2),
                             jnp.concatenate([-sh, ch], axis=2)], axis=1)
    cT = c1[None, :H] * ct[:, None, :] - s1[None, :H] * st[:, None, :]
    sT = s1[None, :H] * ct[:, None, :] + c1[None, :H] * st[:, None, :]
    m_inv = jnp.concatenate([jnp.concatenate([cT, -sT], axis=2),
                             jnp.concatenate([sT, cT], axis=2)], axis=1)
    k2 = jnp.arange(N2, dtype=jnp.int32)
    ph = ((k2[:, None] * n2[None, :]) % N2).astype(F32) * (2.0 * math.pi / N2)
    c2, s2 = jnp.cos(ph), jnp.sin(ph)
    m2_fwd = jnp.concatenate([jnp.concatenate([c2, s2], axis=1),
                              jnp.concatenate([-s2, c2], axis=1)], axis=0)
    m2_inv = m2_fwd.T
    return dict(N1=N1, N2=N2, m_filt=m_filt, m_fwd=m_fwd.astype(BF16), m_inv=m_inv.astype(BF16),
                m2_fwd=m2_fwd.astype(BF16), m2_inv=m2_inv.astype(BF16))


def _filt1_kernel(w1_ref, b1_ref, w2_ref, b2_ref, w3_ref, b3_ref, w4_ref, a_ref, frl_ref,
                  dl_ref, m_ref, o_ref, sabs_ref, *, L, N1, N2, nb):
    j = pl.program_id(0)
    C2 = HY_ORDER * HY_WIDTH
    H = N1 // 2
    rows = nb * N1
    ridx = lax.broadcasted_iota(jnp.int32, (rows, LANES), 0)
    lane = lax.broadcasted_iota(jnp.int32, (rows, LANES), 1)
    n2 = j * nb + ridx // N1
    n1 = ridx % N1
    r = n1 * N2 + n2
    lag = jnp.where(r < L, r, 2 * L - r)
    valid = (r != L).astype(F32)
    lagf = jnp.minimum(lag, L - 1).astype(F32)
    t = lagf / (L - 1)
    wpos = lagf * (2.0 * math.pi / L)
    phase = wpos * frl_ref[...]
    bands = (HY_EMB_DIM - 1) // 2
    zemb = jnp.where(lane == 0, t,
                     jnp.where(lane <= bands, jnp.cos(phase),
                               jnp.where(lane <= 2 * bands, -jnp.sin(phase), 0.0)))
    a = a_ref[...]
    h = jnp.sin(a * (_dot3(zemb, w1_ref[...]) + b1_ref[...]))
    h = jnp.sin(a * (_dot3(h, w2_ref[...]) + b2_ref[...]))
    h = jnp.sin(a * (_dot3(h, w3_ref[...]) + b3_ref[...]))
    h_hi, h_lo = _split_bf16(h)
    w4 = w4_ref[...]
    decay = jnp.exp(-jnp.tile(t, (1, HY_WIDTH // LANES)) * dl_ref[...])
    decay = decay * jnp.tile(valid, (1, HY_WIDTH // LANES))
    decay2 = jnp.tile(decay, (1, HY_ORDER))

    @pl.when(j == 0)
    def _():
        sabs_ref[...] = jnp.zeros_like(sabs_ref)

    tot = jnp.zeros((8, C2), F32)
    for l in range(nb):
        fs = slice(l * N1, l * N1 + H)
        bs = slice(l * N1 + H, (l + 1) * N1)
        gf = _dot(h_hi[fs], w4[:, :C2]) + _dot(h_lo[fs], w4[:, :C2])
        gb = _dot(h_hi[bs], w4[:, C2:]) + _dot(h_lo[bs], w4[:, C2:])
        g = jnp.concatenate([gf, gb], axis=0) * decay2[l * N1:(l + 1) * N1]
        tot = tot + jnp.sum(jnp.abs(g).reshape(N1 // 8, 8, C2), axis=0)
        y = _dot(m_ref[l], g.astype(BF16))
        o_ref[0, :, l * C2:(l + 1) * C2] = y[:N1]
        o_ref[1, :, l * C2:(l + 1) * C2] = y[N1:]
    sabs_ref[...] += tot


def _filt2_kernel(a_ref, m_ref, sc_ref, o_ref):
    kb = a_ref.shape[1]
    N2 = a_ref.shape[2]
    for kl in range(kb):
        a = jnp.concatenate([a_ref[0, kl], a_ref[1, kl]], axis=0).astype(BF16)
        x = _dot(m_ref[...], a) * sc_ref[...]
        o_ref[0, kl] = x[:N2]
        o_ref[1, kl] = x[N2:]


def _hyena_filter_spectrum(L, tabs, filt_w1, filt_b1, filt_w2, filt_b2, filt_w3, filt_b3,
                           filt_w4, filt_freq):
    N1, N2 = tabs["N1"], tabs["N2"]
    N = N1 * N2
    C2 = HY_ORDER * HY_WIDTH
    nb = max(1, 512 // N1)
    bands = (HY_EMB_DIM - 1) // 2
    fr = jnp.linspace(1e-4, bands - 1, bands, dtype=F32)
    frl = jnp.zeros((LANES,), F32).at[1:1 + bands].set(fr).at[1 + bands:1 + 2 * bands].set(fr)[None, :]
    w1p = jnp.zeros((LANES, HY_FILTER_HIDDEN), F32).at[:HY_EMB_DIM].set(filt_w1)
    max_decay = math.log(HY_DECAY_TARGET) / HY_FAST_DECAY_PCT
    min_decay = math.log(HY_DECAY_TARGET) / HY_SLOW_DECAY_PCT
    deltas = jnp.abs(jnp.linspace(min_decay, max_decay, HY_WIDTH, dtype=F32))[None, :]
    full = lambda shape: pl.BlockSpec(shape, lambda j: (0,) * len(shape))
    a_g, sabs = pl.pallas_call(
        functools.partial(_filt1_kernel, L=L, N1=N1, N2=N2, nb=nb),
        grid=(N2 // nb,),
        in_specs=[
            full((LANES, HY_FILTER_HIDDEN)), full((1, HY_FILTER_HIDDEN)),
            full((HY_FILTER_HIDDEN, HY_FILTER_HIDDEN)), full((1, HY_FILTER_HIDDEN)),
            full((HY_FILTER_HIDDEN, HY_FILTER_HIDDEN)), full((1, HY_FILTER_HIDDEN)),
            full((HY_FILTER_HIDDEN, 2 * C2)), full((1, HY_FILTER_HIDDEN)),
            full((1, LANES)), full((1, HY_WIDTH)),
            pl.BlockSpec((nb, 2 * N1, N1), lambda j: (j, 0, 0)),
        ],
        out_specs=[
            pl.BlockSpec((2, N1, nb * C2), lambda j: (0, 0, j)),
            pl.BlockSpec((8, C2), lambda j: (0, 0)),
        ],
        out_shape=[
            jax.ShapeDtypeStruct((2, N1, N2 * C2), F32),
            jax.ShapeDtypeStruct((8, C2), F32),
        ],
        compiler_params=_cparams("arbitrary"),
        name="hyena_filter_stage1",
    )(w1p, filt_b1[None, :], filt_w2, filt_b2[None, :], filt_w3, filt_b3[None, :],
      filt_w4.astype(BF16), filt_freq[None, :], frl, deltas, tabs["m_filt"])
    scale = (1.0 / (jnp.sum(sabs, axis=0) * N))[None, :]
    kb = 4
    tc = HY_WIDTH
    return pl.pallas_call(
        _filt2_kernel,
        grid=(N1 // kb, C2 // tc),
        in_specs=[
            pl.BlockSpec((2, kb, N2, tc), lambda i, c: (0, i, 0, c)),
            pl.BlockSpec((2 * N2, 2 * N2), lambda i, c: (0, 0)),
            pl.BlockSpec((1, tc), lambda i, c: (0, c)),
        ],
        out_specs=pl.BlockSpec((2, kb, N2, tc), lambda i, c: (0, i, 0, c)),
        out_shape=jax.ShapeDtypeStruct((2, N1, N2, C2), F32),
        compiler_params=_cparams("parallel", "parallel"),
        name="hyena_filter_stage2",
    )(a_g.reshape(2, N1, N2, C2), tabs["m2_fwd"], scale)


def _fwd1_kernel(x_ref, m_ref, o_ref):
    nb = m_ref.shape[0]
    N1 = m_ref.shape[2]
    C = HY_WIDTH
    for l in range(nb):
        xl = x_ref[0, :, :, l * C:(l + 1) * C].reshape(N1, C).astype(BF16)
        y = _dot(m_ref[l], xl)
        o_ref[0, 0, :, l * C:(l + 1) * C] = y[:N1]
        o_ref[0, 1, :, l * C:(l + 1) * C] = y[N1:]


def _hyena_stage1(z4d, which, tabs):
    N1, N2 = tabs["N1"], tabs["N2"]
    C = HY_WIDTH
    B = z4d.shape[1]
    nb = max(1, 1024 // N1)
    return pl.pallas_call(
        _fwd1_kernel,
        grid=(B // 2, N2 // nb),
        in_specs=[
            pl.BlockSpec((1, 2, N1 // 2, nb * C), lambda p, j: (which, p, 0, j)),
            pl.BlockSpec((nb, 2 * N1, N1), lambda p, j: (j, 0, 0)),
        ],
        out_specs=pl.BlockSpec((1, 2, N1, nb * C), lambda p, j: (p, 0, 0, j)),
        out_shape=jax.ShapeDtypeStruct((B // 2, 2, N1, N2 * C), F32),
        compiler_params=_cparams("parallel", "parallel"),
        name="hyena_stage1",
    )(z4d, tabs["m_fwd"])


def _mid_kernel(a_ref, g_ref, mf_ref, mi_ref, o_ref):
    kb = a_ref.shape[2]
    N2 = a_ref.shape[3]
    for kl in range(kb):
        a = jnp.concatenate([a_ref[0, 0, kl], a_ref[0, 1, kl]], axis=0).astype(BF16)
        x = _dot(mf_ref[...], a)
        xr, xi = x[:N2], x[N2:]
        gr, gi = g_ref[0, kl], g_ref[1, kl]
        y = jnp.concatenate([xr * gr - xi * gi, xr * gi + xi * gr], axis=0).astype(BF16)
        b = _dot(mi_ref[...], y)
        o_ref[0, 0, kl] = b[:N2]
        o_ref[0, 1, kl] = b[N2:]


def _hyena_stage2(a5d, g4d, order, tabs, kb=4):
    N1, N2 = tabs["N1"], tabs["N2"]
    C = HY_WIDTH
    P = a5d.shape[0]
    blk = (1, 2, kb, N2, C)
    return pl.pallas_call(
        _mid_kernel,
        grid=(N1 // kb, P),
        in_specs=[
            pl.BlockSpec(blk, lambda i, p: (p, 0, i, 0, 0)),
            pl.BlockSpec((2, kb, N2, C), lambda i, p: (0, i, 0, order)),
            pl.BlockSpec((2 * N2, 2 * N2), lambda i, p: (0, 0)),
            pl.BlockSpec((2 * N2, 2 * N2), lambda i, p: (0, 0)),
        ],
        out_specs=pl.BlockSpec(blk, lambda i, p: (p, 0, i, 0, 0)),
        out_shape=jax.ShapeDtypeStruct(a5d.shape, F32),
        compiler_params=_cparams("parallel", "arbitrary"),
        name="hyena_stage2",
    )(a5d, g4d, tabs["m2_fwd"], tabs["m2_inv"])


def _inv1_kernel(b_ref, m_ref, gate_ref, z_ref, bias_ref, nw_ref, o_ref, *, final):
    nb = m_ref.shape[0]
    N1 = m_ref.shape[1]
    C = HY_WIDTH
    for l in range(nb):
        cs = slice(l * C, (l + 1) * C)
        b = jnp.concatenate([b_ref[0, 0, :, cs], b_ref[0, 1, :, cs]], axis=0).astype(BF16)
        conv = _dot(m_ref[l], b).reshape(2, N1 // 2, C)
        out = gate_ref[0, :, :, cs] * (conv + z_ref[0, :, :, cs] * bias_ref[...])
        if final:
            inv = lax.rsqrt(jnp.mean(out * out, axis=-1, keepdims=True) + EPS)
            out = out * inv * nw_ref[...]
        o_ref[0, :, :, cs] = out.astype(o_ref.dtype)


def _hyena_stage3(b4d, gates4d, gate_idx, z4d, z_idx, bias, norm_w, tabs, final):
    N1, N2 = tabs["N1"], tabs["N2"]
    C = HY_WIDTH
    P = b4d.shape[0]
    nb = max(1, 512 // N1)
    half_blk = (1, 2, N1 // 2, nb * C)
    return pl.pallas_call(
        functools.partial(_inv1_kernel, final=final),
        grid=(P, N2 // nb),
        in_specs=[
            pl.BlockSpec((1, 2, N1, nb * C), lambda p, j: (p, 0, 0, j)),
            pl.BlockSpec((nb, N1, 2 * N1), lambda p, j: (j, 0, 0)),
            pl.BlockSpec(half_blk, lambda p, j: (gate_idx, p, 0, j)),
            pl.BlockSpec(half_blk, lambda p, j: (z_idx, p, 0, j)),
            pl.BlockSpec((1, C), lambda p, j: (0, 0)),
            pl.BlockSpec((1, C), lambda p, j: (0, 0)),
        ],
        out_specs=pl.BlockSpec(half_blk, lambda p, j: (0, p, 0, j)),
        out_shape=jax.ShapeDtypeStruct((1, 2 * P, N1 // 2, N2 * C), BF16 if final else F32),
        compiler_params=_cparams("parallel", "parallel"),
        name="hyena_stage3",
    )(b4d, tabs["m_inv"], gates4d, z4d, bias[None, :], norm_w[None, :])


def _hyena_mixer(uc, g_spec, hy_bias, hy_out_norm_w, tabs):
    _, B, L, C = uc.shape
    N1, N2 = tabs["N1"], tabs["N2"]
    uc4 = uc.reshape(3, B, N1 // 2, N2 * C)
    z4, z_idx = uc4, 2
    for order in range(HY_ORDER):
        a = _hyena_stage1(z4, z_idx, tabs)
        bq = _hyena_stage2(a.reshape(B // 2, 2, N1, N2, C), g_spec, order, tabs)
        z4 = _hyena_stage3(bq.reshape(B // 2, 2, N1, N2 * C), uc4, order, z4, z_idx,
                           hy_bias[order], hy_out_norm_w, tabs, final=(order == HY_ORDER - 1))
        z_idx = 0
    return z4.reshape(B, L, C)


def _attn_kernel(sink_ref, q_ref, kp_ref, kc_ref, kn_ref, vp_ref, vc_ref, vn_ref, nw_ref,
                 o_ref, y_scr, *, L):
    i = pl.program_id(1)
    tq = q_ref.shape[1]
    span = tq + 2 * WINDOW
    k = jnp.concatenate([kp_ref[0], kc_ref[0], kn_ref[0]], axis=0)
    v = jnp.concatenate([vp_ref[0], vc_ref[0], vn_ref[0]], axis=0)
    row = lax.broadcasted_iota(jnp.int32, (tq, span), 0)
    col = lax.broadcasted_iota(jnp.int32, (tq, span), 1)
    rel = col - row
    kpos = i * tq - WINDOW + col
    ok = (rel >= 0) & (rel <= 2 * WINDOW) & (kpos >= 0) & (kpos < L)
    bias = jnp.where(ok, 0.0, NEG_BIG).astype(F32)
    for h in range(N_KV_HEADS):
        kh = k[:, h * HEAD_DIM:(h + 1) * HEAD_DIM]
        vh = v[:, h * HEAD_DIM:(h + 1) * HEAD_DIM]
        for g in range(Q_PER_KV):
            qh_i = h * Q_PER_KV + g
            qh = q_ref[0, :, qh_i * HEAD_DIM:(qh_i + 1) * HEAD_DIM]
            s = lax.dot_general(qh, kh, (((1,), (1,)), ((), ())),
                                preferred_element_type=F32) + bias
            sink = sink_ref[qh_i]
            m = jnp.maximum(jnp.max(s, axis=-1, keepdims=True), sink)
            p = jnp.exp(s - m)
            denom = jnp.sum(p, axis=-1, keepdims=True) + jnp.exp(sink - m)
            o = _dot(p.astype(BF16), vh) / denom
            y_scr[:, qh_i * HEAD_DIM:(qh_i + 1) * HEAD_DIM] = o
    y = y_scr[...]
    inv = lax.rsqrt(jnp.mean(y * y, axis=-1, keepdims=True) + EPS)
    o_ref[0] = (y * inv * nw_ref[...]).astype(o_ref.dtype)


def _windowed_attention(q, k, v, sink, out_norm_w, tq=256):
    B, L, _ = q.shape
    rq = tq // WINDOW
    nwb = L // WINDOW
    cur = lambda b, i, s: (b, i, 0)
    prv = lambda b, i, s: (b, jnp.maximum(i * rq - 1, 0), 0)
    nxt = lambda b, i, s: (b, jnp.minimum((i + 1) * rq, nwb - 1), 0)
    grid_spec = pltpu.PrefetchScalarGridSpec(
        num_scalar_prefetch=1,
        grid=(B, L // tq),
        in_specs=[
            pl.BlockSpec((1, tq, ATTN_WIDTH), cur),
            pl.BlockSpec((1, WINDOW, KV_WIDTH), prv),
            pl.BlockSpec((1, tq, KV_WIDTH), cur),
            pl.BlockSpec((1, WINDOW, KV_WIDTH), nxt),
            pl.BlockSpec((1, WINDOW, KV_WIDTH), prv),
            pl.BlockSpec((1, tq, KV_WIDTH), cur),
            pl.BlockSpec((1, WINDOW, KV_WIDTH), nxt),
            pl.BlockSpec((1, ATTN_WIDTH), lambda b, i, s: (0, 0)),
        ],
        out_specs=pl.BlockSpec((1, tq, ATTN_WIDTH), cur),
        scratch_shapes=[pltpu.VMEM((tq, ATTN_WIDTH), F32)],
    )
    return pl.pallas_call(
        functools.partial(_attn_kernel, L=L),
        grid_spec=grid_spec,
        out_shape=jax.ShapeDtypeStruct((B, L, ATTN_WIDTH), BF16),
        compiler_params=_cparams("parallel", "parallel"),
        name="banded_attention",
    )(sink.astype(F32), q, k, k, k, v, v, v, out_norm_w[None, :])


def _outproj_kernel(x_ref, a_ref, hy_ref, wa_ref, wh_ref, nw_ref, wr_ref, x1_ref, h2_ref, lg_ref):
    x1 = x_ref[...] + _dot(a_ref[...], wa_ref[...]) + _dot(hy_ref[...], wh_ref[...])
    x1_ref[...] = x1
    inv = lax.rsqrt(jnp.mean(x1 * x1, axis=-1, keepdims=True) + EPS)
    h2 = x1 * inv * nw_ref[...]
    h2_ref[...] = h2.astype(BF16)
    h_hi, h_lo = _split_bf16(h2)
    lg_ref[...] = _dot(h_hi, wr_ref[0]) + _dot(h_lo, wr_ref[0]) + _dot(h_hi, wr_ref[1])


def _out_proj(x2d, attn_n, hy_n, w_out_a, w_out_h, norm2_w, w_router, tm=256):
    T = x2d.shape[0]
    row = lambda i: (i, 0)
    const = lambda i: (0, 0)
    return pl.pallas_call(
        _outproj_kernel,
        grid=(T // tm,),
        in_specs=[
            pl.BlockSpec((tm, D_MODEL), row),
            pl.BlockSpec((tm, ATTN_WIDTH), row),
            pl.BlockSpec((tm, HY_WIDTH), row),
            pl.BlockSpec((ATTN_WIDTH, D_MODEL), const),
            pl.BlockSpec((HY_WIDTH, D_MODEL), const),
            pl.BlockSpec((1, D_MODEL), const),
            pl.BlockSpec((2, D_MODEL, ROUTER_PAD), lambda i: (0, 0, 0)),
        ],
        out_specs=[
            pl.BlockSpec((tm, D_MODEL), row),
            pl.BlockSpec((tm, D_MODEL), row),
            pl.BlockSpec((tm, ROUTER_PAD), row),
        ],
        out_shape=[
            jax.ShapeDtypeStruct((T, D_MODEL), F32),
            jax.ShapeDtypeStruct((T, D_MODEL), BF16),
            jax.ShapeDtypeStruct((T, ROUTER_PAD), F32),
        ],
        compiler_params=_cparams("parallel"),
        name="out_proj",
    )(x2d, attn_n, hy_n, w_out_a, w_out_h, norm2_w[None, :], w_router)


def _moe_kernel(be_ref, nu_ref, x_ref, wg_ref, wu_ref, wd_ref, o_ref):
    @pl.when(pl.program_id(0) < nu_ref[0])
    def _():
        x = x_ref[...]
        g = _dot(x, wg_ref[0])
        u = _dot(x, wu_ref[0])
        hid = (g * jax.nn.sigmoid(g)) * u
        o_ref[...] = _dot(hid.astype(BF16), wd_ref[0]).astype(o_ref.dtype)


def _expert_blocks(xs, block_expert, n_used, w_gate, w_up, w_down):
    P = xs.shape[0]
    bm = MOE_BLOCK_ROWS
    blk = lambda b, be, nu: (jnp.minimum(b, nu[0] - 1), 0)
    wsel = lambda b, be, nu: (be[jnp.minimum(b, nu[0] - 1)], 0, 0)
    grid_spec = pltpu.PrefetchScalarGridSpec(
        num_scalar_prefetch=2,
        grid=(P // bm,),
        in_specs=[
            pl.BlockSpec((bm, D_MODEL), blk),
            pl.BlockSpec((1, D_MODEL, D_EXPERT), wsel),
            pl.BlockSpec((1, D_MODEL, D_EXPERT), wsel),
            pl.BlockSpec((1, D_EXPERT, D_MODEL), wsel),
        ],
        out_specs=pl.BlockSpec((bm, D_MODEL), blk),
    )
    return pl.pallas_call(
        _moe_kernel,
        grid_spec=grid_spec,
        out_shape=jax.ShapeDtypeStruct((P, D_MODEL), BF16),
        compiler_params=_cparams("arbitrary"),
        name="moe_experts",
    )(block_expert, n_used, xs, w_gate, w_up, w_down)


def _hier_moe(x1, h2, logits, b_rg, b_re, w_gate, w_up, w_down):
    T = h2.shape[0]
    g_logits = logits[:, :N_EXPERT_GROUPS] + b_rg
    e_logits = (logits[:, N_EXPERT_GROUPS:N_EXPERT_GROUPS + N_EXPERTS] + b_re)
    e_logits = e_logits.reshape(T, N_EXPERT_GROUPS, EXPERTS_PER_GROUP)
    g_idx = jnp.argmax(g_logits, axis=-1)
    g_gate = jnp.take_along_axis(jax.nn.softmax(g_logits, axis=-1), g_idx[:, None], axis=1)
    e_sel = jnp.take_along_axis(e_logits, g_idx[:, None, None], axis=1)[:, 0]
    top_val, top_loc = lax.top_k(e_sel, TOP_K)
    weights = g_gate * jax.nn.softmax(top_val, axis=-1)
    experts = (g_idx[:, None] * EXPERTS_PER_GROUP + top_loc).astype(jnp.int32)

    A = T * TOP_K
    bm = MOE_BLOCK_ROWS
    flat_e = experts.reshape(-1)
    onehot = (flat_e[:, None] == jnp.arange(N_EXPERTS, dtype=jnp.int32)[None, :]).astype(jnp.int32)
    csum = jnp.cumsum(onehot, axis=0)
    counts = csum[-1]
    rank = jnp.sum((csum - onehot) * onehot, axis=1)
    padded = ((counts + bm - 1) // bm) * bm
    pend = jnp.cumsum(padded)
    pstart = pend - padded
    seg_start = jnp.cumsum(counts) - counts
    dest = (pstart[flat_e] + rank).astype(jnp.int32)
    n_blocks = (A + bm - 1) // bm + N_EXPERTS
    P = n_blocks * bm
    order = jnp.argsort(flat_e, stable=True).astype(jnp.int32)
    q = jnp.arange(P, dtype=jnp.int32)
    e_q = jnp.minimum(jnp.sum((pend[None, :] <= q[:, None]).astype(jnp.int32), axis=1), N_EXPERTS - 1)
    r_q = jnp.minimum(q - pstart[e_q], jnp.maximum(counts[e_q] - 1, 0))
    src = jnp.clip(seg_start[e_q] + r_q, 0, A - 1)
    buf_tok = order.at[src].get(mode="promise_in_bounds") // TOP_K
    block_expert = e_q.reshape(n_blocks, bm)[:, 0]
    n_used = (pend[-1] // bm).astype(jnp.int32)[None]

    xs = h2.at[buf_tok].get(mode="promise_in_bounds")
    yb = _expert_blocks(xs, block_expert, n_used, w_gate, w_up, w_down)
    d = dest.reshape(T, TOP_K)
    y0 = yb.at[d[:, 0]].get(mode="promise_in_bounds").astype(F32)
    y1 = yb.at[d[:, 1]].get(mode="promise_in_bounds").astype(F32)
    return x1 + (weights[:, 0:1] * y0 + weights[:, 1:2] * y1)


def _encoder_layer(x, p):
    B, L, D = x.shape
    T = B * L
    x2d = x.reshape(T, D)
    q, k, v = _qkv_proj(x2d, L, p["norm1_w"], p["w_qkv"], p["q_norm_w"], p["k_norm_w"])
    attn_n = _windowed_attention(q.reshape(B, L, ATTN_WIDTH), k.reshape(B, L, KV_WIDTH),
                                 v.reshape(B, L, KV_WIDTH), p["attn_sink"], p["attn_out_norm_w"])
    u = _u_proj(x2d, p["norm1_w"], p["w_u"])
    uc = _short_conv(u.reshape(B, L, 3 * HY_WIDTH), p["conv_w"], p["conv_b"])
    tabs = _fft_tables(L)
    g_spec = _hyena_filter_spectrum(L, tabs, p["filt_w1"], p["filt_b1"], p["filt_w2"], p["filt_b2"],
                                    p["filt_w3"], p["filt_b3"], p["filt_w4"], p["filt_freq"])
    hy_n = _hyena_mixer(uc, g_spec, p["hy_bias"], p["hy_out_norm_w"], tabs)
    x1, h2, logits = _out_proj(x2d, attn_n.reshape(T, ATTN_WIDTH), hy_n.reshape(T, HY_WIDTH),
                               p["w_out_a"], p["w_out_h"], p["norm2_w"], p["w_router"])
    out = _hier_moe(x1, h2, logits, p["b_route_group"], p["b_route_expert"],
                    p["w_gate"], p["w_up"], p["w_down"])
    return out.reshape(B, L, D)


def kernel(x_prompt, x_sample, norm1_w, w_in, q_norm_w, k_norm_w, attn_sink, conv_w, conv_b, filt_w1, filt_b1, filt_w2, filt_b2, filt_w3, filt_b3, filt_w4, filt_freq, hy_bias, attn_out_norm_w, hy_out_norm_w, w_out, norm2_w, w_route_group, b_route_group, w_route_expert, b_route_expert, w_gate, w_up, w_down):
    depth = norm1_w.shape[0]

    def layer_params(l):
        w_r = jnp.zeros((D_MODEL, ROUTER_PAD), F32)
        w_r = w_r.at[:, :N_EXPERT_GROUPS].set(w_route_group[l])
        w_r = w_r.at[:, N_EXPERT_GROUPS:N_EXPERT_GROUPS + N_EXPERTS].set(w_route_expert[l])
        r_hi, r_lo = _split_bf16(w_r)
        return dict(
            norm1_w=norm1_w[l], w_qkv=w_in[l][:, :QKV_WIDTH].astype(BF16),
            w_u=w_in[l][:, QKV_WIDTH:].astype(BF16), q_norm_w=q_norm_w[l], k_norm_w=k_norm_w[l],
            attn_sink=attn_sink[l], conv_w=conv_w[l], conv_b=conv_b[l],
            filt_w1=filt_w1[l], filt_b1=filt_b1[l], filt_w2=filt_w2[l], filt_b2=filt_b2[l],
            filt_w3=filt_w3[l], filt_b3=filt_b3[l], filt_w4=filt_w4[l], filt_freq=filt_freq[l],
            hy_bias=hy_bias[l], attn_out_norm_w=attn_out_norm_w[l], hy_out_norm_w=hy_out_norm_w[l],
            w_out_a=w_out[l][:ATTN_WIDTH].astype(BF16), w_out_h=w_out[l][ATTN_WIDTH:].astype(BF16),
            norm2_w=norm2_w[l], w_router=jnp.stack([r_hi, r_lo]),
            b_route_group=b_route_group[l], b_route_expert=b_route_expert[l],
            w_gate=w_gate[l].astype(BF16), w_up=w_up[l].astype(BF16), w_down=w_down[l].astype(BF16))

    params = [layer_params(l) for l in range(depth)]

    def trunk(x):
        for p in params:
            x = _encoder_layer(x, p)
        return x

    return (trunk(x_prompt), trunk(x_sample))
```

```python
import functools
import math

import jax
import jax.numpy as jnp
from jax import lax
from jax.experimental import pallas as pl
from jax.experimental.pallas import tpu as pltpu

F32 = jnp.float32
BF16 = jnp.bfloat16

D_MODEL = 2048
HEAD_DIM = 64
N_Q_HEADS = 16
N_KV_HEADS = 4
Q_PER_KV = N_Q_HEADS // N_KV_HEADS
ATTN_WIDTH = N_Q_HEADS * HEAD_DIM
KV_WIDTH = N_KV_HEADS * HEAD_DIM
QK_WIDTH = ATTN_WIDTH + KV_WIDTH
QKV_WIDTH = ATTN_WIDTH + 2 * KV_WIDTH
WINDOW = 128
ROT_DIM = HEAD_DIM // 4
ROPE_THETA = 500000.0
HY_WIDTH = D_MODEL - ATTN_WIDTH
HY_ORDER = 2
HY_EMB_DIM = 33
HY_FILTER_HIDDEN = 64
HY_DECAY_TARGET = 1e-2
HY_FAST_DECAY_PCT = 0.3
HY_SLOW_DECAY_PCT = 1.5
N_EXPERT_GROUPS = 4
EXPERTS_PER_GROUP = 8
N_EXPERTS = N_EXPERT_GROUPS * EXPERTS_PER_GROUP
TOP_K = 2
D_EXPERT = 1024
EPS = 1e-6

LANES = 128
SUBLANES = 8
FFT_N2 = 128
MOE_BLOCK_ROWS = 256
ROUTER_PAD = 128
VMEM_LIMIT = 56 * 1024 * 1024
NEG_BIG = -1e30


def _cparams(*sem):
    return pltpu.CompilerParams(dimension_semantics=sem, vmem_limit_bytes=VMEM_LIMIT)


def _split_bf16(a):
    hi = a.astype(BF16)
    lo = (a - hi.astype(F32)).astype(BF16)
    return hi, lo


def _dot(a, b):
    return jnp.dot(a, b, preferred_element_type=F32)


def _dot3(a, b):
    a_hi, a_lo = _split_bf16(a)
    b_hi, b_lo = _split_bf16(b)
    return _dot(a_hi, b_hi) + _dot(a_lo, b_hi) + _dot(a_hi, b_lo)


def _qkv_kernel(x_ref, nw_ref, w_ref, seg_ref, hw_ref, rc_ref, rs1_ref, rs2_ref,
                q_ref, k_ref, v_ref):
    x = x_ref[...]
    inv = lax.rsqrt(jnp.mean(x * x, axis=-1, keepdims=True) + EPS)
    h = (x * inv * nw_ref[...]).astype(BF16)
    acc = _dot(h, w_ref[...])
    qk = acc[:, :QK_WIDTH]
    hi, lo = _split_bf16(qk * qk)
    seg = seg_ref[...]
    w = seg.shape[0]
    ms = jnp.concatenate(
        [_dot(hi[:, c * w:(c + 1) * w], seg) + _dot(lo[:, c * w:(c + 1) * w], seg)
         for c in range(QK_WIDTH // w)], axis=-1)
    xn = qk * lax.rsqrt(ms + EPS) * hw_ref[...]
    reps = QK_WIDTH // LANES
    rc = jnp.tile(rc_ref[...], (1, reps))
    rs1 = jnp.tile(rs1_ref[...], (1, reps))
    rs2 = jnp.tile(rs2_ref[...], (1, reps))
    half = ROT_DIM // 2
    y = xn * rc + pltpu.roll(xn, QK_WIDTH - half, 1) * rs1 + pltpu.roll(xn, half, 1) * rs2
    q_ref[...] = (y[:, :ATTN_WIDTH] * (HEAD_DIM ** -0.5)).astype(BF16)
    k_ref[...] = y[:, ATTN_WIDTH:].astype(BF16)
    v_ref[...] = acc[:, QK_WIDTH:].astype(BF16)


def _rope_tables(L):
    half = ROT_DIM // 2
    inv_freq = jnp.power(ROPE_THETA, -jnp.arange(half, dtype=F32) * 2.0 / ROT_DIM)
    ang = jnp.arange(L, dtype=F32)[:, None] * inv_freq[None, :]
    d = jnp.arange(LANES) % HEAD_DIM
    ang_l = ang[:, d % half]
    cos, sin = jnp.cos(ang_l), jnp.sin(ang_l)
    rc = jnp.where(d < ROT_DIM, cos, 1.0)
    rs1 = jnp.where(d < half, -sin, 0.0)
    rs2 = jnp.where((d >= half) & (d < ROT_DIM), sin, 0.0)
    return rc.astype(F32), rs1.astype(F32), rs2.astype(F32)


def _qkv_proj(x2d, L, norm1_w, w_qkv, q_norm_w, k_norm_w, tm=256):
    T = x2d.shape[0]
    seg_w = 256
    seg = (jnp.arange(seg_w)[:, None] // HEAD_DIM == jnp.arange(seg_w)[None, :] // HEAD_DIM)
    seg = (seg.astype(F32) / HEAD_DIM).astype(BF16)
    hw = jnp.concatenate([jnp.tile(q_norm_w, N_Q_HEADS), jnp.tile(k_norm_w, N_KV_HEADS)])[None, :]
    rc, rs1, rs2 = _rope_tables(L)
    lt = L // tm
    row = lambda i: (i, 0)
    const = lambda i: (0, 0)
    pos = lambda i: (i % lt, 0)
    return pl.pallas_call(
        _qkv_kernel,
        grid=(T // tm,),
        in_specs=[
            pl.BlockSpec((tm, D_MODEL), row),
            pl.BlockSpec((1, D_MODEL), const),
            pl.BlockSpec((D_MODEL, QKV_WIDTH), const),
            pl.BlockSpec((seg_w, seg_w), const),
            pl.BlockSpec((1, QK_WIDTH), const),
            pl.BlockSpec((tm, LANES), pos),
            pl.BlockSpec((tm, LANES), pos),
            pl.BlockSpec((tm, LANES), pos),
        ],
        out_specs=[
            pl.BlockSpec((tm, ATTN_WIDTH), row),
            pl.BlockSpec((tm, KV_WIDTH), row),
            pl.BlockSpec((tm, KV_WIDTH), row),
        ],
        out_shape=[
            jax.ShapeDtypeStruct((T, ATTN_WIDTH), BF16),
            jax.ShapeDtypeStruct((T, KV_WIDTH), BF16),
            jax.ShapeDtypeStruct((T, KV_WIDTH), BF16),
        ],
        compiler_params=_cparams("parallel"),
        name="qkv_proj",
    )(x2d, norm1_w[None, :], w_qkv, seg, hw, rc, rs1, rs2)


def _uproj_kernel(x_ref, nw_ref, w_ref, o_ref, h_scr):
    @pl.when(pl.program_id(1) == 0)
    def _():
        x = x_ref[...]
        inv = lax.rsqrt(jnp.mean(x * x, axis=-1, keepdims=True) + EPS)
        h_scr[...] = (x * inv * nw_ref[...]).astype(BF16)

    o_ref[...] = _dot(h_scr[...], w_ref[...])


def _u_proj(x2d, norm1_w, w_u, tm=1024, tn=1024):
    T = x2d.shape[0]
    n = w_u.shape[1]
    return pl.pallas_call(
        _uproj_kernel,
        grid=(T // tm, n // tn),
        in_specs=[
            pl.BlockSpec((tm, D_MODEL), lambda i, j: (i, 0)),
            pl.BlockSpec((1, D_MODEL), lambda i, j: (0, 0)),
            pl.BlockSpec((D_MODEL, tn), lambda i, j: (0, j)),
        ],
        out_specs=pl.BlockSpec((tm, tn), lambda i, j: (i, j)),
        out_shape=jax.ShapeDtypeStruct((T, n), F32),
        scratch_shapes=[pltpu.VMEM((tm, D_MODEL), BF16)],
        compiler_params=_cparams("parallel", "arbitrary"),
        name="u_proj",
    )(x2d, norm1_w[None, :], w_u)


def _sconv_kernel(u_ref, up_ref, un_ref, w_ref, b_ref, o_ref):
    i = pl.program_id(1)
    last = pl.num_programs(1) - 1
    u = u_ref[0]
    tl = u.shape[0]
    prev_row = jnp.where(i > 0, up_ref[0, 7:8, :], 0.0)
    next_row = jnp.where(i < last, un_ref[0, 0:1, :], 0.0)
    rid = lax.broadcasted_iota(jnp.int32, u.shape, 0)
    um = jnp.where(rid == 0, prev_row, pltpu.roll(u, 1, 0))
    up = jnp.where(rid == tl - 1, next_row, pltpu.roll(u, tl - 1, 0))
    w = w_ref[...]
    val = w[0:1] * um + w[1:2] * u + w[2:3] * up + b_ref[...]
    for n1 in range(tl // FFT_N2):
        for j in range(FFT_N2 // SUBLANES):
            r0 = n1 * FFT_N2 + j * SUBLANES
            for c in range(val.shape[1] // LANES):
                o_ref[0, 0, c, j, n1 * SUBLANES:(n1 + 1) * SUBLANES, :] = (
                    val[r0:r0 + SUBLANES, c * LANES:(c + 1) * LANES])


def _short_conv(u3d, conv_w, conv_b, tl=1024):
    B, L, W = u3d.shape
    C = HY_WIDTH
    nparts = W // C
    r8 = tl // 8
    J = FFT_N2 // SUBLANES
    return pl.pallas_call(
        _sconv_kernel,
        grid=(B, L // tl, nparts),
        in_specs=[
            pl.BlockSpec((1, tl, C), lambda b, i, c: (b, i, c)),
            pl.BlockSpec((1, 8, C), lambda b, i, c: (b, jnp.maximum(i * r8 - 1, 0), c)),
            pl.BlockSpec((1, 8, C), lambda b, i, c: (b, jnp.minimum((i + 1) * r8, L // 8 - 1), c)),
            pl.BlockSpec((3, C), lambda b, i, c: (0, c)),
            pl.BlockSpec((1, C), lambda b, i, c: (0, c)),
        ],
        out_specs=pl.BlockSpec((1, 1, C // LANES, J, tl // J, LANES), lambda b, i, c: (c, b, 0, 0, i, 0)),
        out_shape=jax.ShapeDtypeStruct((nparts, B, C // LANES, J, L // J, LANES), F32),
        compiler_params=_cparams("parallel", "parallel", "parallel"),
        name="short_conv",
    )(u3d, u3d, u3d, conv_w, conv_b[None, :])


def _fft_tables(L):
    N = 2 * L
    N2 = FFT_N2
    N1 = N // N2
    H = N1 // 2
    k1 = jnp.arange(N1, dtype=jnp.int32)
    n2 = jnp.arange(N2, dtype=jnp.int32)
    a1 = ((k1[:, None] * k1[None, :]) % N1).astype(F32) * (2.0 * math.pi / N1)
    c1, s1 = jnp.cos(a1), jnp.sin(a1)
    at = (n2[:, None] * k1[None, :]).astype(F32) * (2.0 * math.pi / N)
    ct, st = jnp.cos(at), jnp.sin(at)
    c = c1[None] * ct[:, :, None] - s1[None] * st[:, :, None]
    s = s1[None] * ct[:, :, None] + c1[None] * st[:, :, None]
    m_filt = jnp.concatenate([c, -s], axis=1).astype(BF16)
    ch, sh = c[:, :, :H], s[:, :, :H]
    m_fwd = jnp.concatenate([jnp.concatenate([ch, sh], axis=2),
                             jnp.concatenate([-sh, ch], axis=2)], axis=1)
    cT = c1[None, :H] * ct[:, None, :] - s1[None, :H] * st[:, None, :]
    sT = s1[None, :H] * ct[:, None, :] + c1[None, :H] * st[:, None, :]
    m_inv = jnp.concatenate([jnp.concatenate([cT, -sT], axis=2),
                             jnp.concatenate([sT, cT], axis=2)], axis=1)
    k2 = jnp.arange(N2, dtype=jnp.int32)
    ph = ((k2[:, None] * n2[None, :]) % N2).astype(F32) * (2.0 * math.pi / N2)
    c2, s2 = jnp.cos(ph), jnp.sin(ph)
    m2_fwd = jnp.concatenate([jnp.concatenate([c2, s2], axis=1),
                              jnp.concatenate([-s2, c2], axis=1)], axis=0)
    m2_inv = m2_fwd.T
    return dict(N1=N1, N2=N2, m_filt=m_filt, m_fwd=m_fwd.astype(BF16), m_inv=m_inv.astype(BF16),
                m2_fwd=m2_fwd.astype(BF16), m2_inv=m2_inv.astype(BF16))


def _filt1_kernel(w1_ref, b1_ref, w2_ref, b2_ref, w3_ref, b3_ref, w4_ref, a_ref, frl_ref,
                  dl_ref, m_ref, o_ref, sabs_ref, *, L, N1, N2, nb):
    j = pl.program_id(0)
    C2 = HY_ORDER * HY_WIDTH
    H = N1 // 2
    rows = nb * N1
    ridx = lax.broadcasted_iota(jnp.int32, (rows, LANES), 0)
    lane = lax.broadcasted_iota(jnp.int32, (rows, LANES), 1)
    n2 = j * nb + ridx // N1
    n1 = ridx % N1
    r = n1 * N2 + n2
    lag = jnp.where(r < L, r, 2 * L - r)
    valid = (r != L).astype(F32)
    lagf = jnp.minimum(lag, L - 1).astype(F32)
    t = lagf / (L - 1)
    wpos = lagf * (2.0 * math.pi / L)
    phase = wpos * frl_ref[...]
    bands = (HY_EMB_DIM - 1) // 2
    zemb = jnp.where(lane == 0, t,
                     jnp.where(lane <= bands, jnp.cos(phase),
                               jnp.where(lane <= 2 * bands, -jnp.sin(phase), 0.0)))
    a = a_ref[...]
    h = jnp.sin(a * (_dot3(zemb, w1_ref[...]) + b1_ref[...]))
    h = jnp.sin(a * (_dot3(h, w2_ref[...]) + b2_ref[...]))
    h = jnp.sin(a * (_dot3(h, w3_ref[...]) + b3_ref[...]))
    h_hi, h_lo = _split_bf16(h)
    w4 = w4_ref[...]
    decay = jnp.exp(-jnp.tile(t, (1, HY_WIDTH // LANES)) * dl_ref[...])
    decay = decay * jnp.tile(valid, (1, HY_WIDTH // LANES))
    decay2 = jnp.tile(decay, (1, HY_ORDER))

    @pl.when(j == 0)
    def _():
        sabs_ref[...] = jnp.zeros_like(sabs_ref)

    tot = jnp.zeros((8, C2), F32)
    for l in range(nb):
        fs = slice(l * N1, l * N1 + H)
        bs = slice(l * N1 + H, (l + 1) * N1)
        gf = _dot(h_hi[fs], w4[:, :C2]) + _dot(h_lo[fs], w4[:, :C2])
        gb = _dot(h_hi[bs], w4[:, C2:]) + _dot(h_lo[bs], w4[:, C2:])
        g = jnp.concatenate([gf, gb], axis=0) * decay2[l * N1:(l + 1) * N1]
        tot = tot + jnp.sum(jnp.abs(g).reshape(N1 // 8, 8, C2), axis=0)
        y = _dot(m_ref[l], g.astype(BF16))
        _store_tiles(lambda ri, c: o_ref.at[ri, c], y, N1, C2 // LANES, l)
    sabs_ref[...] += tot


def _store_tiles(dst, y, nrows, ncols, s):
    for ri in range(2):
        for c in range(ncols):
            d = dst(ri, c)
            for t in range(nrows // SUBLANES):
                r0 = ri * nrows + t * SUBLANES
                d[t, s * SUBLANES:(s + 1) * SUBLANES, :] = y[r0:r0 + SUBLANES, c * LANES:(c + 1) * LANES]


def _rows_strided(ref2d, s):
    return ref2d[pl.ds(s, ref2d.shape[0] // SUBLANES, stride=SUBLANES), :]


def _gather_cols(fn, ncols):
    return jnp.concatenate([fn(c) for c in range(ncols)], axis=1)


def _filt2_kernel(a_ref, m_ref, sc_ref, o_ref):
    cb = a_ref.shape[1]
    N2 = a_ref.shape[3] // SUBLANES
    for s in range(SUBLANES):
        a = jnp.concatenate(
            [_gather_cols(lambda c: _rows_strided(a_ref.at[ri, c, 0], s), cb) for ri in range(2)],
            axis=0).astype(BF16)
        x = _dot(m_ref[...], a) * sc_ref[...]
        for ri in range(2):
            for c in range(cb):
                o_ref[ri, c, s] = x[ri * N2:(ri + 1) * N2, c * LANES:(c + 1) * LANES]


def _hyena_filter_spectrum(L, tabs, filt_w1, filt_b1, filt_w2, filt_b2, filt_w3, filt_b3,
                           filt_w4, filt_freq):
    N1, N2 = tabs["N1"], tabs["N2"]
    N = N1 * N2
    C2 = HY_ORDER * HY_WIDTH
    nb = max(1, 512 // N1)
    bands = (HY_EMB_DIM - 1) // 2
    fr = jnp.linspace(1e-4, bands - 1, bands, dtype=F32)
    frl = jnp.zeros((LANES,), F32).at[1:1 + bands].set(fr).at[1 + bands:1 + 2 * bands].set(fr)[None, :]
    w1p = jnp.zeros((LANES, HY_FILTER_HIDDEN), F32).at[:HY_EMB_DIM].set(filt_w1)
    max_decay = math.log(HY_DECAY_TARGET) / HY_FAST_DECAY_PCT
    min_decay = math.log(HY_DECAY_TARGET) / HY_SLOW_DECAY_PCT
    deltas = jnp.abs(jnp.linspace(min_decay, max_decay, HY_WIDTH, dtype=F32))[None, :]
    full = lambda shape: pl.BlockSpec(shape, lambda j: (0,) * len(shape))
    a_g, sabs = pl.pallas_call(
        functools.partial(_filt1_kernel, L=L, N1=N1, N2=N2, nb=nb),
        grid=(N2 // nb,),
        in_specs=[
            full((LANES, HY_FILTER_HIDDEN)), full((1, HY_FILTER_HIDDEN)),
            full((HY_FILTER_HIDDEN, HY_FILTER_HIDDEN)), full((1, HY_FILTER_HIDDEN)),
            full((HY_FILTER_HIDDEN, HY_FILTER_HIDDEN)), full((1, HY_FILTER_HIDDEN)),
            full((HY_FILTER_HIDDEN, 2 * C2)), full((1, HY_FILTER_HIDDEN)),
            full((1, LANES)), full((1, HY_WIDTH)),
            pl.BlockSpec((nb, 2 * N1, N1), lambda j: (j, 0, 0)),
        ],
        out_specs=[
            pl.BlockSpec((2, C2 // LANES, N1 // SUBLANES, nb * SUBLANES, LANES),
                         lambda j: (0, 0, 0, j, 0)),
            pl.BlockSpec((8, C2), lambda j: (0, 0)),
        ],
        out_shape=[
            jax.ShapeDtypeStruct((2, C2 // LANES, N1 // SUBLANES, N2 * SUBLANES, LANES), F32),
            jax.ShapeDtypeStruct((8, C2), F32),
        ],
        compiler_params=_cparams("arbitrary"),
        name="hyena_filter_stage1",
    )(w1p, filt_b1[None, :], filt_w2, filt_b2[None, :], filt_w3, filt_b3[None, :],
      filt_w4.astype(BF16), filt_freq[None, :], frl, deltas, tabs["m_filt"])
    scale = (1.0 / (jnp.sum(sabs, axis=0) * N))[None, :]
    cb = 4
    return pl.pallas_call(
        _filt2_kernel,
        grid=(N1 // SUBLANES, C2 // (cb * LANES)),
        in_specs=[
            pl.BlockSpec((2, cb, 1, N2 * SUBLANES, LANES), lambda i, c: (0, c, i, 0, 0)),
            pl.BlockSpec((2 * N2, 2 * N2), lambda i, c: (0, 0)),
            pl.BlockSpec((1, cb * LANES), lambda i, c: (0, c)),
        ],
        out_specs=pl.BlockSpec((2, cb, SUBLANES, N2, LANES), lambda i, c: (0, c, i, 0, 0)),
        out_shape=jax.ShapeDtypeStruct((2, C2 // LANES, N1, N2, LANES), F32),
        compiler_params=_cparams("parallel", "parallel"),
        name="hyena_filter_stage2",
    )(a_g, tabs["m2_fwd"], scale)


def _fwd1_kernel(x_ref, m_ref, o_ref, *, slabbed):
    cb = o_ref.shape[2]
    N1 = m_ref.shape[2]
    for s in range(SUBLANES):
        if slabbed:
            rows = [_gather_cols(lambda c: _rows_strided(x_ref.at[0, bi, c, 0], s), cb) for bi in range(2)]
        else:
            rows = [_gather_cols(lambda c: x_ref[bi, c, s], cb) for bi in range(2)]
        xl = jnp.concatenate(rows, axis=0).astype(BF16)
        y = _dot(m_ref[s], xl)
        _store_tiles(lambda ri, c: o_ref.at[0, ri, c], y, N1, cb, s)


def _hyena_stage1(z, which, tabs):
    N1, N2 = tabs["N1"], tabs["N2"]
    H = N1 // 2
    slabbed = which is not None
    B, NC = (z.shape[1], z.shape[2]) if slabbed else (z.shape[0], z.shape[1])
    cb = min(NC, 4)
    if slabbed:
        x_spec = pl.BlockSpec((1, 2, cb, 1, H * SUBLANES, LANES), lambda p, c, j: (which, p, c, j, 0, 0))
    else:
        x_spec = pl.BlockSpec((2, cb, SUBLANES, H, LANES), lambda p, c, j: (p, c, j, 0, 0))
    return pl.pallas_call(
        functools.partial(_fwd1_kernel, slabbed=slabbed),
        grid=(B // 2, NC // cb, N2 // SUBLANES),
        in_specs=[x_spec, pl.BlockSpec((SUBLANES, 2 * N1, N1), lambda p, c, j: (j, 0, 0))],
        out_specs=pl.BlockSpec((1, 2, cb, N1 // SUBLANES, SUBLANES * SUBLANES, LANES),
                               lambda p, c, j: (p, 0, c, 0, j, 0)),
        out_shape=jax.ShapeDtypeStruct((B // 2, 2, NC, N1 // SUBLANES, N2 * SUBLANES, LANES), F32),
        compiler_params=_cparams("parallel", "parallel", "parallel"),
        name="hyena_stage1",
    )(z, tabs["m_fwd"])


def _mid_kernel(a_ref, g_ref, mf_ref, mi_ref, o_ref):
    cb = a_ref.shape[2]
    N2 = g_ref.shape[3]
    for s in range(SUBLANES):
        a = jnp.concatenate(
            [_gather_cols(lambda c: _rows_strided(a_ref.at[0, ri, c, 0], s), cb) for ri in range(2)],
            axis=0).astype(BF16)
        x = _dot(mf_ref[...], a)
        xr, xi = x[:N2], x[N2:]
        gr = _gather_cols(lambda c: g_ref[0, c, s], cb)
        gi = _gather_cols(lambda c: g_ref[1, c, s], cb)
        y = jnp.concatenate([xr * gr - xi * gi, xr * gi + xi * gr], axis=0).astype(BF16)
        b = _dot(mi_ref[...], y)
        _store_tiles(lambda ri, c: o_ref.at[0, ri, c], b, N2, cb, s)


def _hyena_stage2(a, g, order, tabs, cb=4):
    N1, N2 = tabs["N1"], tabs["N2"]
    P, _, NC = a.shape[:3]
    ncb = NC // cb
    return pl.pallas_call(
        _mid_kernel,
        grid=(N1 // SUBLANES, ncb, P),
        in_specs=[
            pl.BlockSpec((1, 2, cb, 1, N2 * SUBLANES, LANES), lambda i, c, p: (p, 0, c, i, 0, 0)),
            pl.BlockSpec((2, cb, SUBLANES, N2, LANES), lambda i, c, p: (0, order * ncb + c, i, 0, 0)),
            pl.BlockSpec((2 * N2, 2 * N2), lambda i, c, p: (0, 0)),
            pl.BlockSpec((2 * N2, 2 * N2), lambda i, c, p: (0, 0)),
        ],
        out_specs=pl.BlockSpec((1, 2, cb, N2 // SUBLANES, SUBLANES * SUBLANES, LANES),
                               lambda i, c, p: (p, 0, c, 0, i, 0)),
        out_shape=jax.ShapeDtypeStruct((P, 2, NC, N2 // SUBLANES, N1 * SUBLANES, LANES), F32),
        compiler_params=_cparams("parallel", "parallel", "arbitrary"),
        name="hyena_stage2",
    )(a, g, tabs["m2_fwd"], tabs["m2_inv"])


def _inv1_kernel(b_ref, m_ref, gate_ref, z_ref, bias_ref, o_ref, *, z_slabbed, final):
    cb = b_ref.shape[2]
    N1 = m_ref.shape[1]
    H = N1 // 2
    bias = bias_ref[...]
    for s in range(SUBLANES):
        b = jnp.concatenate(
            [_gather_cols(lambda c: _rows_strided(b_ref.at[0, ri, c, 0], s), cb) for ri in range(2)],
            axis=0).astype(BF16)
        conv = _dot(m_ref[s], b)
        for bi in range(2):
            gate = _gather_cols(lambda c: _rows_strided(gate_ref.at[0, bi, c, 0], s), cb)
            if z_slabbed:
                z = _gather_cols(lambda c: _rows_strided(z_ref.at[0, bi, c, 0], s), cb)
            else:
                z = _gather_cols(lambda c: z_ref[bi, c, s], cb)
            out = gate * (conv[bi * H:(bi + 1) * H] + z * bias)
            for c in range(cb):
                piece = out[:, c * LANES:(c + 1) * LANES]
                if final:
                    o_ref.at[bi, c, 0][pl.ds(s, H, stride=SUBLANES), :] = piece
                else:
                    o_ref[bi, c, s] = piece


def _hyena_stage3(bq, ucl, gate_idx, z, z_idx, bias, tabs, final):
    N1, N2 = tabs["N1"], tabs["N2"]
    H = N1 // 2
    P, _, NC = bq.shape[:3]
    cb = min(NC, max(1, 512 // N1))
    z_slabbed = z_idx is not None
    slab_blk = (1, 2, cb, 1, H * SUBLANES, LANES)
    n2m_blk = (2, cb, SUBLANES, H, LANES)
    if z_slabbed:
        z_spec = pl.BlockSpec(slab_blk, lambda p, c, j: (z_idx, p, c, j, 0, 0))
    else:
        z_spec = pl.BlockSpec(n2m_blk, lambda p, c, j: (p, c, j, 0, 0))
    if final:
        o_spec = pl.BlockSpec((2, cb, 1, H * SUBLANES, LANES), lambda p, c, j: (p, c, j, 0, 0))
        o_shape = (2 * P, NC, N2 // SUBLANES, H * SUBLANES, LANES)
    else:
        o_spec = pl.BlockSpec(n2m_blk, lambda p, c, j: (p, c, j, 0, 0))
        o_shape = (2 * P, NC, N2, H, LANES)
    return pl.pallas_call(
        functools.partial(_inv1_kernel, z_slabbed=z_slabbed, final=final),
        grid=(P, NC // cb, N2 // SUBLANES),
        in_specs=[
            pl.BlockSpec((1, 2, cb, 1, N1 * SUBLANES, LANES), lambda p, c, j: (p, 0, c, j, 0, 0)),
            pl.BlockSpec((SUBLANES, N1, 2 * N1), lambda p, c, j: (j, 0, 0)),
            pl.BlockSpec(slab_blk, lambda p, c, j: (gate_idx, p, c, j, 0, 0)),
            z_spec,
            pl.BlockSpec((1, cb * LANES), lambda p, c, j: (0, c)),
        ],
        out_specs=o_spec,
        out_shape=jax.ShapeDtypeStruct(o_shape, F32),
        compiler_params=_cparams("parallel", "parallel", "parallel"),
        name="hyena_stage3",
    )(bq, tabs["m_inv"], ucl, z, bias[None, :])


def _hyena_mixer(ucl, g_spec, hy_bias, tabs):
    z, z_idx = ucl, 2
    for order in range(HY_ORDER):
        a = _hyena_stage1(z, z_idx, tabs)
        bq = _hyena_stage2(a, g_spec, order, tabs)
        z = _hyena_stage3(bq, ucl, order, z, z_idx, hy_bias[order], tabs,
                          final=(order == HY_ORDER - 1))
        z_idx = None
    return z


def _attn_kernel(sink_ref, q_ref, kp_ref, kc_ref, kn_ref, vp_ref, vc_ref, vn_ref, nw_ref,
                 o_ref, y_scr, *, L):
    i = pl.program_id(1)
    tq = q_ref.shape[1]
    span = tq + 2 * WINDOW
    k = jnp.concatenate([kp_ref[0], kc_ref[0], kn_ref[0]], axis=0)
    v = jnp.concatenate([vp_ref[0], vc_ref[0], vn_ref[0]], axis=0)
    row = lax.broadcasted_iota(jnp.int32, (tq, span), 0)
    col = lax.broadcasted_iota(jnp.int32, (tq, span), 1)
    rel = col - row
    kpos = i * tq - WINDOW + col
    ok = (rel >= 0) & (rel <= 2 * WINDOW) & (kpos >= 0) & (kpos < L)
    bias = jnp.where(ok, 0.0, NEG_BIG).astype(F32)
    for h in range(N_KV_HEADS):
        kh = k[:, h * HEAD_DIM:(h + 1) * HEAD_DIM]
        vh = v[:, h * HEAD_DIM:(h + 1) * HEAD_DIM]
        for g in range(Q_PER_KV):
            qh_i = h * Q_PER_KV + g
            qh = q_ref[0, :, qh_i * HEAD_DIM:(qh_i + 1) * HEAD_DIM]
            s = lax.dot_general(qh, kh, (((1,), (1,)), ((), ())),
                                preferred_element_type=F32) + bias
            sink = sink_ref[qh_i]
            m = jnp.maximum(jnp.max(s, axis=-1, keepdims=True), sink)
            p = jnp.exp(s - m)
            denom = jnp.sum(p, axis=-1, keepdims=True) + jnp.exp(sink - m)
            o = _dot(p.astype(BF16), vh) / denom
            y_scr[:, qh_i * HEAD_DIM:(qh_i + 1) * HEAD_DIM] = o
    y = y_scr[...]
    inv = lax.rsqrt(jnp.mean(y * y, axis=-1, keepdims=True) + EPS)
    o_ref[0] = (y * inv * nw_ref[...]).astype(o_ref.dtype)


def _windowed_attention(q, k, v, sink, out_norm_w, tq=256):
    B, L, _ = q.shape
    rq = tq // WINDOW
    nwb = L // WINDOW
    cur = lambda b, i, s: (b, i, 0)
    prv = lambda b, i, s: (b, jnp.maximum(i * rq - 1, 0), 0)
    nxt = lambda b, i, s: (b, jnp.minimum((i + 1) * rq, nwb - 1), 0)
    grid_spec = pltpu.PrefetchScalarGridSpec(
        num_scalar_prefetch=1,
        grid=(B, L // tq),
        in_specs=[
            pl.BlockSpec((1, tq, ATTN_WIDTH), cur),
            pl.BlockSpec((1, WINDOW, KV_WIDTH), prv),
            pl.BlockSpec((1, tq, KV_WIDTH), cur),
            pl.BlockSpec((1, WINDOW, KV_WIDTH), nxt),
            pl.BlockSpec((1, WINDOW, KV_WIDTH), prv),
            pl.BlockSpec((1, tq, KV_WIDTH), cur),
            pl.BlockSpec((1, WINDOW, KV_WIDTH), nxt),
            pl.BlockSpec((1, ATTN_WIDTH), lambda b, i, s: (0, 0)),
        ],
        out_specs=pl.BlockSpec((1, tq, ATTN_WIDTH), cur),
        scratch_shapes=[pltpu.VMEM((tq, ATTN_WIDTH), F32)],
    )
    return pl.pallas_call(
        functools.partial(_attn_kernel, L=L),
        grid_spec=grid_spec,
        out_shape=jax.ShapeDtypeStruct((B, L, ATTN_WIDTH), BF16),
        compiler_params=_cparams("parallel", "parallel"),
        name="banded_attention",
    )(sink.astype(F32), q, k, k, k, v, v, v, out_norm_w[None, :])


def _outproj_kernel(x_ref, a_ref, hy_ref, hw_ref, wa_ref, wh_ref, nw_ref, wr_ref,
                    x1_ref, h2_ref, lg_ref):
    nslab = hy_ref.shape[2]
    hy = jnp.concatenate(
        [_gather_cols(lambda c: hy_ref[0, c, j, n1 * SUBLANES:(n1 + 1) * SUBLANES, :], hy_ref.shape[1])
         for n1 in range(hy_ref.shape[3] // SUBLANES) for j in range(nslab)],
        axis=0)
    hinv = lax.rsqrt(jnp.mean(hy * hy, axis=-1, keepdims=True) + EPS)
    hy_n = (hy * hinv * hw_ref[...]).astype(BF16)
    x1 = x_ref[...] + _dot(a_ref[...], wa_ref[...]) + _dot(hy_n, wh_ref[...])
    x1_ref[...] = x1
    inv = lax.rsqrt(jnp.mean(x1 * x1, axis=-1, keepdims=True) + EPS)
    h2 = x1 * inv * nw_ref[...]
    h2_ref[...] = h2.astype(BF16)
    h_hi, h_lo = _split_bf16(h2)
    lg_ref[...] = _dot(h_hi, wr_ref[0]) + _dot(h_lo, wr_ref[0]) + _dot(h_hi, wr_ref[1])


def _out_proj(x2d, attn_n, hy, hy_norm_w, w_out_a, w_out_h, norm2_w, w_router, tm=256):
    T = x2d.shape[0]
    _, NC, J, LJ, _ = hy.shape
    lt = (LJ * J) // tm
    row = lambda i: (i, 0)
    const = lambda i: (0, 0)
    return pl.pallas_call(
        _outproj_kernel,
        grid=(T // tm,),
        in_specs=[
            pl.BlockSpec((tm, D_MODEL), row),
            pl.BlockSpec((tm, ATTN_WIDTH), row),
            pl.BlockSpec((1, NC, J, tm // J, LANES), lambda i: (i // lt, 0, 0, i % lt, 0)),
            pl.BlockSpec((1, HY_WIDTH), const),
            pl.BlockSpec((ATTN_WIDTH, D_MODEL), const),
            pl.BlockSpec((HY_WIDTH, D_MODEL), const),
            pl.BlockSpec((1, D_MODEL), const),
            pl.BlockSpec((2, D_MODEL, ROUTER_PAD), lambda i: (0, 0, 0)),
        ],
        out_specs=[
            pl.BlockSpec((tm, D_MODEL), row),
            pl.BlockSpec((tm, D_MODEL), row),
            pl.BlockSpec((tm, ROUTER_PAD), row),
        ],
        out_shape=[
            jax.ShapeDtypeStruct((T, D_MODEL), F32),
            jax.ShapeDtypeStruct((T, D_MODEL), BF16),
            jax.ShapeDtypeStruct((T, ROUTER_PAD), F32),
        ],
        compiler_params=_cparams("parallel"),
        name="out_proj",
    )(x2d, attn_n, hy, hy_norm_w[None, :], w_out_a, w_out_h, norm2_w[None, :], w_router)


def _moe_kernel(be_ref, nu_ref, x_ref, wg_ref, wu_ref, wd_ref, o_ref):
    @pl.when(pl.program_id(0) < nu_ref[0])
    def _():
        x = x_ref[...]
        g = _dot(x, wg_ref[0])
        u = _dot(x, wu_ref[0])
        hid = (g * jax.nn.sigmoid(g)) * u
        o_ref[...] = _dot(hid.astype(BF16), wd_ref[0]).astype(o_ref.dtype)


def _expert_blocks(xs, block_expert, n_used, w_gate, w_up, w_down):
    P = xs.shape[0]
    bm = MOE_BLOCK_ROWS
    blk = lambda b, be, nu: (jnp.minimum(b, nu[0] - 1), 0)
    wsel = lambda b, be, nu: (be[jnp.minimum(b, nu[0] - 1)], 0, 0)
    grid_spec = pltpu.PrefetchScalarGridSpec(
        num_scalar_prefetch=2,
        grid=(P // bm,),
        in_specs=[
            pl.BlockSpec((bm, D_MODEL), blk),
            pl.BlockSpec((1, D_MODEL, D_EXPERT), wsel),
            pl.BlockSpec((1, D_MODEL, D_EXPERT), wsel),
            pl.BlockSpec((1, D_EXPERT, D_MODEL), wsel),
        ],
        out_specs=pl.BlockSpec((bm, D_MODEL), blk),
    )
    return pl.pallas_call(
        _moe_kernel,
        grid_spec=grid_spec,
        out_shape=jax.ShapeDtypeStruct((P, D_MODEL), BF16),
        compiler_params=_cparams("arbitrary"),
        name="moe_experts",
    )(block_expert, n_used, xs, w_gate, w_up, w_down)


def _hier_moe(x1, h2, logits, b_rg, b_re, w_gate, w_up, w_down):
    T = h2.shape[0]
    g_logits = logits[:, :N_EXPERT_GROUPS] + b_rg
    e_logits = (logits[:, N_EXPERT_GROUPS:N_EXPERT_GROUPS + N_EXPERTS] + b_re)
    e_logits = e_logits.reshape(T, N_EXPERT_GROUPS, EXPERTS_PER_GROUP)
    g_idx = jnp.argmax(g_logits, axis=-1)
    g_gate = jnp.take_along_axis(jax.nn.softmax(g_logits, axis=-1), g_idx[:, None], axis=1)
    e_sel = jnp.take_along_axis(e_logits, g_idx[:, None, None], axis=1)[:, 0]
    top_val, top_loc = lax.top_k(e_sel, TOP_K)
    weights = g_gate * jax.nn.softmax(top_val, axis=-1)
    experts = (g_idx[:, None] * EXPERTS_PER_GROUP + top_loc).astype(jnp.int32)

    A = T * TOP_K
    bm = MOE_BLOCK_ROWS
    flat_e = experts.reshape(-1)
    onehot = (flat_e[:, None] == jnp.arange(N_EXPERTS, dtype=jnp.int32)[None, :]).astype(jnp.int32)
    csum = jnp.cumsum(onehot, axis=0)
    counts = csum[-1]
    rank = jnp.sum((csum - onehot) * onehot, axis=1)
    padded = ((counts + bm - 1) // bm) * bm
    pend = jnp.cumsum(padded)
    pstart = pend - padded
    seg_start = jnp.cumsum(counts) - counts
    dest = (pstart[flat_e] + rank).astype(jnp.int32)
    n_blocks = (A + bm - 1) // bm + N_EXPERTS
    P = n_blocks * bm
    order = jnp.argsort(flat_e, stable=True).astype(jnp.int32)
    q = jnp.arange(P, dtype=jnp.int32)
    e_q = jnp.minimum(jnp.sum((pend[None, :] <= q[:, None]).astype(jnp.int32), axis=1), N_EXPERTS - 1)
    r_q = jnp.minimum(q - pstart[e_q], jnp.maximum(counts[e_q] - 1, 0))
    src = jnp.clip(seg_start[e_q] + r_q, 0, A - 1)
    buf_tok = order.at[src].get(mode="promise_in_bounds") // TOP_K
    block_expert = e_q.reshape(n_blocks, bm)[:, 0]
    n_used = (pend[-1] // bm).astype(jnp.int32)[None]

    xs = h2.at[buf_tok].get(mode="promise_in_bounds")
    yb = _expert_blocks(xs, block_expert, n_used, w_gate, w_up, w_down)
    d = dest.reshape(T, TOP_K)
    y0 = yb.at[d[:, 0]].get(mode="promise_in_bounds").astype(F32)
    y1 = yb.at[d[:, 1]].get(mode="promise_in_bounds").astype(F32)
    return x1 + (weights[:, 0:1] * y0 + weights[:, 1:2] * y1)


def _encoder_layer(x, p):
    B, L, D = x.shape
    T = B * L
    x2d = x.reshape(T, D)
    q, k, v = _qkv_proj(x2d, L, p["norm1_w"], p["w_qkv"], p["q_norm_w"], p["k_norm_w"])
    attn_n = _windowed_attention(q.reshape(B, L, ATTN_WIDTH), k.reshape(B, L, KV_WIDTH),
                                 v.reshape(B, L, KV_WIDTH), p["attn_sink"], p["attn_out_norm_w"])
    u = _u_proj(x2d, p["norm1_w"], p["w_u"])
    uc = _short_conv(u.reshape(B, L, 3 * HY_WIDTH), p["conv_w"], p["conv_b"])
    tabs = _fft_tables(L)
    g_spec = _hyena_filter_spectrum(L, tabs, p["filt_w1"], p["filt_b1"], p["filt_w2"], p["filt_b2"],
                                    p["filt_w3"], p["filt_b3"], p["filt_w4"], p["filt_freq"])
    hy = _hyena_mixer(uc, g_spec, p["hy_bias"], tabs)
    x1, h2, logits = _out_proj(x2d, attn_n.reshape(T, ATTN_WIDTH), hy, p["hy_out_norm_w"],
                               p["w_out_a"], p["w_out_h"], p["norm2_w"], p["w_router"])
    out = _hier_moe(x1, h2, logits, p["b_route_group"], p["b_route_expert"],
                    p["w_gate"], p["w_up"], p["w_down"])
    return out.reshape(B, L, D)


def kernel(x_prompt, x_sample, norm1_w, w_in, q_norm_w, k_norm_w, attn_sink, conv_w, conv_b, filt_w1, filt_b1, filt_w2, filt_b2, filt_w3, filt_b3, filt_w4, filt_freq, hy_bias, attn_out_norm_w, hy_out_norm_w, w_out, norm2_w, w_route_group, b_route_group, w_route_expert, b_route_expert, w_gate, w_up, w_down):
    depth = norm1_w.shape[0]

    def layer_params(l):
        w_r = jnp.zeros((D_MODEL, ROUTER_PAD), F32)
        w_r = w_r.at[:, :N_EXPERT_GROUPS].set(w_route_group[l])
        w_r = w_r.at[:, N_EXPERT_GROUPS:N_EXPERT_GROUPS + N_EXPERTS].set(w_route_expert[l])
        r_hi, r_lo = _split_bf16(w_r)
        return dict(
            norm1_w=norm1_w[l], w_qkv=w_in[l][:, :QKV_WIDTH].astype(BF16),
            w_u=w_in[l][:, QKV_WIDTH:].astype(BF16), q_norm_w=q_norm_w[l], k_norm_w=k_norm_w[l],
            attn_sink=attn_sink[l], conv_w=conv_w[l], conv_b=conv_b[l],
            filt_w1=filt_w1[l], filt_b1=filt_b1[l], filt_w2=filt_w2[l], filt_b2=filt_b2[l],
            filt_w3=filt_w3[l], filt_b3=filt_b3[l], filt_w4=filt_w4[l], filt_freq=filt_freq[l],
            hy_bias=hy_bias[l], attn_out_norm_w=attn_out_norm_w[l], hy_out_norm_w=hy_out_norm_w[l],
            w_out_a=w_out[l][:ATTN_WIDTH].astype(BF16), w_out_h=w_out[l][ATTN_WIDTH:].astype(BF16),
            norm2_w=norm2_w[l], w_router=jnp.stack([r_hi, r_lo]),
            b_route_group=b_route_group[l], b_route_expert=b_route_expert[l],
            w_gate=w_gate[l].astype(BF16), w_up=w_up[l].astype(BF16), w_down=w_down[l].astype(BF16))

    params = [layer_params(l) for l in range(depth)]

    def trunk(x):
        for p in params:
            x = _encoder_layer(x, p)
        return x

    return (trunk(x_prompt), trunk(x_sample))
```

```python
import functools
import math

import jax
import jax.numpy as jnp
from jax import lax
from jax.experimental import pallas as pl
from jax.experimental.pallas import tpu as pltpu

F32 = jnp.float32
BF16 = jnp.bfloat16

D_MODEL = 2048
HEAD_DIM = 64
N_Q_HEADS = 16
N_KV_HEADS = 4
Q_PER_KV = N_Q_HEADS // N_KV_HEADS
ATTN_WIDTH = N_Q_HEADS * HEAD_DIM
KV_WIDTH = N_KV_HEADS * HEAD_DIM
QK_WIDTH = ATTN_WIDTH + KV_WIDTH
QKV_WIDTH = ATTN_WIDTH + 2 * KV_WIDTH
WINDOW = 128
ROT_DIM = HEAD_DIM // 4
ROPE_THETA = 500000.0
HY_WIDTH = D_MODEL - ATTN_WIDTH
HY_ORDER = 2
HY_EMB_DIM = 33
HY_FILTER_HIDDEN = 64
HY_DECAY_TARGET = 1e-2
HY_FAST_DECAY_PCT = 0.3
HY_SLOW_DECAY_PCT = 1.5
N_EXPERT_GROUPS = 4
EXPERTS_PER_GROUP = 8
N_EXPERTS = N_EXPERT_GROUPS * EXPERTS_PER_GROUP
TOP_K = 2
D_EXPERT = 1024
EPS = 1e-6

LANES = 128
SUBLANES = 8
FFT_N2 = 128
MOE_BLOCK_ROWS = 256
ROUTER_PAD = 128
VMEM_LIMIT = 56 * 1024 * 1024
NEG_BIG = -1e30


def _cparams(*sem):
    return pltpu.CompilerParams(dimension_semantics=sem, vmem_limit_bytes=VMEM_LIMIT)


def _split_bf16(a):
    hi = a.astype(BF16)
    lo = (a - hi.astype(F32)).astype(BF16)
    return hi, lo


def _dot(a, b):
    return jnp.dot(a, b, preferred_element_type=F32)


def _dot3(a, b):
    a_hi, a_lo = _split_bf16(a)
    b_hi, b_lo = _split_bf16(b)
    return _dot(a_hi, b_hi) + _dot(a_lo, b_hi) + _dot(a_hi, b_lo)


def _qkv_kernel(x_ref, nw_ref, w_ref, seg_ref, hw_ref, rc_ref, rs1_ref, rs2_ref,
                q_ref, k_ref, v_ref):
    x = x_ref[...]
    inv = lax.rsqrt(jnp.mean(x * x, axis=-1, keepdims=True) + EPS)
    h = (x * inv * nw_ref[...]).astype(BF16)
    acc = _dot(h, w_ref[...])
    qk = acc[:, :QK_WIDTH]
    hi, lo = _split_bf16(qk * qk)
    seg = seg_ref[...]
    w = seg.shape[0]
    ms = jnp.concatenate(
        [_dot(hi[:, c * w:(c + 1) * w], seg) + _dot(lo[:, c * w:(c + 1) * w], seg)
         for c in range(QK_WIDTH // w)], axis=-1)
    xn = qk * lax.rsqrt(ms + EPS) * hw_ref[...]
    reps = QK_WIDTH // LANES
    rc = jnp.tile(rc_ref[...], (1, reps))
    rs1 = jnp.tile(rs1_ref[...], (1, reps))
    rs2 = jnp.tile(rs2_ref[...], (1, reps))
    half = ROT_DIM // 2
    y = xn * rc + pltpu.roll(xn, QK_WIDTH - half, 1) * rs1 + pltpu.roll(xn, half, 1) * rs2
    q_ref[...] = (y[:, :ATTN_WIDTH] * (HEAD_DIM ** -0.5)).astype(BF16)
    k_ref[...] = y[:, ATTN_WIDTH:].astype(BF16)
    v_ref[...] = acc[:, QK_WIDTH:].astype(BF16)


def _rope_tables(L):
    half = ROT_DIM // 2
    inv_freq = jnp.power(ROPE_THETA, -jnp.arange(half, dtype=F32) * 2.0 / ROT_DIM)
    ang = jnp.arange(L, dtype=F32)[:, None] * inv_freq[None, :]
    d = jnp.arange(LANES) % HEAD_DIM
    ang_l = ang[:, d % half]
    cos, sin = jnp.cos(ang_l), jnp.sin(ang_l)
    rc = jnp.where(d < ROT_DIM, cos, 1.0)
    rs1 = jnp.where(d < half, -sin, 0.0)
    rs2 = jnp.where((d >= half) & (d < ROT_DIM), sin, 0.0)
    return rc.astype(F32), rs1.astype(F32), rs2.astype(F32)


def _qkv_proj(x2d, L, norm1_w, w_qkv, q_norm_w, k_norm_w, tm=512):
    T = x2d.shape[0]
    seg_w = 256
    seg = (jnp.arange(seg_w)[:, None] // HEAD_DIM == jnp.arange(seg_w)[None, :] // HEAD_DIM)
    seg = (seg.astype(F32) / HEAD_DIM).astype(BF16)
    hw = jnp.concatenate([jnp.tile(q_norm_w, N_Q_HEADS), jnp.tile(k_norm_w, N_KV_HEADS)])[None, :]
    rc, rs1, rs2 = _rope_tables(L)
    lt = L // tm
    row = lambda i: (i, 0)
    const = lambda i: (0, 0)
    pos = lambda i: (i % lt, 0)
    return pl.pallas_call(
        _qkv_kernel,
        grid=(T // tm,),
        in_specs=[
            pl.BlockSpec((tm, D_MODEL), row),
            pl.BlockSpec((1, D_MODEL), const),
            pl.BlockSpec((D_MODEL, QKV_WIDTH), const),
            pl.BlockSpec((seg_w, seg_w), const),
            pl.BlockSpec((1, QK_WIDTH), const),
            pl.BlockSpec((tm, LANES), pos),
            pl.BlockSpec((tm, LANES), pos),
            pl.BlockSpec((tm, LANES), pos),
        ],
        out_specs=[
            pl.BlockSpec((tm, ATTN_WIDTH), row),
            pl.BlockSpec((tm, KV_WIDTH), row),
            pl.BlockSpec((tm, KV_WIDTH), row),
        ],
        out_shape=[
            jax.ShapeDtypeStruct((T, ATTN_WIDTH), BF16),
            jax.ShapeDtypeStruct((T, KV_WIDTH), BF16),
            jax.ShapeDtypeStruct((T, KV_WIDTH), BF16),
        ],
        compiler_params=_cparams("parallel"),
        name="qkv_proj",
    )(x2d, norm1_w[None, :], w_qkv, seg, hw, rc, rs1, rs2)


def _uproj_kernel(x_ref, nw_ref, w_ref, o_ref, h_scr):
    @pl.when(pl.program_id(1) == 0)
    def _():
        x = x_ref[...]
        inv = lax.rsqrt(jnp.mean(x * x, axis=-1, keepdims=True) + EPS)
        h_scr[...] = (x * inv * nw_ref[...]).astype(BF16)

    o_ref[...] = _dot(h_scr[...], w_ref[...])


def _u_proj(x2d, norm1_w, w_u, tm=1024, tn=1024):
    T = x2d.shape[0]
    n = w_u.shape[1]
    return pl.pallas_call(
        _uproj_kernel,
        grid=(T // tm, n // tn),
        in_specs=[
            pl.BlockSpec((tm, D_MODEL), lambda i, j: (i, 0)),
            pl.BlockSpec((1, D_MODEL), lambda i, j: (0, 0)),
            pl.BlockSpec((D_MODEL, tn), lambda i, j: (0, j)),
        ],
        out_specs=pl.BlockSpec((tm, tn), lambda i, j: (i, j)),
        out_shape=jax.ShapeDtypeStruct((T, n), F32),
        scratch_shapes=[pltpu.VMEM((tm, D_MODEL), BF16)],
        compiler_params=_cparams("parallel", "arbitrary"),
        name="u_proj",
    )(x2d, norm1_w[None, :], w_u)


def _sconv_kernel(u_ref, up_ref, un_ref, w_ref, b_ref, o_ref):
    i = pl.program_id(1)
    last = pl.num_programs(1) - 1
    u = u_ref[0]
    tl = u.shape[0]
    prev_row = jnp.where(i > 0, up_ref[0, 7:8, :], 0.0)
    next_row = jnp.where(i < last, un_ref[0, 0:1, :], 0.0)
    rid = lax.broadcasted_iota(jnp.int32, u.shape, 0)
    um = jnp.where(rid == 0, prev_row, pltpu.roll(u, 1, 0))
    up = jnp.where(rid == tl - 1, next_row, pltpu.roll(u, tl - 1, 0))
    w = w_ref[...]
    val = w[0:1] * um + w[1:2] * u + w[2:3] * up + b_ref[...]
    for n1 in range(tl // FFT_N2):
        for j in range(FFT_N2 // SUBLANES):
            r0 = n1 * FFT_N2 + j * SUBLANES
            for c in range(val.shape[1] // LANES):
                o_ref[0, 0, c, j, n1 * SUBLANES:(n1 + 1) * SUBLANES, :] = (
                    val[r0:r0 + SUBLANES, c * LANES:(c + 1) * LANES])


def _short_conv(u3d, conv_w, conv_b, tl=1024):
    B, L, W = u3d.shape
    C = HY_WIDTH
    nparts = W // C
    r8 = tl // 8
    J = FFT_N2 // SUBLANES
    return pl.pallas_call(
        _sconv_kernel,
        grid=(B, L // tl, nparts),
        in_specs=[
            pl.BlockSpec((1, tl, C), lambda b, i, c: (b, i, c)),
            pl.BlockSpec((1, 8, C), lambda b, i, c: (b, jnp.maximum(i * r8 - 1, 0), c)),
            pl.BlockSpec((1, 8, C), lambda b, i, c: (b, jnp.minimum((i + 1) * r8, L // 8 - 1), c)),
            pl.BlockSpec((3, C), lambda b, i, c: (0, c)),
            pl.BlockSpec((1, C), lambda b, i, c: (0, c)),
        ],
        out_specs=pl.BlockSpec((1, 1, C // LANES, J, tl // J, LANES), lambda b, i, c: (c, b, 0, 0, i, 0)),
        out_shape=jax.ShapeDtypeStruct((nparts, B, C // LANES, J, L // J, LANES), F32),
        compiler_params=_cparams("parallel", "parallel", "parallel"),
        name="short_conv",
    )(u3d, u3d, u3d, conv_w, conv_b[None, :])


def _fft_tables(L):
    N = 2 * L
    N2 = FFT_N2
    N1 = N // N2
    H = N1 // 2
    k1 = jnp.arange(N1, dtype=jnp.int32)
    n2 = jnp.arange(N2, dtype=jnp.int32)
    a1 = ((k1[:, None] * k1[None, :]) % N1).astype(F32) * (2.0 * math.pi / N1)
    c1, s1 = jnp.cos(a1), jnp.sin(a1)
    at = (n2[:, None] * k1[None, :]).astype(F32) * (2.0 * math.pi / N)
    ct, st = jnp.cos(at), jnp.sin(at)
    c = c1[None] * ct[:, :, None] - s1[None] * st[:, :, None]
    s = s1[None] * ct[:, :, None] + c1[None] * st[:, :, None]
    m_filt = jnp.concatenate([c, -s], axis=1).astype(BF16)
    ch, sh = c[:, :, :H], s[:, :, :H]
    m_fwd = jnp.concatenate([jnp.concatenate([ch, sh], axis=2),
                             jnp.concatenate([-sh, ch], axis=2)], axis=1)
    cT = c1[None, :H] * ct[:, None, :] - s1[None, :H] * st[:, None, :]
    sT = s1[None, :H] * ct[:, None, :] + c1[None, :H] * st[:, None, :]
    m_inv = jnp.concatenate([jnp.concatenate([cT, -sT], axis=2),
                             jnp.concatenate([sT, cT], axis=2)], axis=1)
    k2 = jnp.arange(N2, dtype=jnp.int32)
    ph = ((k2[:, None] * n2[None, :]) % N2).astype(F32) * (2.0 * math.pi / N2)
    c2, s2 = jnp.cos(ph), jnp.sin(ph)
    m2_fwd = jnp.concatenate([jnp.concatenate([c2, s2], axis=1),
                              jnp.concatenate([-s2, c2], axis=1)], axis=0)
    m2_inv = m2_fwd.T
    return dict(N1=N1, N2=N2, m_filt=m_filt, m_fwd=m_fwd.astype(BF16), m_inv=m_inv.astype(BF16),
                m2_fwd=m2_fwd.astype(BF16), m2_inv=m2_inv.astype(BF16))


def _filt1_kernel(w1_ref, b1_ref, w2_ref, b2_ref, w3_ref, b3_ref, w4_ref, a_ref, frl_ref, off_ref,
                  dl_ref, m_ref, o_ref, sabs_ref, *, L, N1, N2, nb):
    j = pl.program_id(0)
    C2 = HY_ORDER * HY_WIDTH
    H = N1 // 2
    rows = nb * N1
    ridx = lax.broadcasted_iota(jnp.int32, (rows, LANES), 0)
    lane = lax.broadcasted_iota(jnp.int32, (rows, LANES), 1)
    n2 = j * nb + ridx // N1
    n1 = ridx % N1
    r = n1 * N2 + n2
    lag = jnp.where(r < L, r, 2 * L - r)
    valid = (r != L).astype(F32)
    lagf = jnp.minimum(lag, L - 1).astype(F32)
    t = lagf / (L - 1)
    wpos = lagf * (2.0 * math.pi / L)
    phase = wpos * frl_ref[...] + off_ref[...]
    zemb = jnp.where(lane == 0, t, jnp.where(lane < HY_EMB_DIM, jnp.cos(phase), 0.0))
    half = rows // 2
    a = a_ref[...]
    h = jnp.concatenate([zemb[:half], zemb[half:]], axis=1)
    h = jnp.sin(a * (_dot3(h, w1_ref[...]) + b1_ref[...]))
    h = jnp.sin(a * (_dot3(h, w2_ref[...]) + b2_ref[...]))
    h = jnp.sin(a * (_dot3(h, w3_ref[...]) + b3_ref[...]))
    h_hi, h_lo = _split_bf16(h)
    decay = jnp.exp(-jnp.tile(t, (1, HY_WIDTH // LANES)) * dl_ref[...])
    decay = decay * jnp.tile(valid, (1, HY_WIDTH // LANES))
    decay2 = jnp.tile(decay, (1, HY_ORDER))

    @pl.when(j == 0)
    def _():
        sabs_ref[...] = jnp.zeros_like(sabs_ref)

    tot = jnp.zeros((8, C2), F32)
    for l in range(nb):
        side = (l * N1) // half
        r0 = l * N1 - side * half
        fs = slice(r0, r0 + H)
        bs = slice(r0 + H, r0 + N1)
        wf = w4_ref[side, :, :C2]
        wb = w4_ref[side, :, C2:]
        gf = _dot(h_hi[fs], wf) + _dot(h_lo[fs], wf)
        gb = _dot(h_hi[bs], wb) + _dot(h_lo[bs], wb)
        g = jnp.concatenate([gf, gb], axis=0) * decay2[l * N1:(l + 1) * N1]
        tot = tot + jnp.sum(jnp.abs(g).reshape(N1 // 8, 8, C2), axis=0)
        y = _dot(m_ref[l], g.astype(BF16))
        _store_tiles(lambda c: o_ref.at[c], _pack_complex(y[:N1], y[N1:]), C2 // LANES, l)
    sabs_ref[...] += tot


def _pack_complex(re, im):
    r = lax.bitcast_convert_type(re.astype(BF16).astype(F32), jnp.uint32)
    i = lax.bitcast_convert_type(im.astype(BF16).astype(F32), jnp.uint32)
    return r | (i >> 16)


def _unpack_complex(w):
    re = lax.bitcast_convert_type(w & jnp.uint32(0xFFFF0000), F32)
    im = lax.bitcast_convert_type(w << 16, F32)
    return re, im


def _store_tiles(dst, w, ncols, s):
    for c in range(ncols):
        d = dst(c)
        for t in range(w.shape[0] // SUBLANES):
            d[t, s * SUBLANES:(s + 1) * SUBLANES, :] = (
                w[t * SUBLANES:(t + 1) * SUBLANES, c * LANES:(c + 1) * LANES])


def _rows_strided(ref2d, s):
    return ref2d[pl.ds(s, ref2d.shape[0] // SUBLANES, stride=SUBLANES), :]


def _gather_cols(fn, ncols):
    return jnp.concatenate([fn(c) for c in range(ncols)], axis=1)


def _filt2_kernel(a_ref, m_ref, sc_ref, o_ref):
    cb = a_ref.shape[0]
    N2 = a_ref.shape[2] // SUBLANES
    for s in range(SUBLANES):
        re, im = _unpack_complex(_gather_cols(lambda c: _rows_strided(a_ref.at[c, 0], s), cb))
        a = jnp.concatenate([re, im], axis=0).astype(BF16)
        x = _dot(m_ref[...], a) * sc_ref[...]
        w = _pack_complex(x[:N2], x[N2:])
        for c in range(cb):
            o_ref[c, s] = w[:, c * LANES:(c + 1) * LANES]


def _block_diag2(w):
    z = jnp.zeros_like(w)
    return jnp.concatenate([jnp.concatenate([w, z], axis=1), jnp.concatenate([z, w], axis=1)], axis=0)


def _hyena_filter_spectrum(L, tabs, filt_w1, filt_b1, filt_w2, filt_b2, filt_w3, filt_b3,
                           filt_w4, filt_freq):
    N1, N2 = tabs["N1"], tabs["N2"]
    N = N1 * N2
    C2 = HY_ORDER * HY_WIDTH
    FH = HY_FILTER_HIDDEN
    nb = max(2, 512 // N1)
    bands = (HY_EMB_DIM - 1) // 2
    fr = jnp.linspace(1e-4, bands - 1, bands, dtype=F32)
    frl = jnp.zeros((LANES,), F32).at[1:1 + bands].set(fr).at[1 + bands:1 + 2 * bands].set(fr)[None, :]
    off = jnp.zeros((LANES,), F32).at[1 + bands:1 + 2 * bands].set(0.5 * math.pi)[None, :]
    w1p = jnp.zeros((LANES, FH), F32).at[:HY_EMB_DIM].set(filt_w1)
    w4 = filt_w4.astype(BF16)
    w4z = jnp.zeros_like(w4)
    w4s = jnp.stack([jnp.concatenate([w4, w4z], axis=0), jnp.concatenate([w4z, w4], axis=0)])
    two = lambda v: jnp.tile(v, 2)[None, :]
    max_decay = math.log(HY_DECAY_TARGET) / HY_FAST_DECAY_PCT
    min_decay = math.log(HY_DECAY_TARGET) / HY_SLOW_DECAY_PCT
    deltas = jnp.abs(jnp.linspace(min_decay, max_decay, HY_WIDTH, dtype=F32))[None, :]
    full = lambda shape: pl.BlockSpec(shape, lambda j: (0,) * len(shape))
    a_g, sabs = pl.pallas_call(
        functools.partial(_filt1_kernel, L=L, N1=N1, N2=N2, nb=nb),
        grid=(N2 // nb,),
        in_specs=[
            full((2 * LANES, 2 * FH)), full((1, 2 * FH)),
            full((2 * FH, 2 * FH)), full((1, 2 * FH)),
            full((2 * FH, 2 * FH)), full((1, 2 * FH)),
            full((2, 2 * FH, 2 * C2)), full((1, 2 * FH)),
            full((1, LANES)), full((1, LANES)), full((1, HY_WIDTH)),
            pl.BlockSpec((nb, 2 * N1, N1), lambda j: (j, 0, 0)),
        ],
        out_specs=[
            pl.BlockSpec((C2 // LANES, N1 // SUBLANES, nb * SUBLANES, LANES), lambda j: (0, 0, j, 0)),
            pl.BlockSpec((8, C2), lambda j: (0, 0)),
        ],
        out_shape=[
            jax.ShapeDtypeStruct((C2 // LANES, N1 // SUBLANES, N2 * SUBLANES, LANES), jnp.uint32),
            jax.ShapeDtypeStruct((8, C2), F32),
        ],
        compiler_params=_cparams("arbitrary"),
        name="hyena_filter_stage1",
    )(_block_diag2(w1p), two(filt_b1), _block_diag2(filt_w2), two(filt_b2), _block_diag2(filt_w3),
      two(filt_b3), w4s, two(filt_freq), frl, off, deltas, tabs["m_filt"])
    scale = (1.0 / (jnp.sum(sabs, axis=0) * N))[None, :]
    cb = 4
    return pl.pallas_call(
        _filt2_kernel,
        grid=(N1 // SUBLANES, C2 // (cb * LANES)),
        in_specs=[
            pl.BlockSpec((cb, 1, N2 * SUBLANES, LANES), lambda i, c: (c, i, 0, 0)),
            pl.BlockSpec((2 * N2, 2 * N2), lambda i, c: (0, 0)),
            pl.BlockSpec((1, cb * LANES), lambda i, c: (0, c)),
        ],
        out_specs=pl.BlockSpec((cb, SUBLANES, N2, LANES), lambda i, c: (c, i, 0, 0)),
        out_shape=jax.ShapeDtypeStruct((C2 // LANES, N1, N2, LANES), jnp.uint32),
        compiler_params=_cparams("parallel", "parallel"),
        name="hyena_filter_stage2",
    )(a_g, tabs["m2_fwd"], scale)


def _fwd1_kernel(x_ref, m_ref, o_ref, *, slabbed):
    cb = o_ref.shape[1]
    N1 = m_ref.shape[2]
    for s in range(SUBLANES):
        if slabbed:
            rows = [_gather_cols(lambda c: _rows_strided(x_ref.at[0, bi, c, 0], s), cb) for bi in range(2)]
        else:
            rows = [_gather_cols(lambda c: x_ref[bi, c, s], cb) for bi in range(2)]
        xl = jnp.concatenate(rows, axis=0).astype(BF16)
        y = _dot(m_ref[s], xl)
        _store_tiles(lambda c: o_ref.at[0, c], _pack_complex(y[:N1], y[N1:]), cb, s)


def _hyena_stage1(z, which, tabs):
    N1, N2 = tabs["N1"], tabs["N2"]
    H = N1 // 2
    slabbed = which is not None
    B, NC = (z.shape[1], z.shape[2]) if slabbed else (z.shape[0], z.shape[1])
    cb = min(NC, 4)
    if slabbed:
        x_spec = pl.BlockSpec((1, 2, cb, 1, H * SUBLANES, LANES), lambda p, c, j: (which, p, c, j, 0, 0))
    else:
        x_spec = pl.BlockSpec((2, cb, SUBLANES, H, LANES), lambda p, c, j: (p, c, j, 0, 0))
    return pl.pallas_call(
        functools.partial(_fwd1_kernel, slabbed=slabbed),
        grid=(B // 2, NC // cb, N2 // SUBLANES),
        in_specs=[x_spec, pl.BlockSpec((SUBLANES, 2 * N1, N1), lambda p, c, j: (j, 0, 0))],
        out_specs=pl.BlockSpec((1, cb, N1 // SUBLANES, SUBLANES * SUBLANES, LANES),
                               lambda p, c, j: (p, c, 0, j, 0)),
        out_shape=jax.ShapeDtypeStruct((B // 2, NC, N1 // SUBLANES, N2 * SUBLANES, LANES), jnp.uint32),
        compiler_params=_cparams("parallel", "parallel", "parallel"),
        name="hyena_stage1",
    )(z, tabs["m_fwd"])


def _mid_kernel(a_ref, g_ref, mf_ref, mi_ref, o_ref):
    cb = a_ref.shape[1]
    N2 = g_ref.shape[2]
    for s in range(SUBLANES):
        re, im = _unpack_complex(_gather_cols(lambda c: _rows_strided(a_ref.at[0, c, 0], s), cb))
        a = jnp.concatenate([re, im], axis=0).astype(BF16)
        x = _dot(mf_ref[...], a)
        xr, xi = x[:N2], x[N2:]
        gr, gi = _unpack_complex(_gather_cols(lambda c: g_ref[c, s], cb))
        y = jnp.concatenate([xr * gr - xi * gi, xr * gi + xi * gr], axis=0).astype(BF16)
        b = _dot(mi_ref[...], y)
        _store_tiles(lambda c: o_ref.at[0, c], _pack_complex(b[:N2], b[N2:]), cb, s)


def _hyena_stage2(a, g, order, tabs, cb=8):
    N1, N2 = tabs["N1"], tabs["N2"]
    P, NC = a.shape[:2]
    cb = min(cb, NC)
    ncb = NC // cb
    return pl.pallas_call(
        _mid_kernel,
        grid=(N1 // SUBLANES, ncb, P),
        in_specs=[
            pl.BlockSpec((1, cb, 1, N2 * SUBLANES, LANES), lambda i, c, p: (p, c, i, 0, 0)),
            pl.BlockSpec((cb, SUBLANES, N2, LANES), lambda i, c, p: (order * ncb + c, i, 0, 0)),
            pl.BlockSpec((2 * N2, 2 * N2), lambda i, c, p: (0, 0)),
            pl.BlockSpec((2 * N2, 2 * N2), lambda i, c, p: (0, 0)),
        ],
        out_specs=pl.BlockSpec((1, cb, N2 // SUBLANES, SUBLANES * SUBLANES, LANES),
                               lambda i, c, p: (p, c, 0, i, 0)),
        out_shape=jax.ShapeDtypeStruct((P, NC, N2 // SUBLANES, N1 * SUBLANES, LANES), jnp.uint32),
        compiler_params=_cparams("parallel", "parallel", "arbitrary"),
        name="hyena_stage2",
    )(a, g, tabs["m2_fwd"], tabs["m2_inv"])


def _inv1_kernel(b_ref, m_ref, gate_ref, z_ref, bias_ref, o_ref, *, z_slabbed, final):
    cb = b_ref.shape[1]
    N1 = m_ref.shape[1]
    H = N1 // 2
    bias = bias_ref[...]
    for s in range(SUBLANES):
        br, bi_ = _unpack_complex(_gather_cols(lambda c: _rows_strided(b_ref.at[0, c, 0], s), cb))
        b = jnp.concatenate([br, bi_], axis=0).astype(BF16)
        conv = _dot(m_ref[s], b)
        for bi in range(2):
            gate = _gather_cols(lambda c: _rows_strided(gate_ref.at[0, bi, c, 0], s), cb)
            if z_slabbed:
                z = _gather_cols(lambda c: _rows_strided(z_ref.at[0, bi, c, 0], s), cb)
            else:
                z = _gather_cols(lambda c: z_ref[bi, c, s], cb)
            out = gate * (conv[bi * H:(bi + 1) * H] + z * bias)
            for c in range(cb):
                piece = out[:, c * LANES:(c + 1) * LANES]
                if final:
                    o_ref.at[bi, c, 0][pl.ds(s, H, stride=SUBLANES), :] = piece
                else:
                    o_ref[bi, c, s] = piece


def _hyena_stage3(bq, ucl, gate_idx, z, z_idx, bias, tabs, final):
    N1, N2 = tabs["N1"], tabs["N2"]
    H = N1 // 2
    P, NC = bq.shape[:2]
    cb = min(NC, 4)
    z_slabbed = z_idx is not None
    slab_blk = (1, 2, cb, 1, H * SUBLANES, LANES)
    n2m_blk = (2, cb, SUBLANES, H, LANES)
    if z_slabbed:
        z_spec = pl.BlockSpec(slab_blk, lambda p, c, j: (z_idx, p, c, j, 0, 0))
    else:
        z_spec = pl.BlockSpec(n2m_blk, lambda p, c, j: (p, c, j, 0, 0))
    if final:
        o_spec = pl.BlockSpec((2, cb, 1, H * SUBLANES, LANES), lambda p, c, j: (p, c, j, 0, 0))
        o_shape = (2 * P, NC, N2 // SUBLANES, H * SUBLANES, LANES)
    else:
        o_spec = pl.BlockSpec(n2m_blk, lambda p, c, j: (p, c, j, 0, 0))
        o_shape = (2 * P, NC, N2, H, LANES)
    return pl.pallas_call(
        functools.partial(_inv1_kernel, z_slabbed=z_slabbed, final=final),
        grid=(P, NC // cb, N2 // SUBLANES),
        in_specs=[
            pl.BlockSpec((1, cb, 1, N1 * SUBLANES, LANES), lambda p, c, j: (p, c, j, 0, 0)),
            pl.BlockSpec((SUBLANES, N1, 2 * N1), lambda p, c, j: (j, 0, 0)),
            pl.BlockSpec(slab_blk, lambda p, c, j: (gate_idx, p, c, j, 0, 0)),
            z_spec,
            pl.BlockSpec((1, cb * LANES), lambda p, c, j: (0, c)),
        ],
        out_specs=o_spec,
        out_shape=jax.ShapeDtypeStruct(o_shape, F32),
        compiler_params=_cparams("parallel", "parallel", "parallel"),
        name="hyena_stage3",
    )(bq, tabs["m_inv"], ucl, z, bias[None, :])


def _hyena_mixer(ucl, g_spec, hy_bias, tabs):
    z, z_idx = ucl, 2
    for order in range(HY_ORDER):
        a = _hyena_stage1(z, z_idx, tabs)
        bq = _hyena_stage2(a, g_spec, order, tabs)
        z = _hyena_stage3(bq, ucl, order, z, z_idx, hy_bias[order], tabs,
                          final=(order == HY_ORDER - 1))
        z_idx = None
    return z


def _attn_kernel(sink_ref, q_ref, kp_ref, kc_ref, kn_ref, vp_ref, vc_ref, vn_ref, nw_ref,
                 o_ref, y_scr, *, L):
    i = pl.program_id(1)
    tq = q_ref.shape[1]
    span = tq + 2 * WINDOW
    k = jnp.concatenate([kp_ref[0], kc_ref[0], kn_ref[0]], axis=0)
    v = jnp.concatenate([vp_ref[0], vc_ref[0], vn_ref[0]], axis=0)
    row = lax.broadcasted_iota(jnp.int32, (tq, span), 0)
    col = lax.broadcasted_iota(jnp.int32, (tq, span), 1)
    rel = col - row
    kpos = i * tq - WINDOW + col
    ok = (rel >= 0) & (rel <= 2 * WINDOW) & (kpos >= 0) & (kpos < L)
    bias = jnp.where(ok, 0.0, NEG_BIG).astype(F32)
    for h in range(N_KV_HEADS):
        kh = k[:, h * HEAD_DIM:(h + 1) * HEAD_DIM]
        vh = v[:, h * HEAD_DIM:(h + 1) * HEAD_DIM]
        for g in range(Q_PER_KV):
            qh_i = h * Q_PER_KV + g
            qh = q_ref[0, :, qh_i * HEAD_DIM:(qh_i + 1) * HEAD_DIM]
            s = lax.dot_general(qh, kh, (((1,), (1,)), ((), ())),
                                preferred_element_type=F32) + bias
            sink = sink_ref[qh_i]
            m = jnp.maximum(jnp.max(s, axis=-1, keepdims=True), sink)
            p = jnp.exp(s - m)
            denom = jnp.sum(p, axis=-1, keepdims=True) + jnp.exp(sink - m)
            o = _dot(p.astype(BF16), vh) / denom
            y_scr[:, qh_i * HEAD_DIM:(qh_i + 1) * HEAD_DIM] = o
    y = y_scr[...]
    inv = lax.rsqrt(jnp.mean(y * y, axis=-1, keepdims=True) + EPS)
    o_ref[0] = (y * inv * nw_ref[...]).astype(o_ref.dtype)


def _windowed_attention(q, k, v, sink, out_norm_w, tq=256):
    B, L, _ = q.shape
    rq = tq // WINDOW
    nwb = L // WINDOW
    cur = lambda b, i, s: (b, i, 0)
    prv = lambda b, i, s: (b, jnp.maximum(i * rq - 1, 0), 0)
    nxt = lambda b, i, s: (b, jnp.minimum((i + 1) * rq, nwb - 1), 0)
    grid_spec = pltpu.PrefetchScalarGridSpec(
        num_scalar_prefetch=1,
        grid=(B, L // tq),
        in_specs=[
            pl.BlockSpec((1, tq, ATTN_WIDTH), cur),
            pl.BlockSpec((1, WINDOW, KV_WIDTH), prv),
            pl.BlockSpec((1, tq, KV_WIDTH), cur),
            pl.BlockSpec((1, WINDOW, KV_WIDTH), nxt),
            pl.BlockSpec((1, WINDOW, KV_WIDTH), prv),
            pl.BlockSpec((1, tq, KV_WIDTH), cur),
            pl.BlockSpec((1, WINDOW, KV_WIDTH), nxt),
            pl.BlockSpec((1, ATTN_WIDTH), lambda b, i, s: (0, 0)),
        ],
        out_specs=pl.BlockSpec((1, tq, ATTN_WIDTH), cur),
        scratch_shapes=[pltpu.VMEM((tq, ATTN_WIDTH), F32)],
    )
    return pl.pallas_call(
        functools.partial(_attn_kernel, L=L),
        grid_spec=grid_spec,
        out_shape=jax.ShapeDtypeStruct((B, L, ATTN_WIDTH), BF16),
        compiler_params=_cparams("parallel", "parallel"),
        name="banded_attention",
    )(sink.astype(F32), q, k, k, k, v, v, v, out_norm_w[None, :])


def _outproj_kernel(x_ref, a_ref, hy_ref, hw_ref, wa_ref, wh_ref, nw_ref, wr_ref,
                    x1_ref, h2_ref, lg_ref):
    nslab = hy_ref.shape[2]
    hy = jnp.concatenate(
        [_gather_cols(lambda c: hy_ref[0, c, j, n1 * SUBLANES:(n1 + 1) * SUBLANES, :], hy_ref.shape[1])
         for n1 in range(hy_ref.shape[3] // SUBLANES) for j in range(nslab)],
        axis=0)
    hinv = lax.rsqrt(jnp.mean(hy * hy, axis=-1, keepdims=True) + EPS)
    hy_n = (hy * hinv * hw_ref[...]).astype(BF16)
    x1 = x_ref[...] + _dot(a_ref[...], wa_ref[...]) + _dot(hy_n, wh_ref[...])
    x1_ref[...] = x1
    inv = lax.rsqrt(jnp.mean(x1 * x1, axis=-1, keepdims=True) + EPS)
    h2 = x1 * inv * nw_ref[...]
    h2_ref[...] = h2.astype(BF16)
    h_hi, h_lo = _split_bf16(h2)
    lg_ref[...] = _dot(h_hi, wr_ref[0]) + _dot(h_lo, wr_ref[0]) + _dot(h_hi, wr_ref[1])


def _out_proj(x2d, attn_n, hy, hy_norm_w, w_out_a, w_out_h, norm2_w, w_router, tm=512):
    T = x2d.shape[0]
    _, NC, J, LJ, _ = hy.shape
    lt = (LJ * J) // tm
    row = lambda i: (i, 0)
    const = lambda i: (0, 0)
    return pl.pallas_call(
        _outproj_kernel,
        grid=(T // tm,),
        in_specs=[
            pl.BlockSpec((tm, D_MODEL), row),
            pl.BlockSpec((tm, ATTN_WIDTH), row),
            pl.BlockSpec((1, NC, J, tm // J, LANES), lambda i: (i // lt, 0, 0, i % lt, 0)),
            pl.BlockSpec((1, HY_WIDTH), const),
            pl.BlockSpec((ATTN_WIDTH, D_MODEL), const),
            pl.BlockSpec((HY_WIDTH, D_MODEL), const),
            pl.BlockSpec((1, D_MODEL), const),
            pl.BlockSpec((2, D_MODEL, ROUTER_PAD), lambda i: (0, 0, 0)),
        ],
        out_specs=[
            pl.BlockSpec((tm, D_MODEL), row),
            pl.BlockSpec((tm, D_MODEL), row),
            pl.BlockSpec((tm, ROUTER_PAD), row),
        ],
        out_shape=[
            jax.ShapeDtypeStruct((T, D_MODEL), F32),
            jax.ShapeDtypeStruct((T, D_MODEL), BF16),
            jax.ShapeDtypeStruct((T, ROUTER_PAD), F32),
        ],
        compiler_params=_cparams("parallel"),
        name="out_proj",
    )(x2d, attn_n, hy, hy_norm_w[None, :], w_out_a, w_out_h, norm2_w[None, :], w_router)


def _moe_kernel(be_ref, nu_ref, x_ref, wg_ref, wu_ref, wd_ref, o_ref):
    @pl.when(pl.program_id(0) < nu_ref[0])
    def _():
        x = x_ref[...]
        g = _dot(x, wg_ref[0])
        u = _dot(x, wu_ref[0])
        hid = (g * jax.nn.sigmoid(g)) * u
        o_ref[...] = _dot(hid.astype(BF16), wd_ref[0]).astype(o_ref.dtype)


def _expert_blocks(xs, block_expert, n_used, w_gate, w_up, w_down):
    P = xs.shape[0]
    bm = MOE_BLOCK_ROWS
    blk = lambda b, be, nu: (jnp.minimum(b, nu[0] - 1), 0)
    wsel = lambda b, be, nu: (be[jnp.minimum(b, nu[0] - 1)], 0, 0)
    grid_spec = pltpu.PrefetchScalarGridSpec(
        num_scalar_prefetch=2,
        grid=(P // bm,),
        in_specs=[
            pl.BlockSpec((bm, D_MODEL), blk),
            pl.BlockSpec((1, D_MODEL, D_EXPERT), wsel),
            pl.BlockSpec((1, D_MODEL, D_EXPERT), wsel),
            pl.BlockSpec((1, D_EXPERT, D_MODEL), wsel),
        ],
        out_specs=pl.BlockSpec((bm, D_MODEL), blk),
    )
    return pl.pallas_call(
        _moe_kernel,
        grid_spec=grid_spec,
        out_shape=jax.ShapeDtypeStruct((P, D_MODEL), BF16),
        compiler_params=_cparams("arbitrary"),
        name="moe_experts",
    )(block_expert, n_used, xs, w_gate, w_up, w_down)


def _hier_moe(x1, h2, logits, b_rg, b_re, w_gate, w_up, w_down):
    T = h2.shape[0]
    g_logits = logits[:, :N_EXPERT_GROUPS] + b_rg
    e_logits = (logits[:, N_EXPERT_GROUPS:N_EXPERT_GROUPS + N_EXPERTS] + b_re)
    e_logits = e_logits.reshape(T, N_EXPERT_GROUPS, EXPERTS_PER_GROUP)
    g_idx = jnp.argmax(g_logits, axis=-1)
    g_gate = jnp.take_along_axis(jax.nn.softmax(g_logits, axis=-1), g_idx[:, None], axis=1)
    e_sel = jnp.take_along_axis(e_logits, g_idx[:, None, None], axis=1)[:, 0]
    top_val, top_loc = lax.top_k(e_sel, TOP_K)
    weights = g_gate * jax.nn.softmax(top_val, axis=-1)
    experts = (g_idx[:, None] * EXPERTS_PER_GROUP + top_loc).astype(jnp.int32)

    A = T * TOP_K
    bm = MOE_BLOCK_ROWS
    flat_e = experts.reshape(-1)
    onehot = (flat_e[:, None] == jnp.arange(N_EXPERTS, dtype=jnp.int32)[None, :]).astype(jnp.int32)
    csum = jnp.cumsum(onehot, axis=0)
    counts = csum[-1]
    rank = jnp.sum((csum - onehot) * onehot, axis=1)
    padded = ((counts + bm - 1) // bm) * bm
    pend = jnp.cumsum(padded)
    pstart = pend - padded
    seg_start = jnp.cumsum(counts) - counts
    dest = (pstart[flat_e] + rank).astype(jnp.int32)
    n_blocks = (A + bm - 1) // bm + N_EXPERTS
    P = n_blocks * bm
    order = jnp.argsort(flat_e, stable=True).astype(jnp.int32)
    q = jnp.arange(P, dtype=jnp.int32)
    e_q = jnp.minimum(jnp.sum((pend[None, :] <= q[:, None]).astype(jnp.int32), axis=1), N_EXPERTS - 1)
    r_q = jnp.minimum(q - pstart[e_q], jnp.maximum(counts[e_q] - 1, 0))
    src = jnp.clip(seg_start[e_q] + r_q, 0, A - 1)
    buf_tok = order.at[src].get(mode="promise_in_bounds") // TOP_K
    block_expert = e_q.reshape(n_blocks, bm)[:, 0]
    n_used = (pend[-1] // bm).astype(jnp.int32)[None]

    xs = h2.at[buf_tok].get(mode="promise_in_bounds")
    yb = _expert_blocks(xs, block_expert, n_used, w_gate, w_up, w_down)
    d = dest.reshape(T, TOP_K)
    y0 = yb.at[d[:, 0]].get(mode="promise_in_bounds").astype(F32)
    y1 = yb.at[d[:, 1]].get(mode="promise_in_bounds").astype(F32)
    return x1 + (weights[:, 0:1] * y0 + weights[:, 1:2] * y1)


def _encoder_layer(x, p):
    B, L, D = x.shape
    T = B * L
    x2d = x.reshape(T, D)
    q, k, v = _qkv_proj(x2d, L, p["norm1_w"], p["w_qkv"], p["q_norm_w"], p["k_norm_w"])
    attn_n = _windowed_attention(q.reshape(B, L, ATTN_WIDTH), k.reshape(B, L, KV_WIDTH),
                                 v.reshape(B, L, KV_WIDTH), p["attn_sink"], p["attn_out_norm_w"])
    u = _u_proj(x2d, p["norm1_w"], p["w_u"])
    uc = _short_conv(u.reshape(B, L, 3 * HY_WIDTH), p["conv_w"], p["conv_b"])
    tabs = _fft_tables(L)
    g_spec = _hyena_filter_spectrum(L, tabs, p["filt_w1"], p["filt_b1"], p["filt_w2"], p["filt_b2"],
                                    p["filt_w3"], p["filt_b3"], p["filt_w4"], p["filt_freq"])
    hy = _hyena_mixer(uc, g_spec, p["hy_bias"], tabs)
    x1, h2, logits = _out_proj(x2d, attn_n.reshape(T, ATTN_WIDTH), hy, p["hy_out_norm_w"],
                               p["w_out_a"], p["w_out_h"], p["norm2_w"], p["w_router"])
    out = _hier_moe(x1, h2, logits, p["b_route_group"], p["b_route_expert"],
                    p["w_gate"], p["w_up"], p["w_down"])
    return out.reshape(B, L, D)


def kernel(x_prompt, x_sample, norm1_w, w_in, q_norm_w, k_norm_w, attn_sink, conv_w, conv_b, filt_w1, filt_b1, filt_w2, filt_b2, filt_w3, filt_b3, filt_w4, filt_freq, hy_bias, attn_out_norm_w, hy_out_norm_w, w_out, norm2_w, w_route_group, b_route_group, w_route_expert, b_route_expert, w_gate, w_up, w_down):
    depth = norm1_w.shape[0]

    def layer_params(l):
        w_r = jnp.zeros((D_MODEL, ROUTER_PAD), F32)
        w_r = w_r.at[:, :N_EXPERT_GROUPS].set(w_route_group[l])
        w_r = w_r.at[:, N_EXPERT_GROUPS:N_EXPERT_GROUPS + N_EXPERTS].set(w_route_expert[l])
        r_hi, r_lo = _split_bf16(w_r)
        return dict(
            norm1_w=norm1_w[l], w_qkv=w_in[l][:, :QKV_WIDTH].astype(BF16),
            w_u=w_in[l][:, QKV_WIDTH:].astype(BF16), q_norm_w=q_norm_w[l], k_norm_w=k_norm_w[l],
            attn_sink=attn_sink[l], conv_w=conv_w[l], conv_b=conv_b[l],
            filt_w1=filt_w1[l], filt_b1=filt_b1[l], filt_w2=filt_w2[l], filt_b2=filt_b2[l],
            filt_w3=filt_w3[l], filt_b3=filt_b3[l], filt_w4=filt_w4[l], filt_freq=filt_freq[l],
            hy_bias=hy_bias[l], attn_out_norm_w=attn_out_norm_w[l], hy_out_norm_w=hy_out_norm_w[l],
            w_out_a=w_out[l][:ATTN_WIDTH].astype(BF16), w_out_h=w_out[l][ATTN_WIDTH:].astype(BF16),
            norm2_w=norm2_w[l], w_router=jnp.stack([r_hi, r_lo]),
            b_route_group=b_route_group[l], b_route_expert=b_route_expert[l],
            w_gate=w_gate[l].astype(BF16), w_up=w_up[l].astype(BF16), w_down=w_down[l].astype(BF16))

    params = [layer_params(l) for l in range(depth)]

    def trunk(x):
        for p in params:
            x = _encoder_layer(x, p)
        return x

    return (trunk(x_prompt), trunk(x_sample))
```

```python
import functools
import math

import jax
import jax.numpy as jnp
from jax import lax
from jax.experimental import pallas as pl
from jax.experimental.pallas import tpu as pltpu

F32 = jnp.float32
BF16 = jnp.bfloat16

D_MODEL = 2048
HEAD_DIM = 64
N_Q_HEADS = 16
N_KV_HEADS = 4
Q_PER_KV = N_Q_HEADS // N_KV_HEADS
ATTN_WIDTH = N_Q_HEADS * HEAD_DIM
KV_WIDTH = N_KV_HEADS * HEAD_DIM
QK_WIDTH = ATTN_WIDTH + KV_WIDTH
QKV_WIDTH = ATTN_WIDTH + 2 * KV_WIDTH
WINDOW = 128
ROT_DIM = HEAD_DIM // 4
ROPE_THETA = 500000.0
HY_WIDTH = D_MODEL - ATTN_WIDTH
HY_ORDER = 2
HY_EMB_DIM = 33
HY_FILTER_HIDDEN = 64
HY_DECAY_TARGET = 1e-2
HY_FAST_DECAY_PCT = 0.3
HY_SLOW_DECAY_PCT = 1.5
N_EXPERT_GROUPS = 4
EXPERTS_PER_GROUP = 8
N_EXPERTS = N_EXPERT_GROUPS * EXPERTS_PER_GROUP
TOP_K = 2
D_EXPERT = 1024
EPS = 1e-6

LANES = 128
SUBLANES = 8
FFT_N2 = 128
MOE_BLOCK_ROWS = 256
ROUTER_PAD = 128
VMEM_LIMIT = 56 * 1024 * 1024
NEG_BIG = -1e30
HEADS_PER_PASS = Q_PER_KV


def _cparams(*sem):
    return pltpu.CompilerParams(dimension_semantics=sem, vmem_limit_bytes=VMEM_LIMIT)


def _split_bf16(a):
    hi = a.astype(BF16)
    lo = (a - hi.astype(F32)).astype(BF16)
    return hi, lo


def _dot(a, b):
    return jnp.dot(a, b, preferred_element_type=F32)


def _dot3(a, b):
    a_hi, a_lo = _split_bf16(a)
    b_hi, b_lo = _split_bf16(b)
    return _dot(a_hi, b_hi) + _dot(a_lo, b_hi) + _dot(a_hi, b_lo)


def _qkv_kernel(x_ref, nw_ref, w_ref, seg_ref, hw_ref, rc_ref, rs1_ref, rs2_ref,
                q_ref, k_ref, v_ref):
    x = x_ref[...]
    inv = lax.rsqrt(jnp.mean(x * x, axis=-1, keepdims=True) + EPS)
    h = (x * inv * nw_ref[...]).astype(BF16)
    acc = _dot(h, w_ref[...])
    qk = acc[:, :QK_WIDTH]
    hi, lo = _split_bf16(qk * qk)
    seg = seg_ref[...]
    w = seg.shape[0]
    ms = jnp.concatenate(
        [_dot(hi[:, c * w:(c + 1) * w], seg) + _dot(lo[:, c * w:(c + 1) * w], seg)
         for c in range(QK_WIDTH // w)], axis=-1)
    xn = qk * lax.rsqrt(ms + EPS) * hw_ref[...]
    reps = QK_WIDTH // LANES
    rc = jnp.tile(rc_ref[...], (1, reps))
    rs1 = jnp.tile(rs1_ref[...], (1, reps))
    rs2 = jnp.tile(rs2_ref[...], (1, reps))
    half = ROT_DIM // 2
    y = xn * rc + pltpu.roll(xn, QK_WIDTH - half, 1) * rs1 + pltpu.roll(xn, half, 1) * rs2
    q_ref[...] = (y[:, :ATTN_WIDTH] * (HEAD_DIM ** -0.5)).astype(BF16)
    k_ref[...] = y[:, ATTN_WIDTH:].astype(BF16)
    v_ref[0] = acc[:, QK_WIDTH:].T.astype(BF16)


def _rope_tables(L):
    half = ROT_DIM // 2
    inv_freq = jnp.power(ROPE_THETA, -jnp.arange(half, dtype=F32) * 2.0 / ROT_DIM)
    ang = jnp.arange(L, dtype=F32)[:, None] * inv_freq[None, :]
    d = jnp.arange(LANES) % HEAD_DIM
    ang_l = ang[:, d % half]
    cos, sin = jnp.cos(ang_l), jnp.sin(ang_l)
    rc = jnp.where(d < ROT_DIM, cos, 1.0)
    rs1 = jnp.where(d < half, -sin, 0.0)
    rs2 = jnp.where((d >= half) & (d < ROT_DIM), sin, 0.0)
    return rc.astype(F32), rs1.astype(F32), rs2.astype(F32)


def _qkv_proj(x2d, L, norm1_w, w_qkv, q_norm_w, k_norm_w, tm=512):
    T = x2d.shape[0]
    seg_w = 256
    seg = (jnp.arange(seg_w)[:, None] // HEAD_DIM == jnp.arange(seg_w)[None, :] // HEAD_DIM)
    seg = (seg.astype(F32) / HEAD_DIM).astype(BF16)
    hw = jnp.concatenate([jnp.tile(q_norm_w, N_Q_HEADS), jnp.tile(k_norm_w, N_KV_HEADS)])[None, :]
    rc, rs1, rs2 = _rope_tables(L)
    lt = L // tm
    row = lambda i: (i, 0)
    const = lambda i: (0, 0)
    pos = lambda i: (i % lt, 0)
    return pl.pallas_call(
        _qkv_kernel,
        grid=(T // tm,),
        in_specs=[
            pl.BlockSpec((tm, D_MODEL), row),
            pl.BlockSpec((1, D_MODEL), const),
            pl.BlockSpec((D_MODEL, QKV_WIDTH), const),
            pl.BlockSpec((seg_w, seg_w), const),
            pl.BlockSpec((1, QK_WIDTH), const),
            pl.BlockSpec((tm, LANES), pos),
            pl.BlockSpec((tm, LANES), pos),
            pl.BlockSpec((tm, LANES), pos),
        ],
        out_specs=[
            pl.BlockSpec((tm, ATTN_WIDTH), row),
            pl.BlockSpec((tm, KV_WIDTH), row),
            pl.BlockSpec((1, KV_WIDTH, tm), lambda i: (i // lt, 0, i % lt)),
        ],
        out_shape=[
            jax.ShapeDtypeStruct((T, ATTN_WIDTH), BF16),
            jax.ShapeDtypeStruct((T, KV_WIDTH), BF16),
            jax.ShapeDtypeStruct((T // L, KV_WIDTH, L), BF16),
        ],
        compiler_params=_cparams("parallel"),
        name="qkv_proj",
    )(x2d, norm1_w[None, :], w_qkv, seg, hw, rc, rs1, rs2)


def _uproj_kernel(x_ref, nw_ref, w_ref, o_ref, h_scr):
    @pl.when(pl.program_id(1) == 0)
    def _():
        x = x_ref[...]
        inv = lax.rsqrt(jnp.mean(x * x, axis=-1, keepdims=True) + EPS)
        h_scr[...] = (x * inv * nw_ref[...]).astype(BF16)

    o_ref[...] = _dot(h_scr[...], w_ref[...])


def _u_proj(x2d, norm1_w, w_u, tm=1024, tn=1024):
    T = x2d.shape[0]
    n = w_u.shape[1]
    return pl.pallas_call(
        _uproj_kernel,
        grid=(T // tm, n // tn),
        in_specs=[
            pl.BlockSpec((tm, D_MODEL), lambda i, j: (i, 0)),
            pl.BlockSpec((1, D_MODEL), lambda i, j: (0, 0)),
            pl.BlockSpec((D_MODEL, tn), lambda i, j: (0, j)),
        ],
        out_specs=pl.BlockSpec((tm, tn), lambda i, j: (i, j)),
        out_shape=jax.ShapeDtypeStruct((T, n), F32),
        scratch_shapes=[pltpu.VMEM((tm, D_MODEL), BF16)],
        compiler_params=_cparams("parallel", "arbitrary"),
        name="u_proj",
    )(x2d, norm1_w[None, :], w_u)


def _sconv_kernel(u_ref, up_ref, un_ref, w_ref, b_ref, o_ref):
    i = pl.program_id(1)
    last = pl.num_programs(1) - 1
    u = u_ref[0]
    tl = u.shape[0]
    prev_row = jnp.where(i > 0, up_ref[0, 7:8, :], 0.0)
    next_row = jnp.where(i < last, un_ref[0, 0:1, :], 0.0)
    rid = lax.broadcasted_iota(jnp.int32, u.shape, 0)
    um = jnp.where(rid == 0, prev_row, pltpu.roll(u, 1, 0))
    up = jnp.where(rid == tl - 1, next_row, pltpu.roll(u, tl - 1, 0))
    w = w_ref[...]
    val = w[0:1] * um + w[1:2] * u + w[2:3] * up + b_ref[...]
    for n1 in range(tl // FFT_N2):
        for j in range(FFT_N2 // SUBLANES):
            r0 = n1 * FFT_N2 + j * SUBLANES
            for c in range(val.shape[1] // LANES):
                o_ref[0, 0, c, j, n1 * SUBLANES:(n1 + 1) * SUBLANES, :] = (
                    val[r0:r0 + SUBLANES, c * LANES:(c + 1) * LANES])


def _short_conv(u3d, conv_w, conv_b, tl=1024):
    B, L, W = u3d.shape
    C = HY_WIDTH
    nparts = W // C
    r8 = tl // 8
    J = FFT_N2 // SUBLANES
    return pl.pallas_call(
        _sconv_kernel,
        grid=(B, L // tl, nparts),
        in_specs=[
            pl.BlockSpec((1, tl, C), lambda b, i, c: (b, i, c)),
            pl.BlockSpec((1, 8, C), lambda b, i, c: (b, jnp.maximum(i * r8 - 1, 0), c)),
            pl.BlockSpec((1, 8, C), lambda b, i, c: (b, jnp.minimum((i + 1) * r8, L // 8 - 1), c)),
            pl.BlockSpec((3, C), lambda b, i, c: (0, c)),
            pl.BlockSpec((1, C), lambda b, i, c: (0, c)),
        ],
        out_specs=pl.BlockSpec((1, 1, C // LANES, J, tl // J, LANES), lambda b, i, c: (c, b, 0, 0, i, 0)),
        out_shape=jax.ShapeDtypeStruct((nparts, B, C // LANES, J, L // J, LANES), F32),
        compiler_params=_cparams("parallel", "parallel", "parallel"),
        name="short_conv",
    )(u3d, u3d, u3d, conv_w, conv_b[None, :])


def _fft_tables(L):
    N = 2 * L
    N2 = FFT_N2
    N1 = N // N2
    H = N1 // 2
    k1 = jnp.arange(N1, dtype=jnp.int32)
    n2 = jnp.arange(N2, dtype=jnp.int32)
    a1 = ((k1[:, None] * k1[None, :]) % N1).astype(F32) * (2.0 * math.pi / N1)
    c1, s1 = jnp.cos(a1), jnp.sin(a1)
    at = (n2[:, None] * k1[None, :]).astype(F32) * (2.0 * math.pi / N)
    ct, st = jnp.cos(at), jnp.sin(at)
    c = c1[None] * ct[:, :, None] - s1[None] * st[:, :, None]
    s = s1[None] * ct[:, :, None] + c1[None] * st[:, :, None]
    m_filt = jnp.concatenate([c, -s], axis=1).astype(BF16)
    ch, sh = c[:, :, :H], s[:, :, :H]
    m_fwd = jnp.concatenate([jnp.concatenate([ch, sh], axis=2),
                             jnp.concatenate([-sh, ch], axis=2)], axis=1)
    cT = c1[None, :H] * ct[:, None, :] - s1[None, :H] * st[:, None, :]
    sT = s1[None, :H] * ct[:, None, :] + c1[None, :H] * st[:, None, :]
    m_inv = jnp.concatenate([jnp.concatenate([cT, -sT], axis=2),
                             jnp.concatenate([sT, cT], axis=2)], axis=1)
    k2 = jnp.arange(N2, dtype=jnp.int32)
    ph = ((k2[:, None] * n2[None, :]) % N2).astype(F32) * (2.0 * math.pi / N2)
    c2, s2 = jnp.cos(ph), jnp.sin(ph)
    m2_fwd = jnp.concatenate([jnp.concatenate([c2, s2], axis=1),
                              jnp.concatenate([-s2, c2], axis=1)], axis=0)
    m2_inv = m2_fwd.T
    return dict(N1=N1, N2=N2, m_filt=m_filt, m_fwd=m_fwd.astype(BF16), m_inv=m_inv.astype(BF16),
                m2_fwd=m2_fwd.astype(BF16), m2_inv=m2_inv.astype(BF16))


def _filt1_kernel(w1_ref, b1_ref, w2_ref, b2_ref, w3_ref, b3_ref, w4_ref, a_ref, frl_ref, off_ref,
                  dl_ref, m_ref, o_ref, sabs_ref, *, L, N1, N2, nb):
    j = pl.program_id(0)
    C2 = HY_ORDER * HY_WIDTH
    H = N1 // 2
    rows = nb * N1
    ridx = lax.broadcasted_iota(jnp.int32, (rows, LANES), 0)
    lane = lax.broadcasted_iota(jnp.int32, (rows, LANES), 1)
    n2 = j * nb + ridx // N1
    n1 = ridx % N1
    r = n1 * N2 + n2
    lag = jnp.where(r < L, r, 2 * L - r)
    valid = (r != L).astype(F32)
    lagf = jnp.minimum(lag, L - 1).astype(F32)
    t = lagf / (L - 1)
    wpos = lagf * (2.0 * math.pi / L)
    phase = wpos * frl_ref[...] + off_ref[...]
    zemb = jnp.where(lane == 0, t, jnp.where(lane < HY_EMB_DIM, jnp.cos(phase), 0.0))
    half = rows // 2
    a = a_ref[...]
    h = jnp.concatenate([zemb[:half], zemb[half:]], axis=1)
    h = jnp.sin(a * (_dot3(h, w1_ref[...]) + b1_ref[...]))
    h = jnp.sin(a * (_dot3(h, w2_ref[...]) + b2_ref[...]))
    h = jnp.sin(a * (_dot3(h, w3_ref[...]) + b3_ref[...]))
    h_hi, h_lo = _split_bf16(h)
    decay = jnp.exp(-jnp.tile(t, (1, HY_WIDTH // LANES)) * dl_ref[...])
    decay = decay * jnp.tile(valid, (1, HY_WIDTH // LANES))
    decay2 = jnp.tile(decay, (1, HY_ORDER))

    @pl.when(j == 0)
    def _():
        sabs_ref[...] = jnp.zeros_like(sabs_ref)

    tot = jnp.zeros((8, C2), F32)
    for l in range(nb):
        side = (l * N1) // half
        r0 = l * N1 - side * half
        fs = slice(r0, r0 + H)
        bs = slice(r0 + H, r0 + N1)
        wf = w4_ref[side, :, :C2]
        wb = w4_ref[side, :, C2:]
        gf = _dot(h_hi[fs], wf) + _dot(h_lo[fs], wf)
        gb = _dot(h_hi[bs], wb) + _dot(h_lo[bs], wb)
        g = jnp.concatenate([gf, gb], axis=0) * decay2[l * N1:(l + 1) * N1]
        tot = tot + jnp.sum(jnp.abs(g).reshape(N1 // 8, 8, C2), axis=0)
        y = _dot(m_ref[l], g.astype(BF16))
        _store_tiles(lambda c: o_ref.at[c], _pack_complex(y[:N1], y[N1:]), C2 // LANES, l)
    sabs_ref[...] += tot


def _pack_complex(re, im):
    r = lax.bitcast_convert_type(re.astype(BF16).astype(F32), jnp.uint32)
    i = lax.bitcast_convert_type(im.astype(BF16).astype(F32), jnp.uint32)
    return r | (i >> 16)


def _unpack_complex(w):
    re = lax.bitcast_convert_type(w & jnp.uint32(0xFFFF0000), F32)
    im = lax.bitcast_convert_type(w << 16, F32)
    return re, im


def _store_tiles(dst, w, ncols, s):
    for c in range(ncols):
        d = dst(c)
        for t in range(w.shape[0] // SUBLANES):
            d[t, s * SUBLANES:(s + 1) * SUBLANES, :] = (
                w[t * SUBLANES:(t + 1) * SUBLANES, c * LANES:(c + 1) * LANES])


def _rows_strided(ref2d, s):
    return ref2d[pl.ds(s, ref2d.shape[0] // SUBLANES, stride=SUBLANES), :]


def _gather_cols(fn, ncols):
    return jnp.concatenate([fn(c) for c in range(ncols)], axis=1)


def _filt2_kernel(a_ref, m_ref, sc_ref, o_ref):
    cb = a_ref.shape[0]
    N2 = a_ref.shape[2] // SUBLANES
    for s in range(SUBLANES):
        re, im = _unpack_complex(_gather_cols(lambda c: _rows_strided(a_ref.at[c, 0], s), cb))
        a = jnp.concatenate([re, im], axis=0).astype(BF16)
        x = _dot(m_ref[...], a) * sc_ref[...]
        w = _pack_complex(x[:N2], x[N2:])
        for c in range(cb):
            o_ref[c, s] = w[:, c * LANES:(c + 1) * LANES]


def _block_diag2(w):
    z = jnp.zeros_like(w)
    return jnp.concatenate([jnp.concatenate([w, z], axis=1), jnp.concatenate([z, w], axis=1)], axis=0)


def _hyena_filter_spectrum(L, tabs, filt_w1, filt_b1, filt_w2, filt_b2, filt_w3, filt_b3,
                           filt_w4, filt_freq):
    N1, N2 = tabs["N1"], tabs["N2"]
    N = N1 * N2
    C2 = HY_ORDER * HY_WIDTH
    FH = HY_FILTER_HIDDEN
    nb = max(2, 512 // N1)
    bands = (HY_EMB_DIM - 1) // 2
    fr = jnp.linspace(1e-4, bands - 1, bands, dtype=F32)
    frl = jnp.zeros((LANES,), F32).at[1:1 + bands].set(fr).at[1 + bands:1 + 2 * bands].set(fr)[None, :]
    off = jnp.zeros((LANES,), F32).at[1 + bands:1 + 2 * bands].set(0.5 * math.pi)[None, :]
    w1p = jnp.zeros((LANES, FH), F32).at[:HY_EMB_DIM].set(filt_w1)
    w4 = filt_w4.astype(BF16)
    w4z = jnp.zeros_like(w4)
    w4s = jnp.stack([jnp.concatenate([w4, w4z], axis=0), jnp.concatenate([w4z, w4], axis=0)])
    two = lambda v: jnp.tile(v, 2)[None, :]
    max_decay = math.log(HY_DECAY_TARGET) / HY_FAST_DECAY_PCT
    min_decay = math.log(HY_DECAY_TARGET) / HY_SLOW_DECAY_PCT
    deltas = jnp.abs(jnp.linspace(min_decay, max_decay, HY_WIDTH, dtype=F32))[None, :]
    full = lambda shape: pl.BlockSpec(shape, lambda j: (0,) * len(shape))
    a_g, sabs = pl.pallas_call(
        functools.partial(_filt1_kernel, L=L, N1=N1, N2=N2, nb=nb),
        grid=(N2 // nb,),
        in_specs=[
            full((2 * LANES, 2 * FH)), full((1, 2 * FH)),
            full((2 * FH, 2 * FH)), full((1, 2 * FH)),
            full((2 * FH, 2 * FH)), full((1, 2 * FH)),
            full((2, 2 * FH, 2 * C2)), full((1, 2 * FH)),
            full((1, LANES)), full((1, LANES)), full((1, HY_WIDTH)),
            pl.BlockSpec((nb, 2 * N1, N1), lambda j: (j, 0, 0)),
        ],
        out_specs=[
            pl.BlockSpec((C2 // LANES, N1 // SUBLANES, nb * SUBLANES, LANES), lambda j: (0, 0, j, 0)),
            pl.BlockSpec((8, C2), lambda j: (0, 0)),
        ],
        out_shape=[
            jax.ShapeDtypeStruct((C2 // LANES, N1 // SUBLANES, N2 * SUBLANES, LANES), jnp.uint32),
            jax.ShapeDtypeStruct((8, C2), F32),
        ],
        compiler_params=_cparams("arbitrary"),
        name="hyena_filter_stage1",
    )(_block_diag2(w1p), two(filt_b1), _block_diag2(filt_w2), two(filt_b2), _block_diag2(filt_w3),
      two(filt_b3), w4s, two(filt_freq), frl, off, deltas, tabs["m_filt"])
    scale = (1.0 / (jnp.sum(sabs, axis=0) * N))[None, :]
    cb = 4
    return pl.pallas_call(
        _filt2_kernel,
        grid=(N1 // SUBLANES, C2 // (cb * LANES)),
        in_specs=[
            pl.BlockSpec((cb, 1, N2 * SUBLANES, LANES), lambda i, c: (c, i, 0, 0)),
            pl.BlockSpec((2 * N2, 2 * N2), lambda i, c: (0, 0)),
            pl.BlockSpec((1, cb * LANES), lambda i, c: (0, c)),
        ],
        out_specs=pl.BlockSpec((cb, SUBLANES, N2, LANES), lambda i, c: (c, i, 0, 0)),
        out_shape=jax.ShapeDtypeStruct((C2 // LANES, N1, N2, LANES), jnp.uint32),
        compiler_params=_cparams("parallel", "parallel"),
        name="hyena_filter_stage2",
    )(a_g, tabs["m2_fwd"], scale)


def _fwd1_kernel(x_ref, m_ref, o_ref, *, slabbed):
    cb = o_ref.shape[1]
    N1 = m_ref.shape[2]
    for s in range(SUBLANES):
        if slabbed:
            rows = [_gather_cols(lambda c: _rows_strided(x_ref.at[0, bi, c, 0], s), cb) for bi in range(2)]
        else:
            rows = [_gather_cols(lambda c: x_ref[bi, c, s], cb) for bi in range(2)]
        xl = jnp.concatenate(rows, axis=0).astype(BF16)
        y = _dot(m_ref[s], xl)
        _store_tiles(lambda c: o_ref.at[0, c], _pack_complex(y[:N1], y[N1:]), cb, s)


def _hyena_stage1(z, which, tabs):
    N1, N2 = tabs["N1"], tabs["N2"]
    H = N1 // 2
    slabbed = which is not None
    B, NC = (z.shape[1], z.shape[2]) if slabbed else (z.shape[0], z.shape[1])
    cb = min(NC, 4)
    if slabbed:
        x_spec = pl.BlockSpec((1, 2, cb, 1, H * SUBLANES, LANES), lambda p, c, j: (which, p, c, j, 0, 0))
    else:
        x_spec = pl.BlockSpec((2, cb, SUBLANES, H, LANES), lambda p, c, j: (p, c, j, 0, 0))
    return pl.pallas_call(
        functools.partial(_fwd1_kernel, slabbed=slabbed),
        grid=(B // 2, NC // cb, N2 // SUBLANES),
        in_specs=[x_spec, pl.BlockSpec((SUBLANES, 2 * N1, N1), lambda p, c, j: (j, 0, 0))],
        out_specs=pl.BlockSpec((1, cb, N1 // SUBLANES, SUBLANES * SUBLANES, LANES),
                               lambda p, c, j: (p, c, 0, j, 0)),
        out_shape=jax.ShapeDtypeStruct((B // 2, NC, N1 // SUBLANES, N2 * SUBLANES, LANES), jnp.uint32),
        compiler_params=_cparams("parallel", "parallel", "parallel"),
        name="hyena_stage1",
    )(z, tabs["m_fwd"])


def _mid_kernel(a_ref, g_ref, mf_ref, mi_ref, o_ref):
    cb = a_ref.shape[1]
    N2 = g_ref.shape[2]
    for s in range(SUBLANES):
        re, im = _unpack_complex(_gather_cols(lambda c: _rows_strided(a_ref.at[0, c, 0], s), cb))
        a = jnp.concatenate([re, im], axis=0).astype(BF16)
        x = _dot(mf_ref[...], a)
        xr, xi = x[:N2], x[N2:]
        gr, gi = _unpack_complex(_gather_cols(lambda c: g_ref[c, s], cb))
        y = jnp.concatenate([xr * gr - xi * gi, xr * gi + xi * gr], axis=0).astype(BF16)
        b = _dot(mi_ref[...], y)
        _store_tiles(lambda c: o_ref.at[0, c], _pack_complex(b[:N2], b[N2:]), cb, s)


def _hyena_stage2(a, g, order, tabs, cb=8):
    N1, N2 = tabs["N1"], tabs["N2"]
    P, NC = a.shape[:2]
    cb = min(cb, NC)
    ncb = NC // cb
    return pl.pallas_call(
        _mid_kernel,
        grid=(N1 // SUBLANES, ncb, P),
        in_specs=[
            pl.BlockSpec((1, cb, 1, N2 * SUBLANES, LANES), lambda i, c, p: (p, c, i, 0, 0)),
            pl.BlockSpec((cb, SUBLANES, N2, LANES), lambda i, c, p: (order * ncb + c, i, 0, 0)),
            pl.BlockSpec((2 * N2, 2 * N2), lambda i, c, p: (0, 0)),
            pl.BlockSpec((2 * N2, 2 * N2), lambda i, c, p: (0, 0)),
        ],
        out_specs=pl.BlockSpec((1, cb, N2 // SUBLANES, SUBLANES * SUBLANES, LANES),
                               lambda i, c, p: (p, c, 0, i, 0)),
        out_shape=jax.ShapeDtypeStruct((P, NC, N2 // SUBLANES, N1 * SUBLANES, LANES), jnp.uint32),
        compiler_params=_cparams("parallel", "parallel", "arbitrary"),
        name="hyena_stage2",
    )(a, g, tabs["m2_fwd"], tabs["m2_inv"])


def _inv1_kernel(b_ref, m_ref, gate_ref, z_ref, bias_ref, o_ref, *, z_slabbed, final):
    cb = b_ref.shape[1]
    N1 = m_ref.shape[1]
    H = N1 // 2
    bias = bias_ref[...]
    for s in range(SUBLANES):
        br, bi_ = _unpack_complex(_gather_cols(lambda c: _rows_strided(b_ref.at[0, c, 0], s), cb))
        b = jnp.concatenate([br, bi_], axis=0).astype(BF16)
        conv = _dot(m_ref[s], b)
        for bi in range(2):
            gate = _gather_cols(lambda c: _rows_strided(gate_ref.at[0, bi, c, 0], s), cb)
            if z_slabbed:
                z = _gather_cols(lambda c: _rows_strided(z_ref.at[0, bi, c, 0], s), cb)
            else:
                z = _gather_cols(lambda c: z_ref[bi, c, s], cb)
            out = gate * (conv[bi * H:(bi + 1) * H] + z * bias)
            for c in range(cb):
                piece = out[:, c * LANES:(c + 1) * LANES]
                if final:
                    o_ref.at[bi, c, 0][pl.ds(s, H, stride=SUBLANES), :] = piece
                else:
                    o_ref[bi, c, s] = piece


def _hyena_stage3(bq, ucl, gate_idx, z, z_idx, bias, tabs, final):
    N1, N2 = tabs["N1"], tabs["N2"]
    H = N1 // 2
    P, NC = bq.shape[:2]
    cb = min(NC, 4)
    z_slabbed = z_idx is not None
    slab_blk = (1, 2, cb, 1, H * SUBLANES, LANES)
    n2m_blk = (2, cb, SUBLANES, H, LANES)
    if z_slabbed:
        z_spec = pl.BlockSpec(slab_blk, lambda p, c, j: (z_idx, p, c, j, 0, 0))
    else:
        z_spec = pl.BlockSpec(n2m_blk, lambda p, c, j: (p, c, j, 0, 0))
    if final:
        o_spec = pl.BlockSpec((2, cb, 1, H * SUBLANES, LANES), lambda p, c, j: (p, c, j, 0, 0))
        o_shape = (2 * P, NC, N2 // SUBLANES, H * SUBLANES, LANES)
    else:
        o_spec = pl.BlockSpec(n2m_blk, lambda p, c, j: (p, c, j, 0, 0))
        o_shape = (2 * P, NC, N2, H, LANES)
    return pl.pallas_call(
        functools.partial(_inv1_kernel, z_slabbed=z_slabbed, final=final),
        grid=(P, NC // cb, N2 // SUBLANES),
        in_specs=[
            pl.BlockSpec((1, cb, 1, N1 * SUBLANES, LANES), lambda p, c, j: (p, c, j, 0, 0)),
            pl.BlockSpec((SUBLANES, N1, 2 * N1), lambda p, c, j: (j, 0, 0)),
            pl.BlockSpec(slab_blk, lambda p, c, j: (gate_idx, p, c, j, 0, 0)),
            z_spec,
            pl.BlockSpec((1, cb * LANES), lambda p, c, j: (0, c)),
        ],
        out_specs=o_spec,
        out_shape=jax.ShapeDtypeStruct(o_shape, F32),
        compiler_params=_cparams("parallel", "parallel", "parallel"),
        name="hyena_stage3",
    )(bq, tabs["m_inv"], ucl, z, bias[None, :])


def _hyena_mixer(ucl, g_spec, hy_bias, tabs):
    z, z_idx = ucl, 2
    for order in range(HY_ORDER):
        a = _hyena_stage1(z, z_idx, tabs)
        bq = _hyena_stage2(a, g_spec, order, tabs)
        z = _hyena_stage3(bq, ucl, order, z, z_idx, hy_bias[order], tabs,
                          final=(order == HY_ORDER - 1))
        z_idx = None
    return z


def _attn_kernel(sink_ref, q_ref, kp_ref, kc_ref, kn_ref, vp_ref, vc_ref, vn_ref, nw_ref,
                 o_ref, y_scr, *, L):
    i = pl.program_id(1)
    tq = q_ref.shape[1]
    QB = WINDOW
    span = QB + 2 * WINDOW
    k = jnp.concatenate([kp_ref[0], kc_ref[0], kn_ref[0]], axis=0)
    vt = jnp.concatenate([vp_ref[0], vc_ref[0], vn_ref[0]], axis=1)
    key = lax.broadcasted_iota(jnp.int32, (span, QB), 0)
    qry = lax.broadcasted_iota(jnp.int32, (span, QB), 1)
    rel = key - qry
    band = (rel >= 0) & (rel <= 2 * WINDOW)
    for sub in range(tq // QB):
        r0 = sub * QB
        kpos = i * tq + r0 - WINDOW + key
        ok = band & (kpos >= 0) & (kpos < L)
        bias = jnp.where(ok, 0.0, NEG_BIG).astype(F32)
        for h in range(N_KV_HEADS):
            kh = k[r0:r0 + span, h * HEAD_DIM:(h + 1) * HEAD_DIM]
            vht = vt[h * HEAD_DIM:(h + 1) * HEAD_DIM, r0:r0 + span]
            for g0 in range(0, Q_PER_KV, HEADS_PER_PASS):
                heads = [h * Q_PER_KV + g0 + g for g in range(HEADS_PER_PASS)]
                qh = jnp.concatenate(
                    [q_ref[0, r0:r0 + QB, a * HEAD_DIM:(a + 1) * HEAD_DIM] for a in heads], axis=0)
                s = lax.dot_general(kh, qh, (((1,), (1,)), ((), ())), preferred_element_type=F32)
                s = s + jnp.tile(bias, (1, HEADS_PER_PASS))
                sink = jnp.concatenate([jnp.full((1, QB), sink_ref[a], F32) for a in heads], axis=1)
                m = jnp.maximum(jnp.max(s, axis=0, keepdims=True), sink)
                p = jnp.exp(s - m)
                denom = jnp.sum(p, axis=0, keepdims=True) + jnp.exp(sink - m)
                ot = _dot(vht, p.astype(BF16)) / denom
                for g, a in enumerate(heads):
                    y_scr[a * HEAD_DIM:(a + 1) * HEAD_DIM, r0:r0 + QB] = ot[:, g * QB:(g + 1) * QB]
    yt = y_scr[...]
    inv = lax.rsqrt(jnp.mean(yt * yt, axis=0, keepdims=True) + EPS)
    o_ref[0] = ((yt * inv).T * nw_ref[...]).astype(o_ref.dtype)


def _windowed_attention(q, k, vt, sink, out_norm_w, tq=256):
    B, L, _ = q.shape
    rq = tq // WINDOW
    nwb = L // WINDOW
    cur = lambda b, i, s: (b, i, 0)
    prv = lambda b, i, s: (b, jnp.maximum(i * rq - 1, 0), 0)
    nxt = lambda b, i, s: (b, jnp.minimum((i + 1) * rq, nwb - 1), 0)
    cur_t = lambda b, i, s: (b, 0, i)
    prv_t = lambda b, i, s: (b, 0, jnp.maximum(i * rq - 1, 0))
    nxt_t = lambda b, i, s: (b, 0, jnp.minimum((i + 1) * rq, nwb - 1))
    grid_spec = pltpu.PrefetchScalarGridSpec(
        num_scalar_prefetch=1,
        grid=(B, L // tq),
        in_specs=[
            pl.BlockSpec((1, tq, ATTN_WIDTH), cur),
            pl.BlockSpec((1, WINDOW, KV_WIDTH), prv),
            pl.BlockSpec((1, tq, KV_WIDTH), cur),
            pl.BlockSpec((1, WINDOW, KV_WIDTH), nxt),
            pl.BlockSpec((1, KV_WIDTH, WINDOW), prv_t),
            pl.BlockSpec((1, KV_WIDTH, tq), cur_t),
            pl.BlockSpec((1, KV_WIDTH, WINDOW), nxt_t),
            pl.BlockSpec((1, ATTN_WIDTH), lambda b, i, s: (0, 0)),
        ],
        out_specs=pl.BlockSpec((1, tq, ATTN_WIDTH), cur),
        scratch_shapes=[pltpu.VMEM((ATTN_WIDTH, tq), F32)],
    )
    return pl.pallas_call(
        functools.partial(_attn_kernel, L=L),
        grid_spec=grid_spec,
        out_shape=jax.ShapeDtypeStruct((B, L, ATTN_WIDTH), BF16),
        compiler_params=_cparams("parallel", "parallel"),
        name="banded_attention",
    )(sink.astype(F32), q, k, k, k, vt, vt, vt, out_norm_w[None, :])


def _outproj_kernel(x_ref, a_ref, hy_ref, hw_ref, wa_ref, wh_ref, nw_ref, wr_ref,
                    x1_ref, h2_ref, lg_ref):
    nslab = hy_ref.shape[2]
    hy = jnp.concatenate(
        [_gather_cols(lambda c: hy_ref[0, c, j, n1 * SUBLANES:(n1 + 1) * SUBLANES, :], hy_ref.shape[1])
         for n1 in range(hy_ref.shape[3] // SUBLANES) for j in range(nslab)],
        axis=0)
    hinv = lax.rsqrt(jnp.mean(hy * hy, axis=-1, keepdims=True) + EPS)
    hy_n = (hy * hinv * hw_ref[...]).astype(BF16)
    x1 = x_ref[...] + _dot(a_ref[...], wa_ref[...]) + _dot(hy_n, wh_ref[...])
    x1_ref[...] = x1
    inv = lax.rsqrt(jnp.mean(x1 * x1, axis=-1, keepdims=True) + EPS)
    h2 = x1 * inv * nw_ref[...]
    h2_ref[...] = h2.astype(BF16)
    h_hi, h_lo = _split_bf16(h2)
    lg_ref[...] = _dot(h_hi, wr_ref[0]) + _dot(h_lo, wr_ref[0]) + _dot(h_hi, wr_ref[1])


def _out_proj(x2d, attn_n, hy, hy_norm_w, w_out_a, w_out_h, norm2_w, w_router, tm=256):
    T = x2d.shape[0]
    _, NC, J, LJ, _ = hy.shape
    lt = (LJ * J) // tm
    row = lambda i: (i, 0)
    const = lambda i: (0, 0)
    return pl.pallas_call(
        _outproj_kernel,
        grid=(T // tm,),
        in_specs=[
            pl.BlockSpec((tm, D_MODEL), row),
            pl.BlockSpec((tm, ATTN_WIDTH), row),
            pl.BlockSpec((1, NC, J, tm // J, LANES), lambda i: (i // lt, 0, 0, i % lt, 0)),
            pl.BlockSpec((1, HY_WIDTH), const),
            pl.BlockSpec((ATTN_WIDTH, D_MODEL), const),
            pl.BlockSpec((HY_WIDTH, D_MODEL), const),
            pl.BlockSpec((1, D_MODEL), const),
            pl.BlockSpec((2, D_MODEL, ROUTER_PAD), lambda i: (0, 0, 0)),
        ],
        out_specs=[
            pl.BlockSpec((tm, D_MODEL), row),
            pl.BlockSpec((tm, D_MODEL), row),
            pl.BlockSpec((tm, ROUTER_PAD), row),
        ],
        out_shape=[
            jax.ShapeDtypeStruct((T, D_MODEL), F32),
            jax.ShapeDtypeStruct((T, D_MODEL), BF16),
            jax.ShapeDtypeStruct((T, ROUTER_PAD), F32),
        ],
        compiler_params=_cparams("parallel"),
        name="out_proj",
    )(x2d, attn_n, hy, hy_norm_w[None, :], w_out_a, w_out_h, norm2_w[None, :], w_router)


def _cast_kernel(x_ref, o_ref):
    o_ref[...] = x_ref[...].astype(o_ref.dtype)


def _to_bf16(w, rb=1024):
    E, R, C = w.shape
    spec = pl.BlockSpec((1, rb, C), lambda e, i: (e, i, 0))
    return pl.pallas_call(
        _cast_kernel,
        grid=(E, R // rb),
        in_specs=[spec],
        out_specs=spec,
        out_shape=jax.ShapeDtypeStruct(w.shape, BF16),
        compiler_params=_cparams("parallel", "parallel"),
        name="expert_weights_to_bf16",
    )(w)


def _moe_kernel(be_ref, nu_ref, x_ref, wg_ref, wu_ref, wd_ref, o_ref):
    @pl.when(pl.program_id(0) < nu_ref[0])
    def _():
        x = x_ref[...]
        g = _dot(x, wg_ref[0])
        u = _dot(x, wu_ref[0])
        hid = (g * jax.nn.sigmoid(g)) * u
        o_ref[...] = _dot(hid.astype(BF16), wd_ref[0]).astype(o_ref.dtype)


def _expert_blocks(xs, block_expert, n_used, w_gate, w_up, w_down):
    P = xs.shape[0]
    bm = MOE_BLOCK_ROWS
    blk = lambda b, be, nu: (jnp.minimum(b, nu[0] - 1), 0)
    wsel = lambda b, be, nu: (be[jnp.minimum(b, nu[0] - 1)], 0, 0)
    grid_spec = pltpu.PrefetchScalarGridSpec(
        num_scalar_prefetch=2,
        grid=(P // bm,),
        in_specs=[
            pl.BlockSpec((bm, D_MODEL), blk),
            pl.BlockSpec((1, D_MODEL, D_EXPERT), wsel),
            pl.BlockSpec((1, D_MODEL, D_EXPERT), wsel),
            pl.BlockSpec((1, D_EXPERT, D_MODEL), wsel),
        ],
        out_specs=pl.BlockSpec((bm, D_MODEL), blk),
    )
    return pl.pallas_call(
        _moe_kernel,
        grid_spec=grid_spec,
        out_shape=jax.ShapeDtypeStruct((P, D_MODEL), BF16),
        compiler_params=_cparams("arbitrary"),
        name="moe_experts",
    )(block_expert, n_used, xs, w_gate, w_up, w_down)


def _hier_moe(x1, h2, logits, b_rg, b_re, w_gate, w_up, w_down):
    T = h2.shape[0]
    g_logits = logits[:, :N_EXPERT_GROUPS] + b_rg
    e_logits = (logits[:, N_EXPERT_GROUPS:N_EXPERT_GROUPS + N_EXPERTS] + b_re)
    e_logits = e_logits.reshape(T, N_EXPERT_GROUPS, EXPERTS_PER_GROUP)
    g_idx = jnp.argmax(g_logits, axis=-1)
    g_gate = jnp.take_along_axis(jax.nn.softmax(g_logits, axis=-1), g_idx[:, None], axis=1)
    e_sel = jnp.take_along_axis(e_logits, g_idx[:, None, None], axis=1)[:, 0]
    top_val, top_loc = lax.top_k(e_sel, TOP_K)
    weights = g_gate * jax.nn.softmax(top_val, axis=-1)
    experts = (g_idx[:, None] * EXPERTS_PER_GROUP + top_loc).astype(jnp.int32)

    A = T * TOP_K
    bm = MOE_BLOCK_ROWS
    flat_e = experts.reshape(-1)
    onehot = (flat_e[:, None] == jnp.arange(N_EXPERTS, dtype=jnp.int32)[None, :]).astype(jnp.int32)
    csum = jnp.cumsum(onehot, axis=0)
    counts = csum[-1]
    rank = jnp.sum((csum - onehot) * onehot, axis=1)
    padded = ((counts + bm - 1) // bm) * bm
    pend = jnp.cumsum(padded)
    pstart = pend - padded
    seg_start = jnp.cumsum(counts) - counts
    dest = (pstart[flat_e] + rank).astype(jnp.int32)
    n_blocks = (A + bm - 1) // bm + N_EXPERTS
    P = n_blocks * bm
    order = jnp.argsort(flat_e, stable=True).astype(jnp.int32)
    q = jnp.arange(P, dtype=jnp.int32)
    e_q = jnp.minimum(jnp.sum((pend[None, :] <= q[:, None]).astype(jnp.int32), axis=1), N_EXPERTS - 1)
    r_q = jnp.minimum(q - pstart[e_q], jnp.maximum(counts[e_q] - 1, 0))
    src = jnp.clip(seg_start[e_q] + r_q, 0, A - 1)
    buf_tok = order.at[src].get(mode="promise_in_bounds") // TOP_K
    block_expert = e_q.reshape(n_blocks, bm)[:, 0]
    n_used = (pend[-1] // bm).astype(jnp.int32)[None]

    xs = h2.at[buf_tok].get(mode="promise_in_bounds")
    yb = _expert_blocks(xs, block_expert, n_used, w_gate, w_up, w_down)
    d = dest.reshape(T, TOP_K)
    y0 = yb.at[d[:, 0]].get(mode="promise_in_bounds").astype(F32)
    y1 = yb.at[d[:, 1]].get(mode="promise_in_bounds").astype(F32)
    return x1 + (weights[:, 0:1] * y0 + weights[:, 1:2] * y1)


def _encoder_layer(x, p):
    B, L, D = x.shape
    T = B * L
    x2d = x.reshape(T, D)
    q, k, v = _qkv_proj(x2d, L, p["norm1_w"], p["w_qkv"], p["q_norm_w"], p["k_norm_w"])
    attn_n = _windowed_attention(q.reshape(B, L, ATTN_WIDTH), k.reshape(B, L, KV_WIDTH),
                                 v, p["attn_sink"], p["attn_out_norm_w"])
    u = _u_proj(x2d, p["norm1_w"], p["w_u"])
    uc = _short_conv(u.reshape(B, L, 3 * HY_WIDTH), p["conv_w"], p["conv_b"])
    tabs = _fft_tables(L)
    g_spec = _hyena_filter_spectrum(L, tabs, p["filt_w1"], p["filt_b1"], p["filt_w2"], p["filt_b2"],
                                    p["filt_w3"], p["filt_b3"], p["filt_w4"], p["filt_freq"])
    hy = _hyena_mixer(uc, g_spec, p["hy_bias"], tabs)
    x1, h2, logits = _out_proj(x2d, attn_n.reshape(T, ATTN_WIDTH), hy, p["hy_out_norm_w"],
                               p["w_out_a"], p["w_out_h"], p["norm2_w"], p["w_router"])
    out = _hier_moe(x1, h2, logits, p["b_route_group"], p["b_route_expert"],
                    p["w_gate"], p["w_up"], p["w_down"])
    return out.reshape(B, L, D)


def kernel(x_prompt, x_sample, norm1_w, w_in, q_norm_w, k_norm_w, attn_sink, conv_w, conv_b, filt_w1, filt_b1, filt_w2, filt_b2, filt_w3, filt_b3, filt_w4, filt_freq, hy_bias, attn_out_norm_w, hy_out_norm_w, w_out, norm2_w, w_route_group, b_route_group, w_route_expert, b_route_expert, w_gate, w_up, w_down):
    depth = norm1_w.shape[0]

    def layer_params(l):
        w_r = jnp.zeros((D_MODEL, ROUTER_PAD), F32)
        w_r = w_r.at[:, :N_EXPERT_GROUPS].set(w_route_group[l])
        w_r = w_r.at[:, N_EXPERT_GROUPS:N_EXPERT_GROUPS + N_EXPERTS].set(w_route_expert[l])
        r_hi, r_lo = _split_bf16(w_r)
        return dict(
            norm1_w=norm1_w[l], w_qkv=w_in[l][:, :QKV_WIDTH].astype(BF16),
            w_u=w_in[l][:, QKV_WIDTH:].astype(BF16), q_norm_w=q_norm_w[l], k_norm_w=k_norm_w[l],
            attn_sink=attn_sink[l], conv_w=conv_w[l], conv_b=conv_b[l],
            filt_w1=filt_w1[l], filt_b1=filt_b1[l], filt_w2=filt_w2[l], filt_b2=filt_b2[l],
            filt_w3=filt_w3[l], filt_b3=filt_b3[l], filt_w4=filt_w4[l], filt_freq=filt_freq[l],
            hy_bias=hy_bias[l], attn_out_norm_w=attn_out_norm_w[l], hy_out_norm_w=hy_out_norm_w[l],
            w_out_a=w_out[l][:ATTN_WIDTH].astype(BF16), w_out_h=w_out[l][ATTN_WIDTH:].astype(BF16),
            norm2_w=norm2_w[l], w_router=jnp.stack([r_hi, r_lo]),
            b_route_group=b_route_group[l], b_route_expert=b_route_expert[l],
            w_gate=_to_bf16(w_gate[l]), w_up=_to_bf16(w_up[l]), w_down=_to_bf16(w_down[l]))

    params = [layer_params(l) for l in range(depth)]

    def trunk(x):
        for p in params:
            x = _encoder_layer(x, p)
        return x

    return (trunk(x_prompt), trunk(x_sample))
```

```python
import functools
import math

import jax
import jax.numpy as jnp
from jax import lax
from jax.experimental import pallas as pl
from jax.experimental.pallas import tpu as pltpu

F32 = jnp.float32
BF16 = jnp.bfloat16

D_MODEL = 2048
HEAD_DIM = 64
N_Q_HEADS = 16
N_KV_HEADS = 4
Q_PER_KV = N_Q_HEADS // N_KV_HEADS
ATTN_WIDTH = N_Q_HEADS * HEAD_DIM
KV_WIDTH = N_KV_HEADS * HEAD_DIM
QK_WIDTH = ATTN_WIDTH + KV_WIDTH
QKV_WIDTH = ATTN_WIDTH + 2 * KV_WIDTH
WINDOW = 128
ROT_DIM = HEAD_DIM // 4
ROPE_THETA = 500000.0
HY_WIDTH = D_MODEL - ATTN_WIDTH
HY_ORDER = 2
HY_EMB_DIM = 33
HY_FILTER_HIDDEN = 64
HY_DECAY_TARGET = 1e-2
HY_FAST_DECAY_PCT = 0.3
HY_SLOW_DECAY_PCT = 1.5
N_EXPERT_GROUPS = 4
EXPERTS_PER_GROUP = 8
N_EXPERTS = N_EXPERT_GROUPS * EXPERTS_PER_GROUP
TOP_K = 2
D_EXPERT = 1024
EPS = 1e-6

LANES = 128
SUBLANES = 8
FFT_N2 = 128
MOE_BLOCK_ROWS = 256
ROUTER_PAD = 128
VMEM_LIMIT = 56 * 1024 * 1024
NEG_BIG = -1e30
HEADS_PER_PASS = Q_PER_KV


def _cparams(*sem):
    return pltpu.CompilerParams(dimension_semantics=sem, vmem_limit_bytes=VMEM_LIMIT)


def _split_bf16(a):
    hi = a.astype(BF16)
    lo = (a - hi.astype(F32)).astype(BF16)
    return hi, lo


def _dot(a, b):
    return jnp.dot(a, b, preferred_element_type=F32)


def _dot3(a, b):
    a_hi, a_lo = _split_bf16(a)
    b_hi, b_lo = _split_bf16(b)
    return _dot(a_hi, b_hi) + _dot(a_lo, b_hi) + _dot(a_hi, b_lo)


def _qkv_kernel(x_ref, nw_ref, w_ref, seg_ref, hw_ref, rc_ref, rs1_ref, rs2_ref,
                q_ref, k_ref, v_ref):
    x = x_ref[...]
    inv = lax.rsqrt(jnp.mean(x * x, axis=-1, keepdims=True) + EPS)
    h = (x * inv * nw_ref[...]).astype(BF16)
    acc = _dot(h, w_ref[...])
    qk = acc[:, :QK_WIDTH]
    hi, lo = _split_bf16(qk * qk)
    seg = seg_ref[...]
    w = seg.shape[0]
    ms = jnp.concatenate(
        [_dot(hi[:, c * w:(c + 1) * w], seg) + _dot(lo[:, c * w:(c + 1) * w], seg)
         for c in range(QK_WIDTH // w)], axis=-1)
    xn = qk * lax.rsqrt(ms + EPS) * hw_ref[...]
    reps = QK_WIDTH // LANES
    rc = jnp.tile(rc_ref[...], (1, reps))
    rs1 = jnp.tile(rs1_ref[...], (1, reps))
    rs2 = jnp.tile(rs2_ref[...], (1, reps))
    half = ROT_DIM // 2
    y = xn * rc + pltpu.roll(xn, QK_WIDTH - half, 1) * rs1 + pltpu.roll(xn, half, 1) * rs2
    q_ref[...] = (y[:, :ATTN_WIDTH] * (HEAD_DIM ** -0.5)).astype(BF16)
    k_ref[...] = y[:, ATTN_WIDTH:].astype(BF16)
    v_ref[0] = acc[:, QK_WIDTH:].T.astype(BF16)


def _rope_tables(L):
    half = ROT_DIM // 2
    inv_freq = jnp.power(ROPE_THETA, -jnp.arange(half, dtype=F32) * 2.0 / ROT_DIM)
    ang = jnp.arange(L, dtype=F32)[:, None] * inv_freq[None, :]
    d = jnp.arange(LANES) % HEAD_DIM
    ang_l = ang[:, d % half]
    cos, sin = jnp.cos(ang_l), jnp.sin(ang_l)
    rc = jnp.where(d < ROT_DIM, cos, 1.0)
    rs1 = jnp.where(d < half, -sin, 0.0)
    rs2 = jnp.where((d >= half) & (d < ROT_DIM), sin, 0.0)
    return rc.astype(F32), rs1.astype(F32), rs2.astype(F32)


def _qkv_proj(x2d, L, norm1_w, w_qkv, q_norm_w, k_norm_w, tm=512):
    T = x2d.shape[0]
    seg_w = 256
    seg = (jnp.arange(seg_w)[:, None] // HEAD_DIM == jnp.arange(seg_w)[None, :] // HEAD_DIM)
    seg = (seg.astype(F32) / HEAD_DIM).astype(BF16)
    hw = jnp.concatenate([jnp.tile(q_norm_w, N_Q_HEADS), jnp.tile(k_norm_w, N_KV_HEADS)])[None, :]
    rc, rs1, rs2 = _rope_tables(L)
    lt = L // tm
    row = lambda i: (i, 0)
    const = lambda i: (0, 0)
    pos = lambda i: (i % lt, 0)
    return pl.pallas_call(
        _qkv_kernel,
        grid=(T // tm,),
        in_specs=[
            pl.BlockSpec((tm, D_MODEL), row),
            pl.BlockSpec((1, D_MODEL), const),
            pl.BlockSpec((D_MODEL, QKV_WIDTH), const),
            pl.BlockSpec((seg_w, seg_w), const),
            pl.BlockSpec((1, QK_WIDTH), const),
            pl.BlockSpec((tm, LANES), pos),
            pl.BlockSpec((tm, LANES), pos),
            pl.BlockSpec((tm, LANES), pos),
        ],
        out_specs=[
            pl.BlockSpec((tm, ATTN_WIDTH), row),
            pl.BlockSpec((tm, KV_WIDTH), row),
            pl.BlockSpec((1, KV_WIDTH, tm), lambda i: (i // lt, 0, i % lt)),
        ],
        out_shape=[
            jax.ShapeDtypeStruct((T, ATTN_WIDTH), BF16),
            jax.ShapeDtypeStruct((T, KV_WIDTH), BF16),
            jax.ShapeDtypeStruct((T // L, KV_WIDTH, L), BF16),
        ],
        compiler_params=_cparams("parallel"),
        name="qkv_proj",
    )(x2d, norm1_w[None, :], w_qkv, seg, hw, rc, rs1, rs2)


HALO = 16


def _uproj_kernel(x_ref, xp_ref, xn_ref, nw_ref, w_ref, cw_ref, cb_ref, o_ref, h_scr, *, lt):
    i = pl.program_id(0)
    tm = x_ref.shape[0]

    @pl.when(pl.program_id(1) == 0)
    def _():
        def normed(x):
            inv = lax.rsqrt(jnp.mean(x * x, axis=-1, keepdims=True) + EPS)
            return x * inv * nw_ref[...]

        li = i % lt
        hp = jnp.where(li > 0, normed(xp_ref[...]), 0.0)
        hn = jnp.where(li < lt - 1, normed(xn_ref[...]), 0.0)
        h_scr[0:HALO] = hp.astype(BF16)
        h_scr[HALO:HALO + tm] = normed(x_ref[...]).astype(BF16)
        h_scr[HALO + tm:2 * HALO + tm] = hn.astype(BF16)

    u = _dot(h_scr[...], w_ref[...])
    w = cw_ref[...]
    val = (w[0:1] * u[HALO - 1:HALO - 1 + tm] + w[1:2] * u[HALO:HALO + tm]
           + w[2:3] * u[HALO + 1:HALO + 1 + tm] + cb_ref[...])
    for n1 in range(tm // FFT_N2):
        for j in range(FFT_N2 // SUBLANES):
            r0 = n1 * FFT_N2 + j * SUBLANES
            for c in range(val.shape[1] // LANES):
                o_ref[0, 0, c, j, n1 * SUBLANES:(n1 + 1) * SUBLANES, :] = (
                    val[r0:r0 + SUBLANES, c * LANES:(c + 1) * LANES])


def _u_proj_conv(x2d, L, norm1_w, w_u, conv_w, conv_b, tm=1024):
    T = x2d.shape[0]
    C = HY_WIDTH
    nparts = w_u.shape[1] // C
    lt = L // tm
    hb = tm // HALO
    J = FFT_N2 // SUBLANES
    return pl.pallas_call(
        functools.partial(_uproj_kernel, lt=lt),
        grid=(T // tm, nparts),
        in_specs=[
            pl.BlockSpec((tm, D_MODEL), lambda i, j: (i, 0)),
            pl.BlockSpec((HALO, D_MODEL), lambda i, j: (jnp.maximum(i * hb - 1, 0), 0)),
            pl.BlockSpec((HALO, D_MODEL), lambda i, j: (jnp.minimum((i + 1) * hb, T // HALO - 1), 0)),
            pl.BlockSpec((1, D_MODEL), lambda i, j: (0, 0)),
            pl.BlockSpec((D_MODEL, C), lambda i, j: (0, j)),
            pl.BlockSpec((3, C), lambda i, j: (0, j)),
            pl.BlockSpec((1, C), lambda i, j: (0, j)),
        ],
        out_specs=pl.BlockSpec((1, 1, C // LANES, J, tm // J, LANES),
                               lambda i, j: (j, i // lt, 0, 0, i % lt, 0)),
        out_shape=jax.ShapeDtypeStruct((nparts, T // L, C // LANES, J, L // J, LANES), F32),
        scratch_shapes=[pltpu.VMEM((tm + 2 * HALO, D_MODEL), BF16)],
        compiler_params=_cparams("parallel", "arbitrary"),
        name="u_proj_conv",
    )(x2d, x2d, x2d, norm1_w[None, :], w_u, conv_w, conv_b[None, :])


def _fft_tables(L):
    N = 2 * L
    N2 = FFT_N2
    N1 = N // N2
    H = N1 // 2
    k1 = jnp.arange(N1, dtype=jnp.int32)
    n2 = jnp.arange(N2, dtype=jnp.int32)
    a1 = ((k1[:, None] * k1[None, :]) % N1).astype(F32) * (2.0 * math.pi / N1)
    c1, s1 = jnp.cos(a1), jnp.sin(a1)
    at = (n2[:, None] * k1[None, :]).astype(F32) * (2.0 * math.pi / N)
    ct, st = jnp.cos(at), jnp.sin(at)
    c = c1[None] * ct[:, :, None] - s1[None] * st[:, :, None]
    s = s1[None] * ct[:, :, None] + c1[None] * st[:, :, None]
    m_filt = jnp.concatenate([c, -s], axis=1).astype(BF16)
    ch, sh = c[:, :, :H], s[:, :, :H]
    m_fwd = jnp.concatenate([jnp.concatenate([ch, sh], axis=2),
                             jnp.concatenate([-sh, ch], axis=2)], axis=1)
    cT = c1[None, :H] * ct[:, None, :] - s1[None, :H] * st[:, None, :]
    sT = s1[None, :H] * ct[:, None, :] + c1[None, :H] * st[:, None, :]
    m_inv = jnp.concatenate([jnp.concatenate([cT, -sT], axis=2),
                             jnp.concatenate([sT, cT], axis=2)], axis=1)
    k2 = jnp.arange(N2, dtype=jnp.int32)
    ph = ((k2[:, None] * n2[None, :]) % N2).astype(F32) * (2.0 * math.pi / N2)
    c2, s2 = jnp.cos(ph), jnp.sin(ph)
    m2_fwd = jnp.concatenate([jnp.concatenate([c2, s2], axis=1),
                              jnp.concatenate([-s2, c2], axis=1)], axis=0)
    m2_inv = m2_fwd.T
    return dict(N1=N1, N2=N2, m_filt=m_filt, m_fwd=m_fwd.astype(BF16), m_inv=m_inv.astype(BF16),
                m2_fwd=m2_fwd.astype(BF16), m2_inv=m2_inv.astype(BF16))


def _filt1_kernel(w1_ref, b1_ref, w2_ref, b2_ref, w3_ref, b3_ref, w4_ref, a_ref, frl_ref, off_ref,
                  dl_ref, m_ref, o_ref, sabs_ref, *, L, N1, N2, nb):
    j = pl.program_id(0)
    C2 = HY_ORDER * HY_WIDTH
    H = N1 // 2
    rows = nb * N1
    ridx = lax.broadcasted_iota(jnp.int32, (rows, LANES), 0)
    lane = lax.broadcasted_iota(jnp.int32, (rows, LANES), 1)
    n2 = j * nb + ridx // N1
    n1 = ridx % N1
    r = n1 * N2 + n2
    lag = jnp.where(r < L, r, 2 * L - r)
    valid = (r != L).astype(F32)
    lagf = jnp.minimum(lag, L - 1).astype(F32)
    t = lagf / (L - 1)
    wpos = lagf * (2.0 * math.pi / L)
    phase = wpos * frl_ref[...] + off_ref[...]
    zemb = jnp.where(lane == 0, t, jnp.where(lane < HY_EMB_DIM, jnp.cos(phase), 0.0))
    half = rows // 2
    a = a_ref[...]
    h = jnp.concatenate([zemb[:half], zemb[half:]], axis=1)
    h = jnp.sin(a * (_dot3(h, w1_ref[...]) + b1_ref[...]))
    h = jnp.sin(a * (_dot3(h, w2_ref[...]) + b2_ref[...]))
    h = jnp.sin(a * (_dot3(h, w3_ref[...]) + b3_ref[...]))
    h_hi, h_lo = _split_bf16(h)
    decay = jnp.exp(-jnp.tile(t, (1, HY_WIDTH // LANES)) * dl_ref[...])
    decay = decay * jnp.tile(valid, (1, HY_WIDTH // LANES))
    decay2 = jnp.tile(decay, (1, HY_ORDER))

    @pl.when(j == 0)
    def _():
        sabs_ref[...] = jnp.zeros_like(sabs_ref)

    tot = jnp.zeros((8, C2), F32)
    for l in range(nb):
        side = (l * N1) // half
        r0 = l * N1 - side * half
        fs = slice(r0, r0 + H)
        bs = slice(r0 + H, r0 + N1)
        wf = w4_ref[side, :, :C2]
        wb = w4_ref[side, :, C2:]
        gf = _dot(h_hi[fs], wf) + _dot(h_lo[fs], wf)
        gb = _dot(h_hi[bs], wb) + _dot(h_lo[bs], wb)
        g = jnp.concatenate([gf, gb], axis=0) * decay2[l * N1:(l + 1) * N1]
        tot = tot + jnp.sum(jnp.abs(g).reshape(N1 // 8, 8, C2), axis=0)
        y = _dot(m_ref[l], g.astype(BF16))
        _store_tiles(lambda c: o_ref.at[c], _pack_complex(y[:N1], y[N1:]), C2 // LANES, l)
    sabs_ref[...] += tot


def _pack_complex(re, im):
    r = lax.bitcast_convert_type(re.astype(BF16).astype(F32), jnp.uint32)
    i = lax.bitcast_convert_type(im.astype(BF16).astype(F32), jnp.uint32)
    return r | (i >> 16)


def _unpack_complex(w):
    re = lax.bitcast_convert_type(w & jnp.uint32(0xFFFF0000), F32)
    im = lax.bitcast_convert_type(w << 16, F32)
    return re, im


def _store_tiles(dst, w, ncols, s):
    for c in range(ncols):
        d = dst(c)
        for t in range(w.shape[0] // SUBLANES):
            d[t, s * SUBLANES:(s + 1) * SUBLANES, :] = (
                w[t * SUBLANES:(t + 1) * SUBLANES, c * LANES:(c + 1) * LANES])


def _rows_strided(ref2d, s):
    return ref2d[pl.ds(s, ref2d.shape[0] // SUBLANES, stride=SUBLANES), :]


def _gather_cols(fn, ncols):
    return jnp.concatenate([fn(c) for c in range(ncols)], axis=1)


def _block_diag2(w):
    z = jnp.zeros_like(w)
    return jnp.concatenate([jnp.concatenate([w, z], axis=1), jnp.concatenate([z, w], axis=1)], axis=0)


def _hyena_filter_spectrum(L, tabs, filt_w1, filt_b1, filt_w2, filt_b2, filt_w3, filt_b3,
                           filt_w4, filt_freq):
    N1, N2 = tabs["N1"], tabs["N2"]
    N = N1 * N2
    C2 = HY_ORDER * HY_WIDTH
    FH = HY_FILTER_HIDDEN
    nb = max(2, 512 // N1)
    bands = (HY_EMB_DIM - 1) // 2
    fr = jnp.linspace(1e-4, bands - 1, bands, dtype=F32)
    frl = jnp.zeros((LANES,), F32).at[1:1 + bands].set(fr).at[1 + bands:1 + 2 * bands].set(fr)[None, :]
    off = jnp.zeros((LANES,), F32).at[1 + bands:1 + 2 * bands].set(0.5 * math.pi)[None, :]
    w1p = jnp.zeros((LANES, FH), F32).at[:HY_EMB_DIM].set(filt_w1)
    w4 = filt_w4.astype(BF16)
    w4z = jnp.zeros_like(w4)
    w4s = jnp.stack([jnp.concatenate([w4, w4z], axis=0), jnp.concatenate([w4z, w4], axis=0)])
    two = lambda v: jnp.tile(v, 2)[None, :]
    max_decay = math.log(HY_DECAY_TARGET) / HY_FAST_DECAY_PCT
    min_decay = math.log(HY_DECAY_TARGET) / HY_SLOW_DECAY_PCT
    deltas = jnp.abs(jnp.linspace(min_decay, max_decay, HY_WIDTH, dtype=F32))[None, :]
    full = lambda shape: pl.BlockSpec(shape, lambda j: (0,) * len(shape))
    a_g, sabs = pl.pallas_call(
        functools.partial(_filt1_kernel, L=L, N1=N1, N2=N2, nb=nb),
        grid=(N2 // nb,),
        in_specs=[
            full((2 * LANES, 2 * FH)), full((1, 2 * FH)),
            full((2 * FH, 2 * FH)), full((1, 2 * FH)),
            full((2 * FH, 2 * FH)), full((1, 2 * FH)),
            full((2, 2 * FH, 2 * C2)), full((1, 2 * FH)),
            full((1, LANES)), full((1, LANES)), full((1, HY_WIDTH)),
            pl.BlockSpec((nb, 2 * N1, N1), lambda j: (j, 0, 0)),
        ],
        out_specs=[
            pl.BlockSpec((C2 // LANES, N1 // SUBLANES, nb * SUBLANES, LANES), lambda j: (0, 0, j, 0)),
            pl.BlockSpec((8, C2), lambda j: (0, 0)),
        ],
        out_shape=[
            jax.ShapeDtypeStruct((C2 // LANES, N1 // SUBLANES, N2 * SUBLANES, LANES), jnp.uint32),
            jax.ShapeDtypeStruct((8, C2), F32),
        ],
        compiler_params=_cparams("arbitrary"),
        name="hyena_filter_stage1",
    )(_block_diag2(w1p), two(filt_b1), _block_diag2(filt_w2), two(filt_b2), _block_diag2(filt_w3),
      two(filt_b3), w4s, two(filt_freq), frl, off, deltas, tabs["m_filt"])
    scale = (1.0 / (jnp.sum(sabs, axis=0) * N))[None, :]
    return a_g, scale


def _fwd1_kernel(x_ref, m_ref, o_ref, *, slabbed):
    cb = o_ref.shape[1]
    N1 = m_ref.shape[2]
    for s in range(SUBLANES):
        if slabbed:
            rows = [_gather_cols(lambda c: _rows_strided(x_ref.at[0, bi, c, 0], s), cb) for bi in range(2)]
        else:
            rows = [_gather_cols(lambda c: x_ref[bi, c, s], cb) for bi in range(2)]
        xl = jnp.concatenate(rows, axis=0).astype(BF16)
        y = _dot(m_ref[s], xl)
        _store_tiles(lambda c: o_ref.at[0, c], _pack_complex(y[:N1], y[N1:]), cb, s)


def _hyena_stage1(z, which, tabs):
    N1, N2 = tabs["N1"], tabs["N2"]
    H = N1 // 2
    slabbed = which is not None
    B, NC = (z.shape[1], z.shape[2]) if slabbed else (z.shape[0], z.shape[1])
    cb = min(NC, 4)
    if slabbed:
        x_spec = pl.BlockSpec((1, 2, cb, 1, H * SUBLANES, LANES), lambda p, c, j: (which, p, c, j, 0, 0))
    else:
        x_spec = pl.BlockSpec((2, cb, SUBLANES, H, LANES), lambda p, c, j: (p, c, j, 0, 0))
    return pl.pallas_call(
        functools.partial(_fwd1_kernel, slabbed=slabbed),
        grid=(B // 2, NC // cb, N2 // SUBLANES),
        in_specs=[x_spec, pl.BlockSpec((SUBLANES, 2 * N1, N1), lambda p, c, j: (j, 0, 0))],
        out_specs=pl.BlockSpec((1, cb, N1 // SUBLANES, SUBLANES * SUBLANES, LANES),
                               lambda p, c, j: (p, c, 0, j, 0)),
        out_shape=jax.ShapeDtypeStruct((B // 2, NC, N1 // SUBLANES, N2 * SUBLANES, LANES), jnp.uint32),
        compiler_params=_cparams("parallel", "parallel", "parallel"),
        name="hyena_stage1",
    )(z, tabs["m_fwd"])


def _mid_kernel(a_ref, ag_ref, sc_ref, mf_ref, mi_ref, o_ref, g_scr):
    cb = a_ref.shape[1]
    N2 = a_ref.shape[3] // SUBLANES

    @pl.when(pl.program_id(2) == 0)
    def _():
        for s in range(SUBLANES):
            re, im = _unpack_complex(_gather_cols(lambda c: _rows_strided(ag_ref.at[c, 0], s), cb))
            ag = jnp.concatenate([re, im], axis=0).astype(BF16)
            g_scr[s] = _dot(mf_ref[...], ag) * sc_ref[...]

    for s in range(SUBLANES):
        re, im = _unpack_complex(_gather_cols(lambda c: _rows_strided(a_ref.at[0, c, 0], s), cb))
        a = jnp.concatenate([re, im], axis=0).astype(BF16)
        x = _dot(mf_ref[...], a)
        xr, xi = x[:N2], x[N2:]
        gr, gi = g_scr[s, :N2], g_scr[s, N2:]
        y = jnp.concatenate([xr * gr - xi * gi, xr * gi + xi * gr], axis=0).astype(BF16)
        b = _dot(mi_ref[...], y)
        _store_tiles(lambda c: o_ref.at[0, c], _pack_complex(b[:N2], b[N2:]), cb, s)


def _hyena_stage2(a, a_g, scale, order, tabs, cb=8):
    N1, N2 = tabs["N1"], tabs["N2"]
    P, NC = a.shape[:2]
    cb = min(cb, NC)
    ncb = NC // cb
    return pl.pallas_call(
        _mid_kernel,
        grid=(N1 // SUBLANES, ncb, P),
        in_specs=[
            pl.BlockSpec((1, cb, 1, N2 * SUBLANES, LANES), lambda i, c, p: (p, c, i, 0, 0)),
            pl.BlockSpec((cb, 1, N2 * SUBLANES, LANES), lambda i, c, p: (order * ncb + c, i, 0, 0)),
            pl.BlockSpec((1, cb * LANES), lambda i, c, p: (0, order * ncb + c)),
            pl.BlockSpec((2 * N2, 2 * N2), lambda i, c, p: (0, 0)),
            pl.BlockSpec((2 * N2, 2 * N2), lambda i, c, p: (0, 0)),
        ],
        out_specs=pl.BlockSpec((1, cb, N2 // SUBLANES, SUBLANES * SUBLANES, LANES),
                               lambda i, c, p: (p, c, 0, i, 0)),
        out_shape=jax.ShapeDtypeStruct((P, NC, N2 // SUBLANES, N1 * SUBLANES, LANES), jnp.uint32),
        scratch_shapes=[pltpu.VMEM((SUBLANES, 2 * N2, cb * LANES), F32)],
        compiler_params=_cparams("parallel", "parallel", "arbitrary"),
        name="hyena_stage2",
    )(a, a_g, scale, tabs["m2_fwd"], tabs["m2_inv"])


def _inv1_kernel(b_ref, m_ref, gate_ref, z_ref, bias_ref, o_ref, *, z_slabbed, final):
    cb = b_ref.shape[1]
    N1 = m_ref.shape[1]
    H = N1 // 2
    bias = bias_ref[...]
    for s in range(SUBLANES):
        br, bi_ = _unpack_complex(_gather_cols(lambda c: _rows_strided(b_ref.at[0, c, 0], s), cb))
        b = jnp.concatenate([br, bi_], axis=0).astype(BF16)
        conv = _dot(m_ref[s], b)
        for bi in range(2):
            gate = _gather_cols(lambda c: _rows_strided(gate_ref.at[0, bi, c, 0], s), cb)
            if z_slabbed:
                z = _gather_cols(lambda c: _rows_strided(z_ref.at[0, bi, c, 0], s), cb)
            else:
                z = _gather_cols(lambda c: z_ref[bi, c, s], cb)
            out = gate * (conv[bi * H:(bi + 1) * H] + z * bias)
            for c in range(cb):
                piece = out[:, c * LANES:(c + 1) * LANES]
                if final:
                    o_ref.at[bi, c, 0][pl.ds(s, H, stride=SUBLANES), :] = piece
                else:
                    o_ref[bi, c, s] = piece


def _hyena_stage3(bq, ucl, gate_idx, z, z_idx, bias, tabs, final):
    N1, N2 = tabs["N1"], tabs["N2"]
    H = N1 // 2
    P, NC = bq.shape[:2]
    cb = min(NC, 4)
    z_slabbed = z_idx is not None
    slab_blk = (1, 2, cb, 1, H * SUBLANES, LANES)
    n2m_blk = (2, cb, SUBLANES, H, LANES)
    if z_slabbed:
        z_spec = pl.BlockSpec(slab_blk, lambda p, c, j: (z_idx, p, c, j, 0, 0))
    else:
        z_spec = pl.BlockSpec(n2m_blk, lambda p, c, j: (p, c, j, 0, 0))
    if final:
        o_spec = pl.BlockSpec((2, cb, 1, H * SUBLANES, LANES), lambda p, c, j: (p, c, j, 0, 0))
        o_shape = (2 * P, NC, N2 // SUBLANES, H * SUBLANES, LANES)
    else:
        o_spec = pl.BlockSpec(n2m_blk, lambda p, c, j: (p, c, j, 0, 0))
        o_shape = (2 * P, NC, N2, H, LANES)
    return pl.pallas_call(
        functools.partial(_inv1_kernel, z_slabbed=z_slabbed, final=final),
        grid=(P, NC // cb, N2 // SUBLANES),
        in_specs=[
            pl.BlockSpec((1, cb, 1, N1 * SUBLANES, LANES), lambda p, c, j: (p, c, j, 0, 0)),
            pl.BlockSpec((SUBLANES, N1, 2 * N1), lambda p, c, j: (j, 0, 0)),
            pl.BlockSpec(slab_blk, lambda p, c, j: (gate_idx, p, c, j, 0, 0)),
            z_spec,
            pl.BlockSpec((1, cb * LANES), lambda p, c, j: (0, c)),
        ],
        out_specs=o_spec,
        out_shape=jax.ShapeDtypeStruct(o_shape, F32),
        compiler_params=_cparams("parallel", "parallel", "parallel"),
        name="hyena_stage3",
    )(bq, tabs["m_inv"], ucl, z, bias[None, :])


def _hyena_mixer(ucl, g_spec, hy_bias, tabs):
    a_g, scale = g_spec
    z, z_idx = ucl, 2
    for order in range(HY_ORDER):
        a = _hyena_stage1(z, z_idx, tabs)
        bq = _hyena_stage2(a, a_g, scale, order, tabs)
        z = _hyena_stage3(bq, ucl, order, z, z_idx, hy_bias[order], tabs,
                          final=(order == HY_ORDER - 1))
        z_idx = None
    return z


def _attn_kernel(sink_ref, q_ref, kp_ref, kc_ref, kn_ref, vp_ref, vc_ref, vn_ref, nw_ref,
                 o_ref, y_scr, *, L):
    i = pl.program_id(1)
    tq = q_ref.shape[1]
    QB = WINDOW
    span = QB + 2 * WINDOW
    k = jnp.concatenate([kp_ref[0], kc_ref[0], kn_ref[0]], axis=0)
    vt = jnp.concatenate([vp_ref[0], vc_ref[0], vn_ref[0]], axis=1)
    key = lax.broadcasted_iota(jnp.int32, (span, QB), 0)
    qry = lax.broadcasted_iota(jnp.int32, (span, QB), 1)
    rel = key - qry
    band = (rel >= 0) & (rel <= 2 * WINDOW)
    for sub in range(tq // QB):
        r0 = sub * QB
        kpos = i * tq + r0 - WINDOW + key
        ok = band & (kpos >= 0) & (kpos < L)
        bias = jnp.where(ok, 0.0, NEG_BIG).astype(F32)
        for h in range(N_KV_HEADS):
            kh = k[r0:r0 + span, h * HEAD_DIM:(h + 1) * HEAD_DIM]
            vht = vt[h * HEAD_DIM:(h + 1) * HEAD_DIM, r0:r0 + span]
            for g0 in range(0, Q_PER_KV, HEADS_PER_PASS):
                heads = [h * Q_PER_KV + g0 + g for g in range(HEADS_PER_PASS)]
                qh = jnp.concatenate(
                    [q_ref[0, r0:r0 + QB, a * HEAD_DIM:(a + 1) * HEAD_DIM] for a in heads], axis=0)
                s = lax.dot_general(kh, qh, (((1,), (1,)), ((), ())), preferred_element_type=F32)
                s = s + jnp.tile(bias, (1, HEADS_PER_PASS))
                sink = jnp.concatenate([jnp.full((1, QB), sink_ref[a], F32) for a in heads], axis=1)
                m = jnp.maximum(jnp.max(s, axis=0, keepdims=True), sink)
                p = jnp.exp(s - m)
                denom = jnp.sum(p, axis=0, keepdims=True) + jnp.exp(sink - m)
                ot = _dot(vht, p.astype(BF16)) / denom
                for g, a in enumerate(heads):
                    y_scr[a * HEAD_DIM:(a + 1) * HEAD_DIM, r0:r0 + QB] = ot[:, g * QB:(g + 1) * QB]
    yt = y_scr[...]
    inv = lax.rsqrt(jnp.mean(yt * yt, axis=0, keepdims=True) + EPS)
    o_ref[0] = ((yt * inv).T * nw_ref[...]).astype(o_ref.dtype)


def _windowed_attention(q, k, vt, sink, out_norm_w, tq=256):
    B, L, _ = q.shape
    rq = tq // WINDOW
    nwb = L // WINDOW
    cur = lambda b, i, s: (b, i, 0)
    prv = lambda b, i, s: (b, jnp.maximum(i * rq - 1, 0), 0)
    nxt = lambda b, i, s: (b, jnp.minimum((i + 1) * rq, nwb - 1), 0)
    cur_t = lambda b, i, s: (b, 0, i)
    prv_t = lambda b, i, s: (b, 0, jnp.maximum(i * rq - 1, 0))
    nxt_t = lambda b, i, s: (b, 0, jnp.minimum((i + 1) * rq, nwb - 1))
    grid_spec = pltpu.PrefetchScalarGridSpec(
        num_scalar_prefetch=1,
        grid=(B, L // tq),
        in_specs=[
            pl.BlockSpec((1, tq, ATTN_WIDTH), cur),
            pl.BlockSpec((1, WINDOW, KV_WIDTH), prv),
            pl.BlockSpec((1, tq, KV_WIDTH), cur),
            pl.BlockSpec((1, WINDOW, KV_WIDTH), nxt),
            pl.BlockSpec((1, KV_WIDTH, WINDOW), prv_t),
            pl.BlockSpec((1, KV_WIDTH, tq), cur_t),
            pl.BlockSpec((1, KV_WIDTH, WINDOW), nxt_t),
            pl.BlockSpec((1, ATTN_WIDTH), lambda b, i, s: (0, 0)),
        ],
        out_specs=pl.BlockSpec((1, tq, ATTN_WIDTH), cur),
        scratch_shapes=[pltpu.VMEM((ATTN_WIDTH, tq), F32)],
    )
    return pl.pallas_call(
        functools.partial(_attn_kernel, L=L),
        grid_spec=grid_spec,
        out_shape=jax.ShapeDtypeStruct((B, L, ATTN_WIDTH), BF16),
        compiler_params=_cparams("parallel", "parallel"),
        name="banded_attention",
    )(sink.astype(F32), q, k, k, k, vt, vt, vt, out_norm_w[None, :])


def _outproj_kernel(x_ref, a_ref, hy_ref, hw_ref, wa_ref, wh_ref, nw_ref, wr_ref,
                    x1_ref, h2_ref, lg_ref):
    nslab = hy_ref.shape[2]
    hy = jnp.concatenate(
        [_gather_cols(lambda c: hy_ref[0, c, j, n1 * SUBLANES:(n1 + 1) * SUBLANES, :], hy_ref.shape[1])
         for n1 in range(hy_ref.shape[3] // SUBLANES) for j in range(nslab)],
        axis=0)
    hinv = lax.rsqrt(jnp.mean(hy * hy, axis=-1, keepdims=True) + EPS)
    hy_n = (hy * hinv * hw_ref[...]).astype(BF16)
    x1 = x_ref[...] + _dot(a_ref[...], wa_ref[...]) + _dot(hy_n, wh_ref[...])
    x1_ref[...] = x1
    inv = lax.rsqrt(jnp.mean(x1 * x1, axis=-1, keepdims=True) + EPS)
    h2 = x1 * inv * nw_ref[...]
    h2_ref[...] = h2.astype(BF16)
    h_hi, h_lo = _split_bf16(h2)
    lg_ref[...] = _dot(h_hi, wr_ref[0]) + _dot(h_lo, wr_ref[0]) + _dot(h_hi, wr_ref[1])


def _out_proj(x2d, attn_n, hy, hy_norm_w, w_out_a, w_out_h, norm2_w, w_router, tm=256):
    T = x2d.shape[0]
    _, NC, J, LJ, _ = hy.shape
    lt = (LJ * J) // tm
    row = lambda i: (i, 0)
    const = lambda i: (0, 0)
    return pl.pallas_call(
        _outproj_kernel,
        grid=(T // tm,),
        in_specs=[
            pl.BlockSpec((tm, D_MODEL), row),
            pl.BlockSpec((tm, ATTN_WIDTH), row),
            pl.BlockSpec((1, NC, J, tm // J, LANES), lambda i: (i // lt, 0, 0, i % lt, 0)),
            pl.BlockSpec((1, HY_WIDTH), const),
            pl.BlockSpec((ATTN_WIDTH, D_MODEL), const),
            pl.BlockSpec((HY_WIDTH, D_MODEL), const),
            pl.BlockSpec((1, D_MODEL), const),
            pl.BlockSpec((2, D_MODEL, ROUTER_PAD), lambda i: (0, 0, 0)),
        ],
        out_specs=[
            pl.BlockSpec((tm, D_MODEL), row),
            pl.BlockSpec((tm, D_MODEL), row),
            pl.BlockSpec((tm, ROUTER_PAD), row),
        ],
        out_shape=[
            jax.ShapeDtypeStruct((T, D_MODEL), F32),
            jax.ShapeDtypeStruct((T, D_MODEL), BF16),
            jax.ShapeDtypeStruct((T, ROUTER_PAD), F32),
        ],
        compiler_params=_cparams("parallel"),
        name="out_proj",
    )(x2d, attn_n, hy, hy_norm_w[None, :], w_out_a, w_out_h, norm2_w[None, :], w_router)


def _cast_kernel(x_ref, o_ref):
    o_ref[...] = x_ref[...].astype(o_ref.dtype)


def _to_bf16(w, rb=2048):
    E, R, C = w.shape
    rb = min(rb, R)
    spec = pl.BlockSpec((1, rb, C), lambda e, i: (e, i, 0))
    return pl.pallas_call(
        _cast_kernel,
        grid=(E, R // rb),
        in_specs=[spec],
        out_specs=spec,
        out_shape=jax.ShapeDtypeStruct(w.shape, BF16),
        compiler_params=_cparams("parallel", "parallel"),
        name="expert_weights_to_bf16",
    )(w)


def _moe_kernel(be_ref, nu_ref, x_ref, wg_ref, wu_ref, wd_ref, o_ref):
    @pl.when(pl.program_id(0) < nu_ref[0])
    def _():
        x = x_ref[...]
        g = _dot(x, wg_ref[0])
        u = _dot(x, wu_ref[0])
        hid = (g * jax.nn.sigmoid(g)) * u
        o_ref[...] = _dot(hid.astype(BF16), wd_ref[0]).astype(o_ref.dtype)


def _expert_blocks(xs, block_expert, n_used, w_gate, w_up, w_down):
    P = xs.shape[0]
    bm = MOE_BLOCK_ROWS
    blk = lambda b, be, nu: (jnp.minimum(b, nu[0] - 1), 0)
    wsel = lambda b, be, nu: (be[jnp.minimum(b, nu[0] - 1)], 0, 0)
    grid_spec = pltpu.PrefetchScalarGridSpec(
        num_scalar_prefetch=2,
        grid=(P // bm,),
        in_specs=[
            pl.BlockSpec((bm, D_MODEL), blk),
            pl.BlockSpec((1, D_MODEL, D_EXPERT), wsel),
            pl.BlockSpec((1, D_MODEL, D_EXPERT), wsel),
            pl.BlockSpec((1, D_EXPERT, D_MODEL), wsel),
        ],
        out_specs=pl.BlockSpec((bm, D_MODEL), blk),
    )
    return pl.pallas_call(
        _moe_kernel,
        grid_spec=grid_spec,
        out_shape=jax.ShapeDtypeStruct((P, D_MODEL), BF16),
        compiler_params=_cparams("arbitrary"),
        name="moe_experts",
    )(block_expert, n_used, xs, w_gate, w_up, w_down)


def _hier_moe(x1, h2, logits, b_rg, b_re, w_gate, w_up, w_down):
    T = h2.shape[0]
    g_logits = logits[:, :N_EXPERT_GROUPS] + b_rg
    e_logits = (logits[:, N_EXPERT_GROUPS:N_EXPERT_GROUPS + N_EXPERTS] + b_re)
    e_logits = e_logits.reshape(T, N_EXPERT_GROUPS, EXPERTS_PER_GROUP)
    g_idx = jnp.argmax(g_logits, axis=-1)
    g_gate = jnp.take_along_axis(jax.nn.softmax(g_logits, axis=-1), g_idx[:, None], axis=1)
    e_sel = jnp.take_along_axis(e_logits, g_idx[:, None, None], axis=1)[:, 0]
    top_val, top_loc = lax.top_k(e_sel, TOP_K)
    weights = g_gate * jax.nn.softmax(top_val, axis=-1)
    experts = (g_idx[:, None] * EXPERTS_PER_GROUP + top_loc).astype(jnp.int32)

    A = T * TOP_K
    bm = MOE_BLOCK_ROWS
    flat_e = experts.reshape(-1)
    onehot = (flat_e[:, None] == jnp.arange(N_EXPERTS, dtype=jnp.int32)[None, :]).astype(jnp.int32)
    csum = jnp.cumsum(onehot, axis=0)
    counts = csum[-1]
    rank = jnp.sum((csum - onehot) * onehot, axis=1)
    padded = ((counts + bm - 1) // bm) * bm
    pend = jnp.cumsum(padded)
    pstart = pend - padded
    seg_start = jnp.cumsum(counts) - counts
    dest = (pstart[flat_e] + rank).astype(jnp.int32)
    n_blocks = (A + bm - 1) // bm + N_EXPERTS
    P = n_blocks * bm
    order = jnp.argsort(flat_e, stable=True).astype(jnp.int32)
    q = jnp.arange(P, dtype=jnp.int32)
    e_q = jnp.minimum(jnp.sum((pend[None, :] <= q[:, None]).astype(jnp.int32), axis=1), N_EXPERTS - 1)
    r_q = jnp.minimum(q - pstart[e_q], jnp.maximum(counts[e_q] - 1, 0))
    src = jnp.clip(seg_start[e_q] + r_q, 0, A - 1)
    buf_tok = order.at[src].get(mode="promise_in_bounds") // TOP_K
    block_expert = e_q.reshape(n_blocks, bm)[:, 0]
    n_used = (pend[-1] // bm).astype(jnp.int32)[None]

    xs = h2.at[buf_tok].get(mode="promise_in_bounds")
    yb = _expert_blocks(xs, block_expert, n_used, w_gate, w_up, w_down)
    d = dest.reshape(T, TOP_K)
    y0 = yb.at[d[:, 0]].get(mode="promise_in_bounds").astype(F32)
    y1 = yb.at[d[:, 1]].get(mode="promise_in_bounds").astype(F32)
    return x1 + (weights[:, 0:1] * y0 + weights[:, 1:2] * y1)


def _encoder_layer(x, p):
    B, L, D = x.shape
    T = B * L
    x2d = x.reshape(T, D)
    q, k, v = _qkv_proj(x2d, L, p["norm1_w"], p["w_qkv"], p["q_norm_w"], p["k_norm_w"])
    attn_n = _windowed_attention(q.reshape(B, L, ATTN_WIDTH), k.reshape(B, L, KV_WIDTH),
                                 v, p["attn_sink"], p["attn_out_norm_w"])
    uc = _u_proj_conv(x2d, L, p["norm1_w"], p["w_u"], p["conv_w"], p["conv_b"])
    tabs = _fft_tables(L)
    g_spec = _hyena_filter_spectrum(L, tabs, p["filt_w1"], p["filt_b1"], p["filt_w2"], p["filt_b2"],
                                    p["filt_w3"], p["filt_b3"], p["filt_w4"], p["filt_freq"])
    hy = _hyena_mixer(uc, g_spec, p["hy_bias"], tabs)
    x1, h2, logits = _out_proj(x2d, attn_n.reshape(T, ATTN_WIDTH), hy, p["hy_out_norm_w"],
                               p["w_out_a"], p["w_out_h"], p["norm2_w"], p["w_router"])
    out = _hier_moe(x1, h2, logits, p["b_route_group"], p["b_route_expert"],
                    p["w_gate"], p["w_up"], p["w_down"])
    return out.reshape(B, L, D)


def kernel(x_prompt, x_sample, norm1_w, w_in, q_norm_w, k_norm_w, attn_sink, conv_w, conv_b, filt_w1, filt_b1, filt_w2, filt_b2, filt_w3, filt_b3, filt_w4, filt_freq, hy_bias, attn_out_norm_w, hy_out_norm_w, w_out, norm2_w, w_route_group, b_route_group, w_route_expert, b_route_expert, w_gate, w_up, w_down):
    depth = norm1_w.shape[0]

    def layer_params(l):
        w_r = jnp.zeros((D_MODEL, ROUTER_PAD), F32)
        w_r = w_r.at[:, :N_EXPERT_GROUPS].set(w_route_group[l])
        w_r = w_r.at[:, N_EXPERT_GROUPS:N_EXPERT_GROUPS + N_EXPERTS].set(w_route_expert[l])
        r_hi, r_lo = _split_bf16(w_r)
        return dict(
            norm1_w=norm1_w[l], w_qkv=w_in[l][:, :QKV_WIDTH].astype(BF16),
            w_u=w_in[l][:, QKV_WIDTH:].astype(BF16), q_norm_w=q_norm_w[l], k_norm_w=k_norm_w[l],
            attn_sink=attn_sink[l], conv_w=conv_w[l], conv_b=conv_b[l],
            filt_w1=filt_w1[l], filt_b1=filt_b1[l], filt_w2=filt_w2[l], filt_b2=filt_b2[l],
            filt_w3=filt_w3[l], filt_b3=filt_b3[l], filt_w4=filt_w4[l], filt_freq=filt_freq[l],
            hy_bias=hy_bias[l], attn_out_norm_w=attn_out_norm_w[l], hy_out_norm_w=hy_out_norm_w[l],
            w_out_a=w_out[l][:ATTN_WIDTH].astype(BF16), w_out_h=w_out[l][ATTN_WIDTH:].astype(BF16),
            norm2_w=norm2_w[l], w_router=jnp.stack([r_hi, r_lo]),
            b_route_group=b_route_group[l], b_route_expert=b_route_expert[l],
            w_gate=_to_bf16(w_gate[l]), w_up=_to_bf16(w_up[l]), w_down=_to_bf16(w_down[l]))

    params = [layer_params(l) for l in range(depth)]

    def trunk(x):
        for p in params:
            x = _encoder_layer(x, p)
        return x

    return (trunk(x_prompt), trunk(x_sample))
```

```python
import functools
import math

import jax
import jax.numpy as jnp
from jax import lax
from jax.experimental import pallas as pl
from jax.experimental.pallas import tpu as pltpu

F32 = jnp.float32
BF16 = jnp.bfloat16

D_MODEL = 2048
HEAD_DIM = 64
N_Q_HEADS = 16
N_KV_HEADS = 4
Q_PER_KV = N_Q_HEADS // N_KV_HEADS
ATTN_WIDTH = N_Q_HEADS * HEAD_DIM
KV_WIDTH = N_KV_HEADS * HEAD_DIM
QK_WIDTH = ATTN_WIDTH + KV_WIDTH
QKV_WIDTH = ATTN_WIDTH + 2 * KV_WIDTH
WINDOW = 128
ROT_DIM = HEAD_DIM // 4
ROPE_THETA = 500000.0
HY_WIDTH = D_MODEL - ATTN_WIDTH
HY_ORDER = 2
HY_EMB_DIM = 33
HY_FILTER_HIDDEN = 64
HY_DECAY_TARGET = 1e-2
HY_FAST_DECAY_PCT = 0.3
HY_SLOW_DECAY_PCT = 1.5
N_EXPERT_GROUPS = 4
EXPERTS_PER_GROUP = 8
N_EXPERTS = N_EXPERT_GROUPS * EXPERTS_PER_GROUP
TOP_K = 2
D_EXPERT = 1024
EPS = 1e-6

LANES = 128
SUBLANES = 8
FFT_N2 = 128
MOE_BLOCK_ROWS = 256
ROUTER_PAD = 128
VMEM_LIMIT = 56 * 1024 * 1024
NEG_BIG = -1e30
HEADS_PER_PASS = Q_PER_KV


def _cparams(*sem):
    return pltpu.CompilerParams(dimension_semantics=sem, vmem_limit_bytes=VMEM_LIMIT)


def _split_bf16(a):
    hi = a.astype(BF16)
    lo = (a - hi.astype(F32)).astype(BF16)
    return hi, lo


def _dot(a, b):
    return jnp.dot(a, b, preferred_element_type=F32)


def _dot3(a, b):
    a_hi, a_lo = _split_bf16(a)
    b_hi, b_lo = _split_bf16(b)
    return _dot(a_hi, b_hi) + _dot(a_lo, b_hi) + _dot(a_hi, b_lo)


def _qkv_kernel(x_ref, nw_ref, w_ref, seg_ref, hw_ref, rc_ref, rs1_ref, rs2_ref,
                q_ref, k_ref, v_ref):
    x = x_ref[...]
    inv = lax.rsqrt(jnp.mean(x * x, axis=-1, keepdims=True) + EPS)
    h = (x * inv * nw_ref[...]).astype(BF16)
    acc = _dot(h, w_ref[...])
    qk = acc[:, :QK_WIDTH]
    hi, lo = _split_bf16(qk * qk)
    seg = seg_ref[...]
    w = seg.shape[0]
    ms = jnp.concatenate(
        [_dot(hi[:, c * w:(c + 1) * w], seg) + _dot(lo[:, c * w:(c + 1) * w], seg)
         for c in range(QK_WIDTH // w)], axis=-1)
    xn = qk * lax.rsqrt(ms + EPS) * hw_ref[...]
    reps = QK_WIDTH // LANES
    rc = jnp.tile(rc_ref[...], (1, reps))
    rs1 = jnp.tile(rs1_ref[...], (1, reps))
    rs2 = jnp.tile(rs2_ref[...], (1, reps))
    half = ROT_DIM // 2
    y = xn * rc + pltpu.roll(xn, QK_WIDTH - half, 1) * rs1 + pltpu.roll(xn, half, 1) * rs2
    q_ref[...] = (y[:, :ATTN_WIDTH] * (HEAD_DIM ** -0.5)).astype(BF16)
    k_ref[...] = y[:, ATTN_WIDTH:].astype(BF16)
    v_ref[0] = acc[:, QK_WIDTH:].T.astype(BF16)


def _rope_tables(L):
    half = ROT_DIM // 2
    inv_freq = jnp.power(ROPE_THETA, -jnp.arange(half, dtype=F32) * 2.0 / ROT_DIM)
    ang = jnp.arange(L, dtype=F32)[:, None] * inv_freq[None, :]
    d = jnp.arange(LANES) % HEAD_DIM
    ang_l = ang[:, d % half]
    cos, sin = jnp.cos(ang_l), jnp.sin(ang_l)
    rc = jnp.where(d < ROT_DIM, cos, 1.0)
    rs1 = jnp.where(d < half, -sin, 0.0)
    rs2 = jnp.where((d >= half) & (d < ROT_DIM), sin, 0.0)
    return rc.astype(F32), rs1.astype(F32), rs2.astype(F32)


def _qkv_proj(x2d, L, norm1_w, w_qkv, q_norm_w, k_norm_w, tm=512):
    T = x2d.shape[0]
    seg_w = 256
    seg = (jnp.arange(seg_w)[:, None] // HEAD_DIM == jnp.arange(seg_w)[None, :] // HEAD_DIM)
    seg = (seg.astype(F32) / HEAD_DIM).astype(BF16)
    hw = jnp.concatenate([jnp.tile(q_norm_w, N_Q_HEADS), jnp.tile(k_norm_w, N_KV_HEADS)])[None, :]
    rc, rs1, rs2 = _rope_tables(L)
    lt = L // tm
    row = lambda i: (i, 0)
    const = lambda i: (0, 0)
    pos = lambda i: (i % lt, 0)
    return pl.pallas_call(
        _qkv_kernel,
        grid=(T // tm,),
        in_specs=[
            pl.BlockSpec((tm, D_MODEL), row),
            pl.BlockSpec((1, D_MODEL), const),
            pl.BlockSpec((D_MODEL, QKV_WIDTH), const),
            pl.BlockSpec((seg_w, seg_w), const),
            pl.BlockSpec((1, QK_WIDTH), const),
            pl.BlockSpec((tm, LANES), pos),
            pl.BlockSpec((tm, LANES), pos),
            pl.BlockSpec((tm, LANES), pos),
        ],
        out_specs=[
            pl.BlockSpec((tm, ATTN_WIDTH), row),
            pl.BlockSpec((tm, KV_WIDTH), row),
            pl.BlockSpec((1, KV_WIDTH, tm), lambda i: (i // lt, 0, i % lt)),
        ],
        out_shape=[
            jax.ShapeDtypeStruct((T, ATTN_WIDTH), BF16),
            jax.ShapeDtypeStruct((T, KV_WIDTH), BF16),
            jax.ShapeDtypeStruct((T // L, KV_WIDTH, L), BF16),
        ],
        compiler_params=_cparams("parallel"),
        name="qkv_proj",
    )(x2d, norm1_w[None, :], w_qkv, seg, hw, rc, rs1, rs2)


HALO = 16

def _uproj_kernel(x_ref, xp_ref, xn_ref, nw_ref, w_ref, cw_ref, cb_ref, o_ref, h_scr, *, lt):
    i = pl.program_id(0)
    tm = x_ref.shape[0]

    @pl.when(pl.program_id(1) == 0)
    def _():
        def normed(x):
            inv = lax.rsqrt(jnp.mean(x * x, axis=-1, keepdims=True) + EPS)
            return x * inv * nw_ref[...]

        li = i % lt
        hp = jnp.where(li > 0, normed(xp_ref[...]), 0.0)
        hn = jnp.where(li < lt - 1, normed(xn_ref[...]), 0.0)
        h_scr[0:HALO] = hp.astype(BF16)
        h_scr[HALO:HALO + tm] = normed(x_ref[...]).astype(BF16)
        h_scr[HALO + tm:2 * HALO + tm] = hn.astype(BF16)

    u = _dot(h_scr[...], w_ref[...])
    w = cw_ref[...]
    val = (w[0:1] * u[HALO - 1:HALO - 1 + tm] + w[1:2] * u[HALO:HALO + tm]
           + w[2:3] * u[HALO + 1:HALO + 1 + tm] + cb_ref[...])
    for n1 in range(tm // FFT_N2):
        for j in range(FFT_N2 // SUBLANES):
            r0 = n1 * FFT_N2 + j * SUBLANES
            for c in range(val.shape[1] // LANES):
                o_ref[0, 0, c, j, n1 * SUBLANES:(n1 + 1) * SUBLANES, :] = (
                    val[r0:r0 + SUBLANES, c * LANES:(c + 1) * LANES])


def _u_proj_conv(x2d, L, norm1_w, w_u, conv_w, conv_b, tm=1024):
    T = x2d.shape[0]
    C = HY_WIDTH
    nparts = w_u.shape[1] // C
    lt = L // tm
    hb = tm // HALO
    J = FFT_N2 // SUBLANES
    return pl.pallas_call(
        functools.partial(_uproj_kernel, lt=lt),
        grid=(T // tm, nparts),
        in_specs=[
            pl.BlockSpec((tm, D_MODEL), lambda i, j: (i, 0)),
            pl.BlockSpec((HALO, D_MODEL), lambda i, j: (jnp.maximum(i * hb - 1, 0), 0)),
            pl.BlockSpec((HALO, D_MODEL), lambda i, j: (jnp.minimum((i + 1) * hb, T // HALO - 1), 0)),
            pl.BlockSpec((1, D_MODEL), lambda i, j: (0, 0)),
            pl.BlockSpec((D_MODEL, C), lambda i, j: (0, j)),
            pl.BlockSpec((3, C), lambda i, j: (0, j)),
            pl.BlockSpec((1, C), lambda i, j: (0, j)),
        ],
        out_specs=pl.BlockSpec((1, 1, C // LANES, J, tm // J, LANES),
                               lambda i, j: (j, i // lt, 0, 0, i % lt, 0)),
        out_shape=jax.ShapeDtypeStruct((nparts, T // L, C // LANES, J, L // J, LANES), F32),
        scratch_shapes=[pltpu.VMEM((tm + 2 * HALO, D_MODEL), BF16)],
        compiler_params=_cparams("parallel", "arbitrary"),
        name="u_proj_conv",
    )(x2d, x2d, x2d, norm1_w[None, :], w_u, conv_w, conv_b[None, :])


def _fft_tables(L):
    N = 2 * L
    N2 = FFT_N2
    N1 = N // N2
    H = N1 // 2
    k1 = jnp.arange(N1, dtype=jnp.int32)
    n2 = jnp.arange(N2, dtype=jnp.int32)
    a1 = ((k1[:, None] * k1[None, :]) % N1).astype(F32) * (2.0 * math.pi / N1)
    c1, s1 = jnp.cos(a1), jnp.sin(a1)
    at = (n2[:, None] * k1[None, :]).astype(F32) * (2.0 * math.pi / N)
    ct, st = jnp.cos(at), jnp.sin(at)
    c = c1[None] * ct[:, :, None] - s1[None] * st[:, :, None]
    s = s1[None] * ct[:, :, None] + c1[None] * st[:, :, None]
    m_filt = jnp.concatenate([c, -s], axis=1).astype(BF16)
    ch, sh = c[:, :, :H], s[:, :, :H]
    m_fwd = jnp.concatenate([jnp.concatenate([ch, sh], axis=2),
                             jnp.concatenate([-sh, ch], axis=2)], axis=1)
    cT = c1[None, :H] * ct[:, None, :] - s1[None, :H] * st[:, None, :]
    sT = s1[None, :H] * ct[:, None, :] + c1[None, :H] * st[:, None, :]
    m_inv = jnp.concatenate([jnp.concatenate([cT, -sT], axis=2),
                             jnp.concatenate([sT, cT], axis=2)], axis=1)
    k2 = jnp.arange(N2, dtype=jnp.int32)
    ph = ((k2[:, None] * n2[None, :]) % N2).astype(F32) * (2.0 * math.pi / N2)
    c2, s2 = jnp.cos(ph), jnp.sin(ph)
    m2_fwd = jnp.concatenate([jnp.concatenate([c2, s2], axis=1),
                              jnp.concatenate([-s2, c2], axis=1)], axis=0)
    m2_inv = m2_fwd.T
    return dict(N1=N1, N2=N2, m_filt=m_filt, m_fwd=m_fwd.astype(BF16), m_inv=m_inv.astype(BF16),
                m2_fwd=m2_fwd.astype(BF16), m2_inv=m2_inv.astype(BF16))


def _filt1_kernel(w1_ref, b1_ref, w2_ref, b2_ref, w3_ref, b3_ref, w4_ref, a_ref, frl_ref, off_ref,
                  dl_ref, m_ref, o_ref, sabs_ref, *, L, N1, N2, nb):
    j = pl.program_id(0)
    C2 = HY_ORDER * HY_WIDTH
    H = N1 // 2
    rows = nb * N1
    ridx = lax.broadcasted_iota(jnp.int32, (rows, LANES), 0)
    lane = lax.broadcasted_iota(jnp.int32, (rows, LANES), 1)
    n2 = j * nb + ridx // N1
    n1 = ridx % N1
    r = n1 * N2 + n2
    lag = jnp.where(r < L, r, 2 * L - r)
    valid = (r != L).astype(F32)
    lagf = jnp.minimum(lag, L - 1).astype(F32)
    t = lagf / (L - 1)
    wpos = lagf * (2.0 * math.pi / L)
    phase = wpos * frl_ref[...] + off_ref[...]
    zemb = jnp.where(lane == 0, t, jnp.where(lane < HY_EMB_DIM, jnp.cos(phase), 0.0))
    half = rows // 2
    a = a_ref[...]
    h = jnp.concatenate([zemb[:half], zemb[half:]], axis=1)
    h = jnp.sin(a * (_dot3(h, w1_ref[...]) + b1_ref[...]))
    h = jnp.sin(a * (_dot3(h, w2_ref[...]) + b2_ref[...]))
    h = jnp.sin(a * (_dot3(h, w3_ref[...]) + b3_ref[...]))
    h_hi, h_lo = _split_bf16(h)
    decay = jnp.exp(-jnp.tile(t, (1, HY_WIDTH // LANES)) * dl_ref[...])
    decay = decay * jnp.tile(valid, (1, HY_WIDTH // LANES))
    decay2 = jnp.tile(decay, (1, HY_ORDER))

    @pl.when(j == 0)
    def _():
        sabs_ref[...] = jnp.zeros_like(sabs_ref)

    tot = jnp.zeros((8, C2), F32)
    for l in range(nb):
        side = (l * N1) // half
        r0 = l * N1 - side * half
        fs = slice(r0, r0 + H)
        bs = slice(r0 + H, r0 + N1)
        wf = w4_ref[side, :, :C2]
        wb = w4_ref[side, :, C2:]
        gf = _dot(h_hi[fs], wf) + _dot(h_lo[fs], wf)
        gb = _dot(h_hi[bs], wb) + _dot(h_lo[bs], wb)
        g = jnp.concatenate([gf, gb], axis=0) * decay2[l * N1:(l + 1) * N1]
        tot = tot + jnp.sum(jnp.abs(g).reshape(N1 // 8, 8, C2), axis=0)
        y = _dot(m_ref[l], g.astype(BF16))
        _store_tiles(lambda c: o_ref.at[c], _pack_complex(y[:N1], y[N1:]), C2 // LANES, l)
    sabs_ref[...] += tot


def _pack_complex(re, im):
    r = lax.bitcast_convert_type(re.astype(BF16).astype(F32), jnp.uint32)
    i = lax.bitcast_convert_type(im.astype(BF16).astype(F32), jnp.uint32)
    return r | (i >> 16)


def _unpack_complex(w):
    re = lax.bitcast_convert_type(w & jnp.uint32(0xFFFF0000), F32)
    im = lax.bitcast_convert_type(w << 16, F32)
    return re, im


def _store_tiles(dst, w, ncols, s):
    for c in range(ncols):
        d = dst(c)
        for t in range(w.shape[0] // SUBLANES):
            d[t, s * SUBLANES:(s + 1) * SUBLANES, :] = (
                w[t * SUBLANES:(t + 1) * SUBLANES, c * LANES:(c + 1) * LANES])


def _rows_strided(ref2d, s):
    return ref2d[pl.ds(s, ref2d.shape[0] // SUBLANES, stride=SUBLANES), :]


def _gather_cols(fn, ncols):
    return jnp.concatenate([fn(c) for c in range(ncols)], axis=1)


def _block_diag2(w):
    z = jnp.zeros_like(w)
    return jnp.concatenate([jnp.concatenate([w, z], axis=1), jnp.concatenate([z, w], axis=1)], axis=0)


def _hyena_filter_spectrum(L, tabs, filt_w1, filt_b1, filt_w2, filt_b2, filt_w3, filt_b3,
                           filt_w4, filt_freq):
    N1, N2 = tabs["N1"], tabs["N2"]
    N = N1 * N2
    C2 = HY_ORDER * HY_WIDTH
    FH = HY_FILTER_HIDDEN
    nb = max(2, 512 // N1)
    bands = (HY_EMB_DIM - 1) // 2
    fr = jnp.linspace(1e-4, bands - 1, bands, dtype=F32)
    frl = jnp.zeros((LANES,), F32).at[1:1 + bands].set(fr).at[1 + bands:1 + 2 * bands].set(fr)[None, :]
    off = jnp.zeros((LANES,), F32).at[1 + bands:1 + 2 * bands].set(0.5 * math.pi)[None, :]
    w1p = jnp.zeros((LANES, FH), F32).at[:HY_EMB_DIM].set(filt_w1)
    w4 = filt_w4.astype(BF16)
    w4z = jnp.zeros_like(w4)
    w4s = jnp.stack([jnp.concatenate([w4, w4z], axis=0), jnp.concatenate([w4z, w4], axis=0)])
    two = lambda v: jnp.tile(v, 2)[None, :]
    max_decay = math.log(HY_DECAY_TARGET) / HY_FAST_DECAY_PCT
    min_decay = math.log(HY_DECAY_TARGET) / HY_SLOW_DECAY_PCT
    deltas = jnp.abs(jnp.linspace(min_decay, max_decay, HY_WIDTH, dtype=F32))[None, :]
    full = lambda shape: pl.BlockSpec(shape, lambda j: (0,) * len(shape))
    a_g, sabs = pl.pallas_call(
        functools.partial(_filt1_kernel, L=L, N1=N1, N2=N2, nb=nb),
        grid=(N2 // nb,),
        in_specs=[
            full((2 * LANES, 2 * FH)), full((1, 2 * FH)),
            full((2 * FH, 2 * FH)), full((1, 2 * FH)),
            full((2 * FH, 2 * FH)), full((1, 2 * FH)),
            full((2, 2 * FH, 2 * C2)), full((1, 2 * FH)),
            full((1, LANES)), full((1, LANES)), full((1, HY_WIDTH)),
            pl.BlockSpec((nb, 2 * N1, N1), lambda j: (j, 0, 0)),
        ],
        out_specs=[
            pl.BlockSpec((C2 // LANES, N1 // SUBLANES, nb * SUBLANES, LANES), lambda j: (0, 0, j, 0)),
            pl.BlockSpec((8, C2), lambda j: (0, 0)),
        ],
        out_shape=[
            jax.ShapeDtypeStruct((C2 // LANES, N1 // SUBLANES, N2 * SUBLANES, LANES), jnp.uint32),
            jax.ShapeDtypeStruct((8, C2), F32),
        ],
        compiler_params=_cparams("arbitrary"),
        name="hyena_filter_stage1",
    )(_block_diag2(w1p), two(filt_b1), _block_diag2(filt_w2), two(filt_b2), _block_diag2(filt_w3),
      two(filt_b3), w4s, two(filt_freq), frl, off, deltas, tabs["m_filt"])
    scale = (1.0 / (jnp.sum(sabs, axis=0) * N))[None, :]
    return a_g, scale


def _fwd1_kernel(x_ref, m_ref, o_ref):
    cb = o_ref.shape[1]
    N1 = m_ref.shape[2]
    for s in range(SUBLANES):
        rows = [_gather_cols(lambda c: _rows_strided(x_ref.at[0, bi, c, 0], s), cb) for bi in range(2)]
        xl = jnp.concatenate(rows, axis=0).astype(BF16)
        y = _dot(m_ref[s], xl)
        _store_tiles(lambda c: o_ref.at[0, c], _pack_complex(y[:N1], y[N1:]), cb, s)


def _hyena_stage1(z, which, tabs):
    N1, N2 = tabs["N1"], tabs["N2"]
    H = N1 // 2
    B, NC = z.shape[1], z.shape[2]
    cb = min(NC, 4)
    x_spec = pl.BlockSpec((1, 2, cb, 1, H * SUBLANES, LANES), lambda p, c, j: (which, p, c, j, 0, 0))
    return pl.pallas_call(
        _fwd1_kernel,
        grid=(B // 2, NC // cb, N2 // SUBLANES),
        in_specs=[x_spec, pl.BlockSpec((SUBLANES, 2 * N1, N1), lambda p, c, j: (j, 0, 0))],
        out_specs=pl.BlockSpec((1, cb, N1 // SUBLANES, SUBLANES * SUBLANES, LANES),
                               lambda p, c, j: (p, c, 0, j, 0)),
        out_shape=jax.ShapeDtypeStruct((B // 2, NC, N1 // SUBLANES, N2 * SUBLANES, LANES), jnp.uint32),
        compiler_params=_cparams("parallel", "parallel", "parallel"),
        name="hyena_stage1",
    )(z, tabs["m_fwd"])


def _mid_kernel(a_ref, ag_ref, sc_ref, mf_ref, mi_ref, o_ref, g_scr):
    cb = a_ref.shape[1]
    N2 = a_ref.shape[3] // SUBLANES

    @pl.when(pl.program_id(2) == 0)
    def _():
        for s in range(SUBLANES):
            re, im = _unpack_complex(_gather_cols(lambda c: _rows_strided(ag_ref.at[c, 0], s), cb))
            ag = jnp.concatenate([re, im], axis=0).astype(BF16)
            g_scr[s] = _dot(mf_ref[...], ag) * sc_ref[...]

    for s in range(SUBLANES):
        re, im = _unpack_complex(_gather_cols(lambda c: _rows_strided(a_ref.at[0, c, 0], s), cb))
        a = jnp.concatenate([re, im], axis=0).astype(BF16)
        x = _dot(mf_ref[...], a)
        xr, xi = x[:N2], x[N2:]
        gr, gi = g_scr[s, :N2], g_scr[s, N2:]
        y = jnp.concatenate([xr * gr - xi * gi, xr * gi + xi * gr], axis=0).astype(BF16)
        b = _dot(mi_ref[...], y)
        _store_tiles(lambda c: o_ref.at[0, c], _pack_complex(b[:N2], b[N2:]), cb, s)


def _hyena_stage2(a, a_g, scale, order, tabs, cb=8):
    N1, N2 = tabs["N1"], tabs["N2"]
    P, NC = a.shape[:2]
    cb = min(cb, NC)
    ncb = NC // cb
    return pl.pallas_call(
        _mid_kernel,
        grid=(N1 // SUBLANES, ncb, P),
        in_specs=[
            pl.BlockSpec((1, cb, 1, N2 * SUBLANES, LANES), lambda i, c, p: (p, c, i, 0, 0)),
            pl.BlockSpec((cb, 1, N2 * SUBLANES, LANES), lambda i, c, p: (order * ncb + c, i, 0, 0)),
            pl.BlockSpec((1, cb * LANES), lambda i, c, p: (0, order * ncb + c)),
            pl.BlockSpec((2 * N2, 2 * N2), lambda i, c, p: (0, 0)),
            pl.BlockSpec((2 * N2, 2 * N2), lambda i, c, p: (0, 0)),
        ],
        out_specs=pl.BlockSpec((1, cb, N2 // SUBLANES, SUBLANES * SUBLANES, LANES),
                               lambda i, c, p: (p, c, 0, i, 0)),
        out_shape=jax.ShapeDtypeStruct((P, NC, N2 // SUBLANES, N1 * SUBLANES, LANES), jnp.uint32),
        scratch_shapes=[pltpu.VMEM((SUBLANES, 2 * N2, cb * LANES), F32)],
        compiler_params=_cparams("parallel", "parallel", "arbitrary"),
        name="hyena_stage2",
    )(a, a_g, scale, tabs["m2_fwd"], tabs["m2_inv"])


def _inv1_kernel(b_ref, m_ref, gate_ref, z_ref, bias_ref, *rest, z_slabbed, final):
    if final:
        (o_ref,) = rest
    else:
        mf_ref, o_ref, a_ref = rest
    cb = b_ref.shape[1]
    N1 = m_ref.shape[1]
    H = N1 // 2
    bias = bias_ref[...]
    for s in range(SUBLANES):
        br, bi_ = _unpack_complex(_gather_cols(lambda c: _rows_strided(b_ref.at[0, c, 0], s), cb))
        b = jnp.concatenate([br, bi_], axis=0).astype(BF16)
        conv = _dot(m_ref[s], b)
        outs = []
        for bi in range(2):
            gate = _gather_cols(lambda c: _rows_strided(gate_ref.at[0, bi, c, 0], s), cb)
            if z_slabbed:
                z = _gather_cols(lambda c: _rows_strided(z_ref.at[0, bi, c, 0], s), cb)
            else:
                z = _gather_cols(lambda c: z_ref[bi, c, s], cb)
            out = gate * (conv[bi * H:(bi + 1) * H] + z * bias)
            outs.append(out)
            for c in range(cb):
                piece = out[:, c * LANES:(c + 1) * LANES]
                if final:
                    o_ref.at[bi, c, 0][pl.ds(s, H, stride=SUBLANES), :] = piece
                else:
                    o_ref[bi, c, s] = piece
        if not final:
            xl = jnp.concatenate(outs, axis=0).astype(BF16)
            y = _dot(mf_ref[s], xl)
            _store_tiles(lambda c: a_ref.at[0, c], _pack_complex(y[:N1], y[N1:]), cb, s)


def _hyena_stage3(bq, ucl, gate_idx, z, z_idx, bias, tabs, final):
    N1, N2 = tabs["N1"], tabs["N2"]
    H = N1 // 2
    P, NC = bq.shape[:2]
    cb = min(NC, 4 if final else max(1, 512 // N1))
    z_slabbed = z_idx is not None
    slab_blk = (1, 2, cb, 1, H * SUBLANES, LANES)
    n2m_blk = (2, cb, SUBLANES, H, LANES)
    if z_slabbed:
        z_spec = pl.BlockSpec(slab_blk, lambda p, c, j: (z_idx, p, c, j, 0, 0))
    else:
        z_spec = pl.BlockSpec(n2m_blk, lambda p, c, j: (p, c, j, 0, 0))
    in_specs = [
        pl.BlockSpec((1, cb, 1, N1 * SUBLANES, LANES), lambda p, c, j: (p, c, j, 0, 0)),
        pl.BlockSpec((SUBLANES, N1, 2 * N1), lambda p, c, j: (j, 0, 0)),
        pl.BlockSpec(slab_blk, lambda p, c, j: (gate_idx, p, c, j, 0, 0)),
        z_spec,
        pl.BlockSpec((1, cb * LANES), lambda p, c, j: (0, c)),
    ]
    args = [bq, tabs["m_inv"], ucl, z, bias[None, :]]
    if final:
        out_specs = pl.BlockSpec((2, cb, 1, H * SUBLANES, LANES), lambda p, c, j: (p, c, j, 0, 0))
        out_shape = jax.ShapeDtypeStruct((2 * P, NC, N2 // SUBLANES, H * SUBLANES, LANES), F32)
    else:
        in_specs.append(pl.BlockSpec((SUBLANES, 2 * N1, N1), lambda p, c, j: (j, 0, 0)))
        args.append(tabs["m_fwd"])
        out_specs = [
            pl.BlockSpec(n2m_blk, lambda p, c, j: (p, c, j, 0, 0)),
            pl.BlockSpec((1, cb, N1 // SUBLANES, SUBLANES * SUBLANES, LANES),
                         lambda p, c, j: (p, c, 0, j, 0)),
        ]
        out_shape = [
            jax.ShapeDtypeStruct((2 * P, NC, N2, H, LANES), F32),
            jax.ShapeDtypeStruct((P, NC, N1 // SUBLANES, N2 * SUBLANES, LANES), jnp.uint32),
        ]
    return pl.pallas_call(
        functools.partial(_inv1_kernel, z_slabbed=z_slabbed, final=final),
        grid=(P, NC // cb, N2 // SUBLANES),
        in_specs=in_specs,
        out_specs=out_specs,
        out_shape=out_shape,
        compiler_params=_cparams("parallel", "parallel", "parallel"),
        name="hyena_stage3",
    )(*args)


def _hyena_mixer(ucl, g_spec, hy_bias, tabs):
    a_g, scale = g_spec
    z, z_idx = ucl, 2
    a = _hyena_stage1(z, z_idx, tabs)
    for order in range(HY_ORDER):
        bq = _hyena_stage2(a, a_g, scale, order, tabs)
        if order == HY_ORDER - 1:
            return _hyena_stage3(bq, ucl, order, z, z_idx, hy_bias[order], tabs, final=True)
        z, a = _hyena_stage3(bq, ucl, order, z, z_idx, hy_bias[order], tabs, final=False)
        z_idx = None


def _attn_kernel(sink_ref, q_ref, kp_ref, kc_ref, kn_ref, vp_ref, vc_ref, vn_ref, nw_ref,
                 o_ref, y_scr, *, L):
    i = pl.program_id(1)
    tq = q_ref.shape[1]
    QB = WINDOW
    span = QB + 2 * WINDOW
    k = jnp.concatenate([kp_ref[0], kc_ref[0], kn_ref[0]], axis=0)
    vt = jnp.concatenate([vp_ref[0], vc_ref[0], vn_ref[0]], axis=1)
    key = lax.broadcasted_iota(jnp.int32, (span, QB), 0)
    qry = lax.broadcasted_iota(jnp.int32, (span, QB), 1)
    rel = key - qry
    band = (rel >= 0) & (rel <= 2 * WINDOW)
    for sub in range(tq // QB):
        r0 = sub * QB
        kpos = i * tq + r0 - WINDOW + key
        ok = band & (kpos >= 0) & (kpos < L)
        bias = jnp.where(ok, 0.0, NEG_BIG).astype(F32)
        for h in range(N_KV_HEADS):
            kh = k[r0:r0 + span, h * HEAD_DIM:(h + 1) * HEAD_DIM]
            vht = vt[h * HEAD_DIM:(h + 1) * HEAD_DIM, r0:r0 + span]
            for g0 in range(0, Q_PER_KV, HEADS_PER_PASS):
                heads = [h * Q_PER_KV + g0 + g for g in range(HEADS_PER_PASS)]
                qh = jnp.concatenate(
                    [q_ref[0, r0:r0 + QB, a * HEAD_DIM:(a + 1) * HEAD_DIM] for a in heads], axis=0)
                s = lax.dot_general(kh, qh, (((1,), (1,)), ((), ())), preferred_element_type=F32)
                s = s + jnp.tile(bias, (1, HEADS_PER_PASS))
                sink = jnp.concatenate([jnp.full((1, QB), sink_ref[a], F32) for a in heads], axis=1)
                m = jnp.maximum(jnp.max(s, axis=0, keepdims=True), sink)
                p = jnp.exp(s - m)
                denom = jnp.sum(p, axis=0, keepdims=True) + jnp.exp(sink - m)
                ot = _dot(vht, p.astype(BF16)) / denom
                for g, a in enumerate(heads):
                    y_scr[a * HEAD_DIM:(a + 1) * HEAD_DIM, r0:r0 + QB] = ot[:, g * QB:(g + 1) * QB]
    yt = y_scr[...]
    inv = lax.rsqrt(jnp.mean(yt * yt, axis=0, keepdims=True) + EPS)
    o_ref[0] = ((yt * inv).T * nw_ref[...]).astype(o_ref.dtype)


def _windowed_attention(q, k, vt, sink, out_norm_w, tq=256):
    B, L, _ = q.shape
    rq = tq // WINDOW
    nwb = L // WINDOW
    cur = lambda b, i, s: (b, i, 0)
    prv = lambda b, i, s: (b, jnp.maximum(i * rq - 1, 0), 0)
    nxt = lambda b, i, s: (b, jnp.minimum((i + 1) * rq, nwb - 1), 0)
    cur_t = lambda b, i, s: (b, 0, i)
    prv_t = lambda b, i, s: (b, 0, jnp.maximum(i * rq - 1, 0))
    nxt_t = lambda b, i, s: (b, 0, jnp.minimum((i + 1) * rq, nwb - 1))
    grid_spec = pltpu.PrefetchScalarGridSpec(
        num_scalar_prefetch=1,
        grid=(B, L // tq),
        in_specs=[
            pl.BlockSpec((1, tq, ATTN_WIDTH), cur),
            pl.BlockSpec((1, WINDOW, KV_WIDTH), prv),
            pl.BlockSpec((1, tq, KV_WIDTH), cur),
            pl.BlockSpec((1, WINDOW, KV_WIDTH), nxt),
            pl.BlockSpec((1, KV_WIDTH, WINDOW), prv_t),
            pl.BlockSpec((1, KV_WIDTH, tq), cur_t),
            pl.BlockSpec((1, KV_WIDTH, WINDOW), nxt_t),
            pl.BlockSpec((1, ATTN_WIDTH), lambda b, i, s: (0, 0)),
        ],
        out_specs=pl.BlockSpec((1, tq, ATTN_WIDTH), cur),
        scratch_shapes=[pltpu.VMEM((ATTN_WIDTH, tq), F32)],
    )
    return pl.pallas_call(
        functools.partial(_attn_kernel, L=L),
        grid_spec=grid_spec,
        out_shape=jax.ShapeDtypeStruct((B, L, ATTN_WIDTH), BF16),
        compiler_params=_cparams("parallel", "parallel"),
        name="banded_attention",
    )(sink.astype(F32), q, k, k, k, vt, vt, vt, out_norm_w[None, :])


def _outproj_kernel(x_ref, a_ref, hy_ref, hw_ref, wa_ref, wh_ref, nw_ref, wr_ref, rb_ref,
                    x1_ref, h2_ref, rt_ref):
    nslab = hy_ref.shape[2]
    hy = jnp.concatenate(
        [_gather_cols(lambda c: hy_ref[0, c, j, n1 * SUBLANES:(n1 + 1) * SUBLANES, :], hy_ref.shape[1])
         for n1 in range(hy_ref.shape[3] // SUBLANES) for j in range(nslab)],
        axis=0)
    hinv = lax.rsqrt(jnp.mean(hy * hy, axis=-1, keepdims=True) + EPS)
    hy_n = (hy * hinv * hw_ref[...]).astype(BF16)
    x1 = x_ref[...] + _dot(a_ref[...], wa_ref[...]) + _dot(hy_n, wh_ref[...])
    x1_ref[...] = x1
    inv = lax.rsqrt(jnp.mean(x1 * x1, axis=-1, keepdims=True) + EPS)
    h2 = x1 * inv * nw_ref[...]
    h2_ref[...] = h2.astype(BF16)
    h_hi, h_lo = _split_bf16(h2)
    both = _dot(h_hi, wr_ref[...])
    lg = (both[:, :ROUTER_PAD] + both[:, ROUTER_PAD:] + _dot(h_lo, wr_ref[:, :ROUTER_PAD])
          + rb_ref[...])
    lane = lax.broadcasted_iota(jnp.int32, lg.shape, 1)
    neg = jnp.float32(-jnp.inf)
    first = lambda mask: jnp.min(jnp.where(mask, lane, ROUTER_PAD), axis=-1, keepdims=True)
    gl = jnp.where(lane < N_EXPERT_GROUPS, lg, neg)
    gmax = jnp.max(gl, axis=-1, keepdims=True)
    g_idx = first(gl == gmax)
    g_gate = 1.0 / jnp.sum(jnp.exp(gl - gmax), axis=-1, keepdims=True)
    e_lane = lane - N_EXPERT_GROUPS
    in_grp = (e_lane >= g_idx * EXPERTS_PER_GROUP) & (e_lane < (g_idx + 1) * EXPERTS_PER_GROUP)
    el = jnp.where(in_grp, lg, neg)
    v1 = jnp.max(el, axis=-1, keepdims=True)
    i1 = first(el == v1)
    el2 = jnp.where(lane == i1, neg, el)
    v2 = jnp.max(el2, axis=-1, keepdims=True)
    i2 = first(el2 == v2)
    t = jnp.exp(v2 - v1)
    w1 = g_gate / (1.0 + t)
    w2 = g_gate * t / (1.0 + t)
    rt_ref[...] = jnp.where(lane == 0, (i1 - N_EXPERT_GROUPS).astype(F32),
                            jnp.where(lane == 1, (i2 - N_EXPERT_GROUPS).astype(F32),
                                      jnp.where(lane == 2, w1, jnp.where(lane == 3, w2, 0.0))))


def _out_proj(x2d, attn_n, hy, hy_norm_w, w_out_a, w_out_h, norm2_w, w_router, b_router, tm=256):
    T = x2d.shape[0]
    _, NC, J, LJ, _ = hy.shape
    lt = (LJ * J) // tm
    row = lambda i: (i, 0)
    const = lambda i: (0, 0)
    return pl.pallas_call(
        _outproj_kernel,
        grid=(T // tm,),
        in_specs=[
            pl.BlockSpec((tm, D_MODEL), row),
            pl.BlockSpec((tm, ATTN_WIDTH), row),
            pl.BlockSpec((1, NC, J, tm // J, LANES), lambda i: (i // lt, 0, 0, i % lt, 0)),
            pl.BlockSpec((1, HY_WIDTH), const),
            pl.BlockSpec((ATTN_WIDTH, D_MODEL), const),
            pl.BlockSpec((HY_WIDTH, D_MODEL), const),
            pl.BlockSpec((1, D_MODEL), const),
            pl.BlockSpec((D_MODEL, 2 * ROUTER_PAD), const),
            pl.BlockSpec((1, ROUTER_PAD), const),
        ],
        out_specs=[
            pl.BlockSpec((tm, D_MODEL), row),
            pl.BlockSpec((tm, D_MODEL), row),
            pl.BlockSpec((tm, ROUTER_PAD), row),
        ],
        out_shape=[
            jax.ShapeDtypeStruct((T, D_MODEL), F32),
            jax.ShapeDtypeStruct((T, D_MODEL), BF16),
            jax.ShapeDtypeStruct((T, ROUTER_PAD), F32),
        ],
        compiler_params=_cparams("parallel"),
        name="out_proj",
    )(x2d, attn_n, hy, hy_norm_w[None, :], w_out_a, w_out_h, norm2_w[None, :], w_router, b_router)


def _cast_kernel(*refs):
    n = len(refs) // 2
    for x_ref, o_ref in zip(refs[:n], refs[n:]):
        o_ref[...] = x_ref[...].astype(o_ref.dtype)


def _to_bf16(ws, parts=4):
    E = ws[0].shape[0]
    specs = [pl.BlockSpec((1, w.shape[1] // parts, w.shape[2]), lambda e, i: (e, i, 0)) for w in ws]
    return pl.pallas_call(
        _cast_kernel,
        grid=(E, parts),
        in_specs=specs,
        out_specs=specs,
        out_shape=[jax.ShapeDtypeStruct(w.shape, BF16) for w in ws],
        compiler_params=_cparams("parallel", "parallel"),
        name="expert_weights_to_bf16",
    )(*ws)


def _moe_kernel(be_ref, nu_ref, x_ref, wg_ref, wu_ref, wd_ref, o_ref):
    @pl.when(pl.program_id(0) < nu_ref[0])
    def _():
        x = x_ref[...]
        g = _dot(x, wg_ref[0])
        u = _dot(x, wu_ref[0])
        hid = (g * jax.nn.sigmoid(g)) * u
        o_ref[...] = _dot(hid.astype(BF16), wd_ref[0]).astype(o_ref.dtype)


def _expert_blocks(xs, block_expert, n_used, w_gate, w_up, w_down):
    P = xs.shape[0]
    bm = MOE_BLOCK_ROWS
    blk = lambda b, be, nu: (jnp.minimum(b, nu[0] - 1), 0)
    wsel = lambda b, be, nu: (be[jnp.minimum(b, nu[0] - 1)], 0, 0)
    grid_spec = pltpu.PrefetchScalarGridSpec(
        num_scalar_prefetch=2,
        grid=(P // bm,),
        in_specs=[
            pl.BlockSpec((bm, D_MODEL), blk),
            pl.BlockSpec((1, D_MODEL, D_EXPERT), wsel),
            pl.BlockSpec((1, D_MODEL, D_EXPERT), wsel),
            pl.BlockSpec((1, D_EXPERT, D_MODEL), wsel),
        ],
        out_specs=pl.BlockSpec((bm, D_MODEL), blk),
    )
    return pl.pallas_call(
        _moe_kernel,
        grid_spec=grid_spec,
        out_shape=jax.ShapeDtypeStruct((P, D_MODEL), BF16),
        compiler_params=_cparams("arbitrary"),
        name="moe_experts",
    )(block_expert, n_used, xs, w_gate, w_up, w_down)


def _hier_moe(x1, h2, route, w_gate, w_up, w_down):
    T = h2.shape[0]
    experts = route[:, :TOP_K].astype(jnp.int32)
    weights = route[:, TOP_K:2 * TOP_K]

    A = T * TOP_K
    bm = MOE_BLOCK_ROWS
    flat_e = experts.reshape(-1)
    onehot = (flat_e[:, None] == jnp.arange(N_EXPERTS, dtype=jnp.int32)[None, :]).astype(jnp.int32)
    csum = jnp.cumsum(onehot, axis=0)
    counts = csum[-1]
    rank = jnp.sum((csum - onehot) * onehot, axis=1)
    padded = ((counts + bm - 1) // bm) * bm
    pend = jnp.cumsum(padded)
    pstart = pend - padded
    seg_start = jnp.cumsum(counts) - counts
    dest = (pstart[flat_e] + rank).astype(jnp.int32)
    n_blocks = (A + bm - 1) // bm + N_EXPERTS
    P = n_blocks * bm
    order = jnp.argsort(flat_e, stable=True).astype(jnp.int32)
    q = jnp.arange(P, dtype=jnp.int32)
    e_q = jnp.minimum(jnp.sum((pend[None, :] <= q[:, None]).astype(jnp.int32), axis=1), N_EXPERTS - 1)
    r_q = jnp.minimum(q - pstart[e_q], jnp.maximum(counts[e_q] - 1, 0))
    src = jnp.clip(seg_start[e_q] + r_q, 0, A - 1)
    buf_tok = order.at[src].get(mode="promise_in_bounds") // TOP_K
    block_expert = e_q.reshape(n_blocks, bm)[:, 0]
    n_used = (pend[-1] // bm).astype(jnp.int32)[None]

    xs = h2.at[buf_tok].get(mode="promise_in_bounds")
    yb = _expert_blocks(xs, block_expert, n_used, w_gate, w_up, w_down)
    d = dest.reshape(T, TOP_K)
    y0 = yb.at[d[:, 0]].get(mode="promise_in_bounds").astype(F32)
    y1 = yb.at[d[:, 1]].get(mode="promise_in_bounds").astype(F32)
    return x1 + (weights[:, 0:1] * y0 + weights[:, 1:2] * y1)


def _encoder_layer(x, p):
    B, L, D = x.shape
    T = B * L
    x2d = x.reshape(T, D)
    q, k, v = _qkv_proj(x2d, L, p["norm1_w"], p["w_qkv"], p["q_norm_w"], p["k_norm_w"])
    attn_n = _windowed_attention(q.reshape(B, L, ATTN_WIDTH), k.reshape(B, L, KV_WIDTH),
                                 v, p["attn_sink"], p["attn_out_norm_w"])
    uc = _u_proj_conv(x2d, L, p["norm1_w"], p["w_u"], p["conv_w"], p["conv_b"])
    tabs = _fft_tables(L)
    g_spec = _hyena_filter_spectrum(L, tabs, p["filt_w1"], p["filt_b1"], p["filt_w2"], p["filt_b2"],
                                    p["filt_w3"], p["filt_b3"], p["filt_w4"], p["filt_freq"])
    hy = _hyena_mixer(uc, g_spec, p["hy_bias"], tabs)
    x1, h2, route = _out_proj(x2d, attn_n.reshape(T, ATTN_WIDTH), hy, p["hy_out_norm_w"],
                              p["w_out_a"], p["w_out_h"], p["norm2_w"], p["w_router"], p["b_router"])
    out = _hier_moe(x1, h2, route, p["w_gate"], p["w_up"], p["w_down"])
    return out.reshape(B, L, D)


def kernel(x_prompt, x_sample, norm1_w, w_in, q_norm_w, k_norm_w, attn_sink, conv_w, conv_b, filt_w1, filt_b1, filt_w2, filt_b2, filt_w3, filt_b3, filt_w4, filt_freq, hy_bias, attn_out_norm_w, hy_out_norm_w, w_out, norm2_w, w_route_group, b_route_group, w_route_expert, b_route_expert, w_gate, w_up, w_down):
    depth = norm1_w.shape[0]

    def layer_params(l):
        w_r = jnp.zeros((D_MODEL, ROUTER_PAD), F32)
        w_r = w_r.at[:, :N_EXPERT_GROUPS].set(w_route_group[l])
        w_r = w_r.at[:, N_EXPERT_GROUPS:N_EXPERT_GROUPS + N_EXPERTS].set(w_route_expert[l])
        r_hi, r_lo = _split_bf16(w_r)
        b_r = jnp.zeros((1, ROUTER_PAD), F32)
        b_r = b_r.at[0, :N_EXPERT_GROUPS].set(b_route_group[l])
        b_r = b_r.at[0, N_EXPERT_GROUPS:N_EXPERT_GROUPS + N_EXPERTS].set(b_route_expert[l])
        wg_bf, wu_bf, wd_bf = _to_bf16([w_gate[l], w_up[l], w_down[l]])
        return dict(
            norm1_w=norm1_w[l], w_qkv=w_in[l][:, :QKV_WIDTH].astype(BF16),
            w_u=w_in[l][:, QKV_WIDTH:].astype(BF16), q_norm_w=q_norm_w[l], k_norm_w=k_norm_w[l],
            attn_sink=attn_sink[l], conv_w=conv_w[l], conv_b=conv_b[l],
            filt_w1=filt_w1[l], filt_b1=filt_b1[l], filt_w2=filt_w2[l], filt_b2=filt_b2[l],
            filt_w3=filt_w3[l], filt_b3=filt_b3[l], filt_w4=filt_w4[l], filt_freq=filt_freq[l],
            hy_bias=hy_bias[l], attn_out_norm_w=attn_out_norm_w[l], hy_out_norm_w=hy_out_norm_w[l],
            w_out_a=w_out[l][:ATTN_WIDTH].astype(BF16), w_out_h=w_out[l][ATTN_WIDTH:].astype(BF16),
            norm2_w=norm2_w[l], w_router=jnp.concatenate([r_hi, r_lo], axis=1), b_router=b_r,
            w_gate=wg_bf, w_up=wu_bf, w_down=wd_bf)

    params = [layer_params(l) for l in range(depth)]

    def trunk(x):
        for p in params:
            x = _encoder_layer(x, p)
        return x

    return (trunk(x_prompt), trunk(x_sample))
```

```python
import functools
import math

import jax
import jax.numpy as jnp
from jax import lax
from jax.experimental import pallas as pl
from jax.experimental.pallas import tpu as pltpu

F32 = jnp.float32
BF16 = jnp.bfloat16

D_MODEL = 2048
HEAD_DIM = 64
N_Q_HEADS = 16
N_KV_HEADS = 4
Q_PER_KV = N_Q_HEADS // N_KV_HEADS
ATTN_WIDTH = N_Q_HEADS * HEAD_DIM
KV_WIDTH = N_KV_HEADS * HEAD_DIM
QK_WIDTH = ATTN_WIDTH + KV_WIDTH
QKV_WIDTH = ATTN_WIDTH + 2 * KV_WIDTH
WINDOW = 128
ROT_DIM = HEAD_DIM // 4
ROPE_THETA = 500000.0
HY_WIDTH = D_MODEL - ATTN_WIDTH
HY_ORDER = 2
HY_EMB_DIM = 33
HY_FILTER_HIDDEN = 64
HY_DECAY_TARGET = 1e-2
HY_FAST_DECAY_PCT = 0.3
HY_SLOW_DECAY_PCT = 1.5
N_EXPERT_GROUPS = 4
EXPERTS_PER_GROUP = 8
N_EXPERTS = N_EXPERT_GROUPS * EXPERTS_PER_GROUP
TOP_K = 2
D_EXPERT = 1024
EPS = 1e-6

LANES = 128
SUBLANES = 8
FFT_N2 = 128
MOE_BLOCK_ROWS = 256
ROUTER_PAD = 128
VMEM_LIMIT = 56 * 1024 * 1024
NEG_BIG = -1e30
HEADS_PER_PASS = Q_PER_KV


def _cparams(*sem):
    return pltpu.CompilerParams(dimension_semantics=sem, vmem_limit_bytes=VMEM_LIMIT)


def _split_bf16(a):
    hi = a.astype(BF16)
    lo = (a - hi.astype(F32)).astype(BF16)
    return hi, lo


def _dot(a, b):
    return jnp.dot(a, b, preferred_element_type=F32)


def _dot3(a, b):
    a_hi, a_lo = _split_bf16(a)
    b_hi, b_lo = _split_bf16(b)
    return _dot(a_hi, b_hi) + _dot(a_lo, b_hi) + _dot(a_hi, b_lo)


def _qkv_kernel(x_ref, nw_ref, w_ref, seg_ref, hw_ref, rc_ref, rs1_ref, rs2_ref,
                q_ref, k_ref, v_ref):
    x = x_ref[...]
    inv = lax.rsqrt(jnp.mean(x * x, axis=-1, keepdims=True) + EPS)
    h = (x * inv * nw_ref[...]).astype(BF16)
    acc = _dot(h, w_ref[...])
    qk = acc[:, :QK_WIDTH]
    hi, lo = _split_bf16(qk * qk)
    seg = seg_ref[...]
    w = seg.shape[0]
    ms = jnp.concatenate(
        [_dot(hi[:, c * w:(c + 1) * w], seg) + _dot(lo[:, c * w:(c + 1) * w], seg)
         for c in range(QK_WIDTH // w)], axis=-1)
    xn = qk * lax.rsqrt(ms + EPS) * hw_ref[...]
    reps = QK_WIDTH // LANES
    rc = jnp.tile(rc_ref[...], (1, reps))
    rs1 = jnp.tile(rs1_ref[...], (1, reps))
    rs2 = jnp.tile(rs2_ref[...], (1, reps))
    half = ROT_DIM // 2
    y = xn * rc + pltpu.roll(xn, QK_WIDTH - half, 1) * rs1 + pltpu.roll(xn, half, 1) * rs2
    q_ref[...] = (y[:, :ATTN_WIDTH] * (HEAD_DIM ** -0.5)).astype(BF16)
    k_ref[...] = y[:, ATTN_WIDTH:].astype(BF16)
    v_ref[0] = acc[:, QK_WIDTH:].T.astype(BF16)


def _rope_tables(L):
    half = ROT_DIM // 2
    inv_freq = jnp.power(ROPE_THETA, -jnp.arange(half, dtype=F32) * 2.0 / ROT_DIM)
    ang = jnp.arange(L, dtype=F32)[:, None] * inv_freq[None, :]
    d = jnp.arange(LANES) % HEAD_DIM
    ang_l = ang[:, d % half]
    cos, sin = jnp.cos(ang_l), jnp.sin(ang_l)
    rc = jnp.where(d < ROT_DIM, cos, 1.0)
    rs1 = jnp.where(d < half, -sin, 0.0)
    rs2 = jnp.where((d >= half) & (d < ROT_DIM), sin, 0.0)
    return rc.astype(F32), rs1.astype(F32), rs2.astype(F32)


def _qkv_proj(x2d, L, norm1_w, w_qkv, q_norm_w, k_norm_w, tm=512):
    T = x2d.shape[0]
    seg_w = 256
    seg = (jnp.arange(seg_w)[:, None] // HEAD_DIM == jnp.arange(seg_w)[None, :] // HEAD_DIM)
    seg = (seg.astype(F32) / HEAD_DIM).astype(BF16)
    hw = jnp.concatenate([jnp.tile(q_norm_w, N_Q_HEADS), jnp.tile(k_norm_w, N_KV_HEADS)])[None, :]
    rc, rs1, rs2 = _rope_tables(L)
    lt = L // tm
    row = lambda i: (i, 0)
    const = lambda i: (0, 0)
    pos = lambda i: (i % lt, 0)
    return pl.pallas_call(
        _qkv_kernel,
        grid=(T // tm,),
        in_specs=[
            pl.BlockSpec((tm, D_MODEL), row),
            pl.BlockSpec((1, D_MODEL), const),
            pl.BlockSpec((D_MODEL, QKV_WIDTH), const),
            pl.BlockSpec((seg_w, seg_w), const),
            pl.BlockSpec((1, QK_WIDTH), const),
            pl.BlockSpec((tm, LANES), pos),
            pl.BlockSpec((tm, LANES), pos),
            pl.BlockSpec((tm, LANES), pos),
        ],
        out_specs=[
            pl.BlockSpec((tm, ATTN_WIDTH), row),
            pl.BlockSpec((tm, KV_WIDTH), row),
            pl.BlockSpec((1, KV_WIDTH, tm), lambda i: (i // lt, 0, i % lt)),
        ],
        out_shape=[
            jax.ShapeDtypeStruct((T, ATTN_WIDTH), BF16),
            jax.ShapeDtypeStruct((T, KV_WIDTH), BF16),
            jax.ShapeDtypeStruct((T // L, KV_WIDTH, L), BF16),
        ],
        compiler_params=_cparams("parallel"),
        name="qkv_proj",
    )(x2d, norm1_w[None, :], w_qkv, seg, hw, rc, rs1, rs2)


HALO = 16

def _uproj_kernel(x_ref, xp_ref, xn_ref, nw_ref, w_ref, cw_ref, cb_ref, o_ref, h_scr, *, lt):
    i = pl.program_id(0)
    tm = x_ref.shape[0]

    @pl.when(pl.program_id(1) == 0)
    def _():
        def normed(x):
            inv = lax.rsqrt(jnp.mean(x * x, axis=-1, keepdims=True) + EPS)
            return x * inv * nw_ref[...]

        li = i % lt
        hp = jnp.where(li > 0, normed(xp_ref[...]), 0.0)
        hn = jnp.where(li < lt - 1, normed(xn_ref[...]), 0.0)
        h_scr[0:HALO] = hp.astype(BF16)
        h_scr[HALO:HALO + tm] = normed(x_ref[...]).astype(BF16)
        h_scr[HALO + tm:2 * HALO + tm] = hn.astype(BF16)

    u = _dot(h_scr[...], w_ref[...])
    w = cw_ref[...]
    val = (w[0:1] * u[HALO - 1:HALO - 1 + tm] + w[1:2] * u[HALO:HALO + tm]
           + w[2:3] * u[HALO + 1:HALO + 1 + tm] + cb_ref[...])
    for n1 in range(tm // FFT_N2):
        for j in range(FFT_N2 // SUBLANES):
            r0 = n1 * FFT_N2 + j * SUBLANES
            for c in range(val.shape[1] // LANES):
                o_ref[0, 0, c, j, n1 * SUBLANES:(n1 + 1) * SUBLANES, :] = (
                    val[r0:r0 + SUBLANES, c * LANES:(c + 1) * LANES])


def _u_proj_conv(x2d, L, norm1_w, w_u, conv_w, conv_b, tm=1024):
    T = x2d.shape[0]
    C = HY_WIDTH
    nparts = w_u.shape[1] // C
    lt = L // tm
    hb = tm // HALO
    J = FFT_N2 // SUBLANES
    return pl.pallas_call(
        functools.partial(_uproj_kernel, lt=lt),
        grid=(T // tm, nparts),
        in_specs=[
            pl.BlockSpec((tm, D_MODEL), lambda i, j: (i, 0)),
            pl.BlockSpec((HALO, D_MODEL), lambda i, j: (jnp.maximum(i * hb - 1, 0), 0)),
            pl.BlockSpec((HALO, D_MODEL), lambda i, j: (jnp.minimum((i + 1) * hb, T // HALO - 1), 0)),
            pl.BlockSpec((1, D_MODEL), lambda i, j: (0, 0)),
            pl.BlockSpec((D_MODEL, C), lambda i, j: (0, j)),
            pl.BlockSpec((3, C), lambda i, j: (0, j)),
            pl.BlockSpec((1, C), lambda i, j: (0, j)),
        ],
        out_specs=pl.BlockSpec((1, 1, C // LANES, J, tm // J, LANES),
                               lambda i, j: (j, i // lt, 0, 0, i % lt, 0)),
        out_shape=jax.ShapeDtypeStruct((nparts, T // L, C // LANES, J, L // J, LANES), F32),
        scratch_shapes=[pltpu.VMEM((tm + 2 * HALO, D_MODEL), BF16)],
        compiler_params=_cparams("parallel", "arbitrary"),
        name="u_proj_conv",
    )(x2d, x2d, x2d, norm1_w[None, :], w_u, conv_w, conv_b[None, :])


def _fft_tables(L):
    N = 2 * L
    N2 = FFT_N2
    N1 = N // N2
    H = N1 // 2
    k1 = jnp.arange(N1, dtype=jnp.int32)
    n2 = jnp.arange(N2, dtype=jnp.int32)
    a1 = ((k1[:, None] * k1[None, :]) % N1).astype(F32) * (2.0 * math.pi / N1)
    c1, s1 = jnp.cos(a1), jnp.sin(a1)
    at = (n2[:, None] * k1[None, :]).astype(F32) * (2.0 * math.pi / N)
    ct, st = jnp.cos(at), jnp.sin(at)
    c = c1[None] * ct[:, :, None] - s1[None] * st[:, :, None]
    s = s1[None] * ct[:, :, None] + c1[None] * st[:, :, None]
    m_filt = jnp.concatenate([c, -s], axis=1).astype(BF16)
    ch, sh = c[:, :, :H], s[:, :, :H]
    m_fwd = jnp.concatenate([jnp.concatenate([ch, sh], axis=2),
                             jnp.concatenate([-sh, ch], axis=2)], axis=1)
    cT = c1[None, :H] * ct[:, None, :] - s1[None, :H] * st[:, None, :]
    sT = s1[None, :H] * ct[:, None, :] + c1[None, :H] * st[:, None, :]
    m_inv = jnp.concatenate([jnp.concatenate([cT, -sT], axis=2),
                             jnp.concatenate([sT, cT], axis=2)], axis=1)
    k2 = jnp.arange(N2, dtype=jnp.int32)
    ph = ((k2[:, None] * n2[None, :]) % N2).astype(F32) * (2.0 * math.pi / N2)
    c2, s2 = jnp.cos(ph), jnp.sin(ph)
    m2_fwd = jnp.concatenate([jnp.concatenate([c2, s2], axis=1),
                              jnp.concatenate([-s2, c2], axis=1)], axis=0)
    m2_inv = m2_fwd.T
    return dict(N1=N1, N2=N2, m_filt=m_filt, m_fwd=m_fwd.astype(BF16), m_inv=m_inv.astype(BF16),
                m2_fwd=m2_fwd.astype(BF16), m2_inv=m2_inv.astype(BF16))


def _filt1_kernel(w1_ref, b1_ref, w2_ref, b2_ref, w3_ref, b3_ref, w4_ref, a_ref, frl_ref, off_ref,
                  dl_ref, m_ref, o_ref, sabs_ref, *, L, N1, N2, nb):
    j = pl.program_id(0)
    C2 = HY_ORDER * HY_WIDTH
    H = N1 // 2
    rows = nb * N1
    ridx = lax.broadcasted_iota(jnp.int32, (rows, LANES), 0)
    lane = lax.broadcasted_iota(jnp.int32, (rows, LANES), 1)
    n2 = j * nb + ridx // N1
    n1 = ridx % N1
    r = n1 * N2 + n2
    lag = jnp.where(r < L, r, 2 * L - r)
    valid = (r != L).astype(F32)
    lagf = jnp.minimum(lag, L - 1).astype(F32)
    t = lagf / (L - 1)
    wpos = lagf * (2.0 * math.pi / L)
    phase = wpos * frl_ref[...] + off_ref[...]
    zemb = jnp.where(lane == 0, t, jnp.where(lane < HY_EMB_DIM, jnp.cos(phase), 0.0))
    half = rows // 2
    a = a_ref[...]
    h = jnp.concatenate([zemb[:half], zemb[half:]], axis=1)
    h = jnp.sin(a * (_dot3(h, w1_ref[...]) + b1_ref[...]))
    h = jnp.sin(a * (_dot3(h, w2_ref[...]) + b2_ref[...]))
    h = jnp.sin(a * (_dot3(h, w3_ref[...]) + b3_ref[...]))
    h_hi, h_lo = _split_bf16(h)
    decay = jnp.exp(-jnp.tile(t, (1, HY_WIDTH // LANES)) * dl_ref[...])
    decay = decay * jnp.tile(valid, (1, HY_WIDTH // LANES))
    decay2 = jnp.tile(decay, (1, HY_ORDER))

    @pl.when(j == 0)
    def _():
        sabs_ref[...] = jnp.zeros_like(sabs_ref)

    tot = jnp.zeros((8, C2), F32)
    for l in range(nb):
        side = (l * N1) // half
        r0 = l * N1 - side * half
        fs = slice(r0, r0 + H)
        bs = slice(r0 + H, r0 + N1)
        wf = w4_ref[side, :, :C2]
        wb = w4_ref[side, :, C2:]
        gf = _dot(h_hi[fs], wf) + _dot(h_lo[fs], wf)
        gb = _dot(h_hi[bs], wb) + _dot(h_lo[bs], wb)
        g = jnp.concatenate([gf, gb], axis=0) * decay2[l * N1:(l + 1) * N1]
        tot = tot + jnp.sum(jnp.abs(g).reshape(N1 // 8, 8, C2), axis=0)
        y = _dot(m_ref[l], g.astype(BF16))
        _store_tiles(lambda c: o_ref.at[c], _pack_complex(y[:N1], y[N1:]), C2 // LANES, l)
    sabs_ref[...] += tot


def _pack_complex(re, im):
    r = lax.bitcast_convert_type(re.astype(BF16).astype(F32), jnp.uint32)
    i = lax.bitcast_convert_type(im.astype(BF16).astype(F32), jnp.uint32)
    return r | (i >> 16)


def _unpack_complex(w):
    re = lax.bitcast_convert_type(w & jnp.uint32(0xFFFF0000), F32)
    im = lax.bitcast_convert_type(w << 16, F32)
    return re, im


def _store_tiles(dst, w, ncols, s):
    for c in range(ncols):
        d = dst(c)
        for t in range(w.shape[0] // SUBLANES):
            d[t, s * SUBLANES:(s + 1) * SUBLANES, :] = (
                w[t * SUBLANES:(t + 1) * SUBLANES, c * LANES:(c + 1) * LANES])


def _rows_strided(ref2d, s):
    return ref2d[pl.ds(s, ref2d.shape[0] // SUBLANES, stride=SUBLANES), :]


def _gather_cols(fn, ncols):
    return jnp.concatenate([fn(c) for c in range(ncols)], axis=1)


def _block_diag2(w):
    z = jnp.zeros_like(w)
    return jnp.concatenate([jnp.concatenate([w, z], axis=1), jnp.concatenate([z, w], axis=1)], axis=0)


def _hyena_filter_spectrum(L, tabs, filt_w1, filt_b1, filt_w2, filt_b2, filt_w3, filt_b3,
                           filt_w4, filt_freq):
    N1, N2 = tabs["N1"], tabs["N2"]
    N = N1 * N2
    C2 = HY_ORDER * HY_WIDTH
    FH = HY_FILTER_HIDDEN
    nb = max(2, 512 // N1)
    bands = (HY_EMB_DIM - 1) // 2
    fr = jnp.linspace(1e-4, bands - 1, bands, dtype=F32)
    frl = jnp.zeros((LANES,), F32).at[1:1 + bands].set(fr).at[1 + bands:1 + 2 * bands].set(fr)[None, :]
    off = jnp.zeros((LANES,), F32).at[1 + bands:1 + 2 * bands].set(0.5 * math.pi)[None, :]
    w1p = jnp.zeros((LANES, FH), F32).at[:HY_EMB_DIM].set(filt_w1)
    w4 = filt_w4.astype(BF16)
    w4z = jnp.zeros_like(w4)
    w4s = jnp.stack([jnp.concatenate([w4, w4z], axis=0), jnp.concatenate([w4z, w4], axis=0)])
    two = lambda v: jnp.tile(v, 2)[None, :]
    max_decay = math.log(HY_DECAY_TARGET) / HY_FAST_DECAY_PCT
    min_decay = math.log(HY_DECAY_TARGET) / HY_SLOW_DECAY_PCT
    deltas = jnp.abs(jnp.linspace(min_decay, max_decay, HY_WIDTH, dtype=F32))[None, :]
    full = lambda shape: pl.BlockSpec(shape, lambda j: (0,) * len(shape))
    a_g, sabs = pl.pallas_call(
        functools.partial(_filt1_kernel, L=L, N1=N1, N2=N2, nb=nb),
        grid=(N2 // nb,),
        in_specs=[
            full((2 * LANES, 2 * FH)), full((1, 2 * FH)),
            full((2 * FH, 2 * FH)), full((1, 2 * FH)),
            full((2 * FH, 2 * FH)), full((1, 2 * FH)),
            full((2, 2 * FH, 2 * C2)), full((1, 2 * FH)),
            full((1, LANES)), full((1, LANES)), full((1, HY_WIDTH)),
            pl.BlockSpec((nb, 2 * N1, N1), lambda j: (j, 0, 0)),
        ],
        out_specs=[
            pl.BlockSpec((C2 // LANES, N1 // SUBLANES, nb * SUBLANES, LANES), lambda j: (0, 0, j, 0)),
            pl.BlockSpec((8, C2), lambda j: (0, 0)),
        ],
        out_shape=[
            jax.ShapeDtypeStruct((C2 // LANES, N1 // SUBLANES, N2 * SUBLANES, LANES), jnp.uint32),
            jax.ShapeDtypeStruct((8, C2), F32),
        ],
        compiler_params=_cparams("arbitrary"),
        name="hyena_filter_stage1",
    )(_block_diag2(w1p), two(filt_b1), _block_diag2(filt_w2), two(filt_b2), _block_diag2(filt_w3),
      two(filt_b3), w4s, two(filt_freq), frl, off, deltas, tabs["m_filt"])
    scale = (1.0 / (jnp.sum(sabs, axis=0) * N))[None, :]
    return a_g, scale


def _fwd1_kernel(x_ref, m_ref, o_ref):
    cb = o_ref.shape[1]
    N1 = m_ref.shape[2]
    for s in range(SUBLANES):
        rows = [_gather_cols(lambda c: _rows_strided(x_ref.at[0, bi, c, 0], s), cb) for bi in range(2)]
        xl = jnp.concatenate(rows, axis=0).astype(BF16)
        y = _dot(m_ref[s], xl)
        _store_tiles(lambda c: o_ref.at[0, c], _pack_complex(y[:N1], y[N1:]), cb, s)


def _hyena_stage1(z, which, tabs):
    N1, N2 = tabs["N1"], tabs["N2"]
    H = N1 // 2
    B, NC = z.shape[1], z.shape[2]
    cb = min(NC, 4)
    x_spec = pl.BlockSpec((1, 2, cb, 1, H * SUBLANES, LANES), lambda p, c, j: (which, p, c, j, 0, 0))
    return pl.pallas_call(
        _fwd1_kernel,
        grid=(B // 2, NC // cb, N2 // SUBLANES),
        in_specs=[x_spec, pl.BlockSpec((SUBLANES, 2 * N1, N1), lambda p, c, j: (j, 0, 0))],
        out_specs=pl.BlockSpec((1, cb, N1 // SUBLANES, SUBLANES * SUBLANES, LANES),
                               lambda p, c, j: (p, c, 0, j, 0)),
        out_shape=jax.ShapeDtypeStruct((B // 2, NC, N1 // SUBLANES, N2 * SUBLANES, LANES), jnp.uint32),
        compiler_params=_cparams("parallel", "parallel", "parallel"),
        name="hyena_stage1",
    )(z, tabs["m_fwd"])


def _mid_kernel(a_ref, ag_ref, sc_ref, mf_ref, mi_ref, o_ref, g_scr):
    cb = a_ref.shape[1]
    N2 = a_ref.shape[3] // SUBLANES

    @pl.when(pl.program_id(2) == 0)
    def _():
        for s in range(SUBLANES):
            re, im = _unpack_complex(_gather_cols(lambda c: _rows_strided(ag_ref.at[c, 0], s), cb))
            ag = jnp.concatenate([re, im], axis=0).astype(BF16)
            g_scr[s] = _dot(mf_ref[...], ag) * sc_ref[...]

    for s in range(SUBLANES):
        re, im = _unpack_complex(_gather_cols(lambda c: _rows_strided(a_ref.at[0, c, 0], s), cb))
        a = jnp.concatenate([re, im], axis=0).astype(BF16)
        x = _dot(mf_ref[...], a)
        xr, xi = x[:N2], x[N2:]
        gr, gi = g_scr[s, :N2], g_scr[s, N2:]
        y = jnp.concatenate([xr * gr - xi * gi, xr * gi + xi * gr], axis=0).astype(BF16)
        b = _dot(mi_ref[...], y)
        _store_tiles(lambda c: o_ref.at[0, c], _pack_complex(b[:N2], b[N2:]), cb, s)


def _hyena_stage2(a, a_g, scale, order, tabs, cb=8):
    N1, N2 = tabs["N1"], tabs["N2"]
    P, NC = a.shape[:2]
    cb = min(cb, NC)
    ncb = NC // cb
    return pl.pallas_call(
        _mid_kernel,
        grid=(N1 // SUBLANES, ncb, P),
        in_specs=[
            pl.BlockSpec((1, cb, 1, N2 * SUBLANES, LANES), lambda i, c, p: (p, c, i, 0, 0)),
            pl.BlockSpec((cb, 1, N2 * SUBLANES, LANES), lambda i, c, p: (order * ncb + c, i, 0, 0)),
            pl.BlockSpec((1, cb * LANES), lambda i, c, p: (0, order * ncb + c)),
            pl.BlockSpec((2 * N2, 2 * N2), lambda i, c, p: (0, 0)),
            pl.BlockSpec((2 * N2, 2 * N2), lambda i, c, p: (0, 0)),
        ],
        out_specs=pl.BlockSpec((1, cb, N2 // SUBLANES, SUBLANES * SUBLANES, LANES),
                               lambda i, c, p: (p, c, 0, i, 0)),
        out_shape=jax.ShapeDtypeStruct((P, NC, N2 // SUBLANES, N1 * SUBLANES, LANES), jnp.uint32),
        scratch_shapes=[pltpu.VMEM((SUBLANES, 2 * N2, cb * LANES), F32)],
        compiler_params=_cparams("parallel", "parallel", "arbitrary"),
        name="hyena_stage2",
    )(a, a_g, scale, tabs["m2_fwd"], tabs["m2_inv"])


def _inv1_kernel(b_ref, m_ref, gate_ref, z_ref, bias_ref, *rest, z_slabbed, final):
    if final:
        (o_ref,) = rest
    else:
        mf_ref, o_ref, a_ref = rest
    cb = b_ref.shape[1]
    N1 = m_ref.shape[1]
    H = N1 // 2
    bias = bias_ref[...]
    for s in range(SUBLANES):
        br, bi_ = _unpack_complex(_gather_cols(lambda c: _rows_strided(b_ref.at[0, c, 0], s), cb))
        b = jnp.concatenate([br, bi_], axis=0).astype(BF16)
        conv = _dot(m_ref[s], b)
        outs = []
        for bi in range(2):
            gate = _gather_cols(lambda c: _rows_strided(gate_ref.at[0, bi, c, 0], s), cb)
            if z_slabbed:
                z = _gather_cols(lambda c: _rows_strided(z_ref.at[0, bi, c, 0], s), cb)
            else:
                z = _gather_cols(lambda c: z_ref[bi, c, s], cb)
            out = gate * (conv[bi * H:(bi + 1) * H] + z * bias)
            outs.append(out)
            for c in range(cb):
                piece = out[:, c * LANES:(c + 1) * LANES]
                if final:
                    o_ref.at[bi, c, 0][pl.ds(s, H, stride=SUBLANES), :] = piece
                else:
                    o_ref[bi, c, s] = piece
        if not final:
            xl = jnp.concatenate(outs, axis=0).astype(BF16)
            y = _dot(mf_ref[s], xl)
            _store_tiles(lambda c: a_ref.at[0, c], _pack_complex(y[:N1], y[N1:]), cb, s)


def _hyena_stage3(bq, ucl, gate_idx, z, z_idx, bias, tabs, final):
    N1, N2 = tabs["N1"], tabs["N2"]
    H = N1 // 2
    P, NC = bq.shape[:2]
    cb = min(NC, 4 if final else max(1, 512 // N1))
    z_slabbed = z_idx is not None
    slab_blk = (1, 2, cb, 1, H * SUBLANES, LANES)
    n2m_blk = (2, cb, SUBLANES, H, LANES)
    if z_slabbed:
        z_spec = pl.BlockSpec(slab_blk, lambda p, c, j: (z_idx, p, c, j, 0, 0))
    else:
        z_spec = pl.BlockSpec(n2m_blk, lambda p, c, j: (p, c, j, 0, 0))
    in_specs = [
        pl.BlockSpec((1, cb, 1, N1 * SUBLANES, LANES), lambda p, c, j: (p, c, j, 0, 0)),
        pl.BlockSpec((SUBLANES, N1, 2 * N1), lambda p, c, j: (j, 0, 0)),
        pl.BlockSpec(slab_blk, lambda p, c, j: (gate_idx, p, c, j, 0, 0)),
        z_spec,
        pl.BlockSpec((1, cb * LANES), lambda p, c, j: (0, c)),
    ]
    args = [bq, tabs["m_inv"], ucl, z, bias[None, :]]
    if final:
        out_specs = pl.BlockSpec((2, cb, 1, H * SUBLANES, LANES), lambda p, c, j: (p, c, j, 0, 0))
        out_shape = jax.ShapeDtypeStruct((2 * P, NC, N2 // SUBLANES, H * SUBLANES, LANES), F32)
    else:
        in_specs.append(pl.BlockSpec((SUBLANES, 2 * N1, N1), lambda p, c, j: (j, 0, 0)))
        args.append(tabs["m_fwd"])
        out_specs = [
            pl.BlockSpec(n2m_blk, lambda p, c, j: (p, c, j, 0, 0)),
            pl.BlockSpec((1, cb, N1 // SUBLANES, SUBLANES * SUBLANES, LANES),
                         lambda p, c, j: (p, c, 0, j, 0)),
        ]
        out_shape = [
            jax.ShapeDtypeStruct((2 * P, NC, N2, H, LANES), F32),
            jax.ShapeDtypeStruct((P, NC, N1 // SUBLANES, N2 * SUBLANES, LANES), jnp.uint32),
        ]
    return pl.pallas_call(
        functools.partial(_inv1_kernel, z_slabbed=z_slabbed, final=final),
        grid=(P, NC // cb, N2 // SUBLANES),
        in_specs=in_specs,
        out_specs=out_specs,
        out_shape=out_shape,
        compiler_params=_cparams("parallel", "parallel", "parallel"),
        name="hyena_stage3",
    )(*args)


def _hyena_mixer(ucl, g_spec, hy_bias, tabs):
    a_g, scale = g_spec
    z, z_idx = ucl, 2
    a = _hyena_stage1(z, z_idx, tabs)
    for order in range(HY_ORDER):
        bq = _hyena_stage2(a, a_g, scale, order, tabs)
        if order == HY_ORDER - 1:
            return _hyena_stage3(bq, ucl, order, z, z_idx, hy_bias[order], tabs, final=True)
        z, a = _hyena_stage3(bq, ucl, order, z, z_idx, hy_bias[order], tabs, final=False)
        z_idx = None


def _ride_along_specs(riders, steps, index):
    specs = []
    for r in riders:
        rows = r.shape[0] // steps
        specs.append(pl.BlockSpec((rows, r.shape[1]), lambda *g, _i=index: (_i(*g), 0)))
    return specs


def _ride_along_cast(in_refs, out_refs):
    for x_ref, o_ref in zip(in_refs, out_refs):
        o_ref[...] = x_ref[...].astype(o_ref.dtype)


def _attn_kernel(sink_ref, q_ref, kp_ref, kc_ref, kn_ref, vp_ref, vc_ref, vn_ref, nw_ref,
                 *rest, L, n_ride):
    ride_in, o_ref, ride_out, y_scr = rest[:n_ride], rest[n_ride], rest[n_ride + 1:-1], rest[-1]
    _ride_along_cast(ride_in, ride_out)
    i = pl.program_id(1)
    tq = q_ref.shape[1]
    QB = WINDOW
    span = QB + 2 * WINDOW
    k = jnp.concatenate([kp_ref[0], kc_ref[0], kn_ref[0]], axis=0)
    vt = jnp.concatenate([vp_ref[0], vc_ref[0], vn_ref[0]], axis=1)
    key = lax.broadcasted_iota(jnp.int32, (span, QB), 0)
    qry = lax.broadcasted_iota(jnp.int32, (span, QB), 1)
    rel = key - qry
    band = (rel >= 0) & (rel <= 2 * WINDOW)
    for sub in range(tq // QB):
        r0 = sub * QB
        kpos = i * tq + r0 - WINDOW + key
        ok = band & (kpos >= 0) & (kpos < L)
        bias = jnp.where(ok, 0.0, NEG_BIG).astype(F32)
        for h in range(N_KV_HEADS):
            kh = k[r0:r0 + span, h * HEAD_DIM:(h + 1) * HEAD_DIM]
            vht = vt[h * HEAD_DIM:(h + 1) * HEAD_DIM, r0:r0 + span]
            for g0 in range(0, Q_PER_KV, HEADS_PER_PASS):
                heads = [h * Q_PER_KV + g0 + g for g in range(HEADS_PER_PASS)]
                qh = jnp.concatenate(
                    [q_ref[0, r0:r0 + QB, a * HEAD_DIM:(a + 1) * HEAD_DIM] for a in heads], axis=0)
                s = lax.dot_general(kh, qh, (((1,), (1,)), ((), ())), preferred_element_type=F32)
                s = s + jnp.tile(bias, (1, HEADS_PER_PASS))
                sink = jnp.concatenate([jnp.full((1, QB), sink_ref[a], F32) for a in heads], axis=1)
                m = jnp.maximum(jnp.max(s, axis=0, keepdims=True), sink)
                p = jnp.exp(s - m)
                denom = jnp.sum(p, axis=0, keepdims=True) + jnp.exp(sink - m)
                ot = _dot(vht, p.astype(BF16)) / denom
                for g, a in enumerate(heads):
                    y_scr[a * HEAD_DIM:(a + 1) * HEAD_DIM, r0:r0 + QB] = ot[:, g * QB:(g + 1) * QB]
    yt = y_scr[...]
    inv = lax.rsqrt(jnp.mean(yt * yt, axis=0, keepdims=True) + EPS)
    o_ref[0] = ((yt * inv).T * nw_ref[...]).astype(o_ref.dtype)


def _windowed_attention(q, k, vt, sink, out_norm_w, riders=(), tq=256):
    B, L, _ = q.shape
    nq = L // tq
    ride_specs = _ride_along_specs(riders, B * nq, lambda b, i, s: b * nq + i)
    rq = tq // WINDOW
    nwb = L // WINDOW
    cur = lambda b, i, s: (b, i, 0)
    prv = lambda b, i, s: (b, jnp.maximum(i * rq - 1, 0), 0)
    nxt = lambda b, i, s: (b, jnp.minimum((i + 1) * rq, nwb - 1), 0)
    cur_t = lambda b, i, s: (b, 0, i)
    prv_t = lambda b, i, s: (b, 0, jnp.maximum(i * rq - 1, 0))
    nxt_t = lambda b, i, s: (b, 0, jnp.minimum((i + 1) * rq, nwb - 1))
    grid_spec = pltpu.PrefetchScalarGridSpec(
        num_scalar_prefetch=1,
        grid=(B, L // tq),
        in_specs=[
            pl.BlockSpec((1, tq, ATTN_WIDTH), cur),
            pl.BlockSpec((1, WINDOW, KV_WIDTH), prv),
            pl.BlockSpec((1, tq, KV_WIDTH), cur),
            pl.BlockSpec((1, WINDOW, KV_WIDTH), nxt),
            pl.BlockSpec((1, KV_WIDTH, WINDOW), prv_t),
            pl.BlockSpec((1, KV_WIDTH, tq), cur_t),
            pl.BlockSpec((1, KV_WIDTH, WINDOW), nxt_t),
            pl.BlockSpec((1, ATTN_WIDTH), lambda b, i, s: (0, 0)),
        ] + ride_specs,
        out_specs=[pl.BlockSpec((1, tq, ATTN_WIDTH), cur)] + ride_specs,
        scratch_shapes=[pltpu.VMEM((ATTN_WIDTH, tq), F32)],
    )
    outs = pl.pallas_call(
        functools.partial(_attn_kernel, L=L, n_ride=len(riders)),
        grid_spec=grid_spec,
        out_shape=[jax.ShapeDtypeStruct((B, L, ATTN_WIDTH), BF16)]
        + [jax.ShapeDtypeStruct(r.shape, BF16) for r in riders],
        compiler_params=_cparams("parallel", "parallel"),
        name="banded_attention",
    )(sink.astype(F32), q, k, k, k, vt, vt, vt, out_norm_w[None, :], *riders)
    return outs[0], outs[1:]


def _outproj_kernel(x_ref, a_ref, hy_ref, hw_ref, wa_ref, wh_ref, nw_ref, wr_ref, rb_ref,
                    *rest, n_ride):
    ride_in, (x1_ref, h2_ref, rt_ref), ride_out = rest[:n_ride], rest[n_ride:n_ride + 3], rest[n_ride + 3:]
    _ride_along_cast(ride_in, ride_out)
    nslab = hy_ref.shape[2]
    hy = jnp.concatenate(
        [_gather_cols(lambda c: hy_ref[0, c, j, n1 * SUBLANES:(n1 + 1) * SUBLANES, :], hy_ref.shape[1])
         for n1 in range(hy_ref.shape[3] // SUBLANES) for j in range(nslab)],
        axis=0)
    hinv = lax.rsqrt(jnp.mean(hy * hy, axis=-1, keepdims=True) + EPS)
    hy_n = (hy * hinv * hw_ref[...]).astype(BF16)
    x1 = x_ref[...] + _dot(a_ref[...], wa_ref[...]) + _dot(hy_n, wh_ref[...])
    x1_ref[...] = x1
    inv = lax.rsqrt(jnp.mean(x1 * x1, axis=-1, keepdims=True) + EPS)
    h2 = x1 * inv * nw_ref[...]
    h2_ref[...] = h2.astype(BF16)
    h_hi, h_lo = _split_bf16(h2)
    both = _dot(h_hi, wr_ref[...])
    lg = (both[:, :ROUTER_PAD] + both[:, ROUTER_PAD:] + _dot(h_lo, wr_ref[:, :ROUTER_PAD])
          + rb_ref[...])
    lane = lax.broadcasted_iota(jnp.int32, lg.shape, 1)
    neg = jnp.float32(-jnp.inf)
    first = lambda mask: jnp.min(jnp.where(mask, lane, ROUTER_PAD), axis=-1, keepdims=True)
    gl = jnp.where(lane < N_EXPERT_GROUPS, lg, neg)
    gmax = jnp.max(gl, axis=-1, keepdims=True)
    g_idx = first(gl == gmax)
    g_gate = 1.0 / jnp.sum(jnp.exp(gl - gmax), axis=-1, keepdims=True)
    e_lane = lane - N_EXPERT_GROUPS
    in_grp = (e_lane >= g_idx * EXPERTS_PER_GROUP) & (e_lane < (g_idx + 1) * EXPERTS_PER_GROUP)
    el = jnp.where(in_grp, lg, neg)
    v1 = jnp.max(el, axis=-1, keepdims=True)
    i1 = first(el == v1)
    el2 = jnp.where(lane == i1, neg, el)
    v2 = jnp.max(el2, axis=-1, keepdims=True)
    i2 = first(el2 == v2)
    t = jnp.exp(v2 - v1)
    w1 = g_gate / (1.0 + t)
    w2 = g_gate * t / (1.0 + t)
    rt_ref[...] = jnp.where(lane == 0, (i1 - N_EXPERT_GROUPS).astype(F32),
                            jnp.where(lane == 1, (i2 - N_EXPERT_GROUPS).astype(F32),
                                      jnp.where(lane == 2, w1, jnp.where(lane == 3, w2, 0.0))))


def _out_proj(x2d, attn_n, hy, hy_norm_w, w_out_a, w_out_h, norm2_w, w_router, b_router,
              riders=(), tm=256):
    T = x2d.shape[0]
    _, NC, J, LJ, _ = hy.shape
    lt = (LJ * J) // tm
    row = lambda i: (i, 0)
    const = lambda i: (0, 0)
    ride_specs = _ride_along_specs(riders, T // tm, lambda i: i)
    outs = pl.pallas_call(
        functools.partial(_outproj_kernel, n_ride=len(riders)),
        grid=(T // tm,),
        in_specs=[
            pl.BlockSpec((tm, D_MODEL), row),
            pl.BlockSpec((tm, ATTN_WIDTH), row),
            pl.BlockSpec((1, NC, J, tm // J, LANES), lambda i: (i // lt, 0, 0, i % lt, 0)),
            pl.BlockSpec((1, HY_WIDTH), const),
            pl.BlockSpec((ATTN_WIDTH, D_MODEL), const),
            pl.BlockSpec((HY_WIDTH, D_MODEL), const),
            pl.BlockSpec((1, D_MODEL), const),
            pl.BlockSpec((D_MODEL, 2 * ROUTER_PAD), const),
            pl.BlockSpec((1, ROUTER_PAD), const),
        ] + ride_specs,
        out_specs=[
            pl.BlockSpec((tm, D_MODEL), row),
            pl.BlockSpec((tm, D_MODEL), row),
            pl.BlockSpec((tm, ROUTER_PAD), row),
        ] + ride_specs,
        out_shape=[
            jax.ShapeDtypeStruct((T, D_MODEL), F32),
            jax.ShapeDtypeStruct((T, D_MODEL), BF16),
            jax.ShapeDtypeStruct((T, ROUTER_PAD), F32),
        ] + [jax.ShapeDtypeStruct(r.shape, BF16) for r in riders],
        compiler_params=_cparams("parallel"),
        name="out_proj",
    )(x2d, attn_n, hy, hy_norm_w[None, :], w_out_a, w_out_h, norm2_w[None, :], w_router, b_router,
      *riders)
    return outs[0], outs[1], outs[2], outs[3:]


def _moe_kernel(be_ref, nu_ref, after_ref, x_ref, wg_ref, wu_ref, wd_ref, o_ref):
    del after_ref
    @pl.when(pl.program_id(0) < nu_ref[0])
    def _():
        x = x_ref[...]
        g = _dot(x, wg_ref[0])
        u = _dot(x, wu_ref[0])
        hid = (g * jax.nn.sigmoid(g)) * u
        o_ref[...] = _dot(hid.astype(BF16), wd_ref[0]).astype(o_ref.dtype)


def _expert_blocks(xs, block_expert, n_used, after, w_gate, w_up, w_down):
    P = xs.shape[0]
    bm = MOE_BLOCK_ROWS
    blk = lambda b, be, nu, af: (jnp.minimum(b, nu[0] - 1), 0)
    wsel = lambda b, be, nu, af: (be[jnp.minimum(b, nu[0] - 1)], 0, 0)
    grid_spec = pltpu.PrefetchScalarGridSpec(
        num_scalar_prefetch=3,
        grid=(P // bm,),
        in_specs=[
            pl.BlockSpec((bm, D_MODEL), blk),
            pl.BlockSpec((1, D_MODEL, D_EXPERT), wsel),
            pl.BlockSpec((1, D_MODEL, D_EXPERT), wsel),
            pl.BlockSpec((1, D_EXPERT, D_MODEL), wsel),
        ],
        out_specs=pl.BlockSpec((bm, D_MODEL), blk),
    )
    return pl.pallas_call(
        _moe_kernel,
        grid_spec=grid_spec,
        out_shape=jax.ShapeDtypeStruct((P, D_MODEL), BF16),
        compiler_params=_cparams("arbitrary"),
        name="moe_experts",
    )(block_expert, n_used, after, xs, w_gate, w_up, w_down)


def _moe_dispatch(h2, route):
    T = h2.shape[0]
    experts = route[:, :TOP_K].astype(jnp.int32)
    weights = route[:, TOP_K:2 * TOP_K]

    A = T * TOP_K
    bm = MOE_BLOCK_ROWS
    flat_e = experts.reshape(-1)
    onehot = (flat_e[:, None] == jnp.arange(N_EXPERTS, dtype=jnp.int32)[None, :]).astype(jnp.int32)
    csum = jnp.cumsum(onehot, axis=0)
    counts = csum[-1]
    rank = jnp.sum((csum - onehot) * onehot, axis=1)
    padded = ((counts + bm - 1) // bm) * bm
    pend = jnp.cumsum(padded)
    pstart = pend - padded
    seg_start = jnp.cumsum(counts) - counts
    dest = (pstart[flat_e] + rank).astype(jnp.int32)
    n_blocks = (A + bm - 1) // bm + N_EXPERTS
    P = n_blocks * bm
    order = jnp.argsort(flat_e, stable=True).astype(jnp.int32)
    q = jnp.arange(P, dtype=jnp.int32)
    e_q = jnp.minimum(jnp.sum((pend[None, :] <= q[:, None]).astype(jnp.int32), axis=1), N_EXPERTS - 1)
    r_q = jnp.minimum(q - pstart[e_q], jnp.maximum(counts[e_q] - 1, 0))
    src = jnp.clip(seg_start[e_q] + r_q, 0, A - 1)
    buf_tok = order.at[src].get(mode="promise_in_bounds") // TOP_K
    block_expert = e_q.reshape(n_blocks, bm)[:, 0]
    n_used = (pend[-1] // bm).astype(jnp.int32)[None]

    xs = h2.at[buf_tok].get(mode="promise_in_bounds")
    return dict(xs=xs, block_expert=block_expert, n_used=n_used, dest=dest.reshape(T, TOP_K),
                weights=weights)


def _moe_gather_back(yb, disp):
    d = disp["dest"]
    return (yb.at[d[:, 0]].get(mode="promise_in_bounds"), yb.at[d[:, 1]].get(mode="promise_in_bounds"))


def _moe_combine(x1, ys, disp):
    w = disp["weights"]
    return x1 + (w[:, 0:1] * ys[0].astype(F32) + w[:, 1:2] * ys[1].astype(F32))


def _layer_front(x, p):
    B, L, D = x.shape
    T = B * L
    x2d = x.reshape(T, D)
    need_cast = "experts_bf16" not in p
    ride_a, ride_o = (), ()
    if need_cast:
        wg, wu, wd = p["experts_f32"]
        ride_a = (wg.reshape(-1, D_EXPERT), wu.reshape(-1, D_EXPERT))
        ride_o = (wd.reshape(-1, D_MODEL),)
    q, k, v = _qkv_proj(x2d, L, p["norm1_w"], p["w_qkv"], p["q_norm_w"], p["k_norm_w"])
    attn_n, cast_a = _windowed_attention(q.reshape(B, L, ATTN_WIDTH), k.reshape(B, L, KV_WIDTH),
                                         v, p["attn_sink"], p["attn_out_norm_w"], riders=ride_a)
    uc = _u_proj_conv(x2d, L, p["norm1_w"], p["w_u"], p["conv_w"], p["conv_b"])
    tabs = _fft_tables(L)
    g_spec = _hyena_filter_spectrum(L, tabs, p["filt_w1"], p["filt_b1"], p["filt_w2"], p["filt_b2"],
                                    p["filt_w3"], p["filt_b3"], p["filt_w4"], p["filt_freq"])
    hy = _hyena_mixer(uc, g_spec, p["hy_bias"], tabs)
    x1, h2, route, cast_o = _out_proj(x2d, attn_n.reshape(T, ATTN_WIDTH), hy, p["hy_out_norm_w"],
                                      p["w_out_a"], p["w_out_h"], p["norm2_w"], p["w_router"],
                                      p["b_router"], riders=ride_o)
    if need_cast:
        p["experts_bf16"] = (cast_a[0].reshape(wg.shape), cast_a[1].reshape(wu.shape),
                             cast_o[0].reshape(wd.shape))
    return x1, _moe_dispatch(h2, route)


def _layers(xs, params):
    for p in params:
        fronts = [_layer_front(x, p) for x in xs]
        outs, pending = [], None
        for t, (x, (x1, disp)) in enumerate(zip(xs, fronts)):
            after = fronts[min(t + 1, len(fronts) - 1)][1]["n_used"]
            yb = _expert_blocks(disp["xs"], disp["block_expert"], disp["n_used"], after,
                                *p["experts_bf16"])
            if pending is not None:
                outs.append(_moe_combine(*pending))
            pending = (x1, _moe_gather_back(yb, disp), disp)
        outs.append(_moe_combine(*pending))
        xs = [o.reshape(x.shape) for o, x in zip(outs, xs)]
    return xs


def kernel(x_prompt, x_sample, norm1_w, w_in, q_norm_w, k_norm_w, attn_sink, conv_w, conv_b, filt_w1, filt_b1, filt_w2, filt_b2, filt_w3, filt_b3, filt_w4, filt_freq, hy_bias, attn_out_norm_w, hy_out_norm_w, w_out, norm2_w, w_route_group, b_route_group, w_route_expert, b_route_expert, w_gate, w_up, w_down):
    depth = norm1_w.shape[0]

    def layer_params(l):
        w_r = jnp.zeros((D_MODEL, ROUTER_PAD), F32)
        w_r = w_r.at[:, :N_EXPERT_GROUPS].set(w_route_group[l])
        w_r = w_r.at[:, N_EXPERT_GROUPS:N_EXPERT_GROUPS + N_EXPERTS].set(w_route_expert[l])
        r_hi, r_lo = _split_bf16(w_r)
        b_r = jnp.zeros((1, ROUTER_PAD), F32)
        b_r = b_r.at[0, :N_EXPERT_GROUPS].set(b_route_group[l])
        b_r = b_r.at[0, N_EXPERT_GROUPS:N_EXPERT_GROUPS + N_EXPERTS].set(b_route_expert[l])
        return dict(
            norm1_w=norm1_w[l], w_qkv=w_in[l][:, :QKV_WIDTH].astype(BF16),
            w_u=w_in[l][:, QKV_WIDTH:].astype(BF16), q_norm_w=q_norm_w[l], k_norm_w=k_norm_w[l],
            attn_sink=attn_sink[l], conv_w=conv_w[l], conv_b=conv_b[l],
            filt_w1=filt_w1[l], filt_b1=filt_b1[l], filt_w2=filt_w2[l], filt_b2=filt_b2[l],
            filt_w3=filt_w3[l], filt_b3=filt_b3[l], filt_w4=filt_w4[l], filt_freq=filt_freq[l],
            hy_bias=hy_bias[l], attn_out_norm_w=attn_out_norm_w[l], hy_out_norm_w=hy_out_norm_w[l],
            w_out_a=w_out[l][:ATTN_WIDTH].astype(BF16), w_out_h=w_out[l][ATTN_WIDTH:].astype(BF16),
            norm2_w=norm2_w[l], w_router=jnp.concatenate([r_hi, r_lo], axis=1), b_router=b_r,
            experts_f32=(w_gate[l], w_up[l], w_down[l]))

    params = [layer_params(l) for l in range(depth)]
    y_prompt, y_sample = _layers([x_prompt, x_sample], params)
    return (y_prompt, y_sample)
```

```python
import functools
import math

import jax
import jax.numpy as jnp
from jax import lax
from jax.experimental import pallas as pl
from jax.experimental.pallas import tpu as pltpu

F32 = jnp.float32
BF16 = jnp.bfloat16

D_MODEL = 2048
HEAD_DIM = 64
N_Q_HEADS = 16
N_KV_HEADS = 4
Q_PER_KV = N_Q_HEADS // N_KV_HEADS
ATTN_WIDTH = N_Q_HEADS * HEAD_DIM
KV_WIDTH = N_KV_HEADS * HEAD_DIM
QK_WIDTH = ATTN_WIDTH + KV_WIDTH
QKV_WIDTH = ATTN_WIDTH + 2 * KV_WIDTH
WINDOW = 128
ROT_DIM = HEAD_DIM // 4
ROPE_THETA = 500000.0
HY_WIDTH = D_MODEL - ATTN_WIDTH
HY_ORDER = 2
HY_EMB_DIM = 33
HY_FILTER_HIDDEN = 64
HY_DECAY_TARGET = 1e-2
HY_FAST_DECAY_PCT = 0.3
HY_SLOW_DECAY_PCT = 1.5
N_EXPERT_GROUPS = 4
EXPERTS_PER_GROUP = 8
N_EXPERTS = N_EXPERT_GROUPS * EXPERTS_PER_GROUP
TOP_K = 2
D_EXPERT = 1024
EPS = 1e-6

LANES = 128
SUBLANES = 8
FFT_N2 = 128
MOE_BLOCK_ROWS = 256
ROUTER_PAD = 128
VMEM_LIMIT = 56 * 1024 * 1024
NEG_BIG = -1e30
HEADS_PER_PASS = Q_PER_KV


def _cparams(*sem):
    return pltpu.CompilerParams(dimension_semantics=sem, vmem_limit_bytes=VMEM_LIMIT)


def _split_bf16(a):
    hi = a.astype(BF16)
    lo = (a - hi.astype(F32)).astype(BF16)
    return hi, lo


def _dot(a, b):
    return jnp.dot(a, b, preferred_element_type=F32)


def _dot3(a, b):
    a_hi, a_lo = _split_bf16(a)
    b_hi, b_lo = _split_bf16(b)
    return _dot(a_hi, b_hi) + _dot(a_lo, b_hi) + _dot(a_hi, b_lo)


def _qkv_kernel(x_ref, nw_ref, w_ref, seg_ref, hw_ref, rc_ref, rs1_ref, rs2_ref,
                q_ref, k_ref, v_ref):
    x = x_ref[...]
    inv = lax.rsqrt(jnp.mean(x * x, axis=-1, keepdims=True) + EPS)
    h = (x * inv * nw_ref[...]).astype(BF16)
    acc = _dot(h, w_ref[...])
    qk = acc[:, :QK_WIDTH]
    hi, lo = _split_bf16(qk * qk)
    seg = seg_ref[...]
    w = seg.shape[0]
    ms = jnp.concatenate(
        [_dot(hi[:, c * w:(c + 1) * w], seg) + _dot(lo[:, c * w:(c + 1) * w], seg)
         for c in range(QK_WIDTH // w)], axis=-1)
    xn = qk * lax.rsqrt(ms + EPS) * hw_ref[...]
    reps = QK_WIDTH // LANES
    rc = jnp.tile(rc_ref[...], (1, reps))
    rs1 = jnp.tile(rs1_ref[...], (1, reps))
    rs2 = jnp.tile(rs2_ref[...], (1, reps))
    half = ROT_DIM // 2
    y = xn * rc + pltpu.roll(xn, QK_WIDTH - half, 1) * rs1 + pltpu.roll(xn, half, 1) * rs2
    q_ref[...] = (y[:, :ATTN_WIDTH] * (HEAD_DIM ** -0.5)).astype(BF16)
    k_ref[...] = y[:, ATTN_WIDTH:].astype(BF16)
    v_ref[0] = acc[:, QK_WIDTH:].T.astype(BF16)


def _rope_tables(L):
    half = ROT_DIM // 2
    inv_freq = jnp.power(ROPE_THETA, -jnp.arange(half, dtype=F32) * 2.0 / ROT_DIM)
    ang = jnp.arange(L, dtype=F32)[:, None] * inv_freq[None, :]
    d = jnp.arange(LANES) % HEAD_DIM
    ang_l = ang[:, d % half]
    cos, sin = jnp.cos(ang_l), jnp.sin(ang_l)
    rc = jnp.where(d < ROT_DIM, cos, 1.0)
    rs1 = jnp.where(d < half, -sin, 0.0)
    rs2 = jnp.where((d >= half) & (d < ROT_DIM), sin, 0.0)
    return rc.astype(F32), rs1.astype(F32), rs2.astype(F32)


def _qkv_proj(x2d, L, norm1_w, w_qkv, q_norm_w, k_norm_w, tm=512):
    T = x2d.shape[0]
    seg_w = 256
    seg = (jnp.arange(seg_w)[:, None] // HEAD_DIM == jnp.arange(seg_w)[None, :] // HEAD_DIM)
    seg = (seg.astype(F32) / HEAD_DIM).astype(BF16)
    hw = jnp.concatenate([jnp.tile(q_norm_w, N_Q_HEADS), jnp.tile(k_norm_w, N_KV_HEADS)])[None, :]
    rc, rs1, rs2 = _rope_tables(L)
    lt = L // tm
    row = lambda i: (i, 0)
    const = lambda i: (0, 0)
    pos = lambda i: (i % lt, 0)
    return pl.pallas_call(
        _qkv_kernel,
        grid=(T // tm,),
        in_specs=[
            pl.BlockSpec((tm, D_MODEL), row),
            pl.BlockSpec((1, D_MODEL), const),
            pl.BlockSpec((D_MODEL, QKV_WIDTH), const),
            pl.BlockSpec((seg_w, seg_w), const),
            pl.BlockSpec((1, QK_WIDTH), const),
            pl.BlockSpec((tm, LANES), pos),
            pl.BlockSpec((tm, LANES), pos),
            pl.BlockSpec((tm, LANES), pos),
        ],
        out_specs=[
            pl.BlockSpec((tm, ATTN_WIDTH), row),
            pl.BlockSpec((tm, KV_WIDTH), row),
            pl.BlockSpec((1, KV_WIDTH, tm), lambda i: (i // lt, 0, i % lt)),
        ],
        out_shape=[
            jax.ShapeDtypeStruct((T, ATTN_WIDTH), BF16),
            jax.ShapeDtypeStruct((T, KV_WIDTH), BF16),
            jax.ShapeDtypeStruct((T // L, KV_WIDTH, L), BF16),
        ],
        compiler_params=_cparams("parallel"),
        name="qkv_proj",
    )(x2d, norm1_w[None, :], w_qkv, seg, hw, rc, rs1, rs2)


HALO = 16

def _uproj_kernel(x_ref, xp_ref, xn_ref, nw_ref, w_ref, cw_ref, cb_ref, o_ref, h_scr, *, lt):
    i = pl.program_id(0)
    tm = x_ref.shape[0]

    @pl.when(pl.program_id(1) == 0)
    def _():
        def normed(x):
            inv = lax.rsqrt(jnp.mean(x * x, axis=-1, keepdims=True) + EPS)
            return x * inv * nw_ref[...]

        li = i % lt
        hp = jnp.where(li > 0, normed(xp_ref[...]), 0.0)
        hn = jnp.where(li < lt - 1, normed(xn_ref[...]), 0.0)
        h_scr[0:HALO] = hp.astype(BF16)
        h_scr[HALO:HALO + tm] = normed(x_ref[...]).astype(BF16)
        h_scr[HALO + tm:2 * HALO + tm] = hn.astype(BF16)

    u = _dot(h_scr[...], w_ref[...])
    w = cw_ref[...]
    val = (w[0:1] * u[HALO - 1:HALO - 1 + tm] + w[1:2] * u[HALO:HALO + tm]
           + w[2:3] * u[HALO + 1:HALO + 1 + tm] + cb_ref[...])
    for n1 in range(tm // FFT_N2):
        for j in range(FFT_N2 // SUBLANES):
            r0 = n1 * FFT_N2 + j * SUBLANES
            for c in range(val.shape[1] // LANES):
                o_ref[0, 0, c, j, n1 * SUBLANES:(n1 + 1) * SUBLANES, :] = (
                    val[r0:r0 + SUBLANES, c * LANES:(c + 1) * LANES])


def _u_proj_conv(x2d, L, norm1_w, w_u, conv_w, conv_b, tm=1024):
    T = x2d.shape[0]
    C = HY_WIDTH
    nparts = w_u.shape[1] // C
    lt = L // tm
    hb = tm // HALO
    J = FFT_N2 // SUBLANES
    return pl.pallas_call(
        functools.partial(_uproj_kernel, lt=lt),
        grid=(T // tm, nparts),
        in_specs=[
            pl.BlockSpec((tm, D_MODEL), lambda i, j: (i, 0)),
            pl.BlockSpec((HALO, D_MODEL), lambda i, j: (jnp.maximum(i * hb - 1, 0), 0)),
            pl.BlockSpec((HALO, D_MODEL), lambda i, j: (jnp.minimum((i + 1) * hb, T // HALO - 1), 0)),
            pl.BlockSpec((1, D_MODEL), lambda i, j: (0, 0)),
            pl.BlockSpec((D_MODEL, C), lambda i, j: (0, j)),
            pl.BlockSpec((3, C), lambda i, j: (0, j)),
            pl.BlockSpec((1, C), lambda i, j: (0, j)),
        ],
        out_specs=pl.BlockSpec((1, 1, C // LANES, J, tm // J, LANES),
                               lambda i, j: (j, i // lt, 0, 0, i % lt, 0)),
        out_shape=jax.ShapeDtypeStruct((nparts, T // L, C // LANES, J, L // J, LANES), F32),
        scratch_shapes=[pltpu.VMEM((tm + 2 * HALO, D_MODEL), BF16)],
        compiler_params=_cparams("parallel", "arbitrary"),
        name="u_proj_conv",
    )(x2d, x2d, x2d, norm1_w[None, :], w_u, conv_w, conv_b[None, :])


def _fft_tables(L):
    N = 2 * L
    N2 = FFT_N2
    N1 = N // N2
    H = N1 // 2
    k1 = jnp.arange(N1, dtype=jnp.int32)
    n2 = jnp.arange(N2, dtype=jnp.int32)
    a1 = ((k1[:, None] * k1[None, :]) % N1).astype(F32) * (2.0 * math.pi / N1)
    c1, s1 = jnp.cos(a1), jnp.sin(a1)
    at = (n2[:, None] * k1[None, :]).astype(F32) * (2.0 * math.pi / N)
    ct, st = jnp.cos(at), jnp.sin(at)
    c = c1[None] * ct[:, :, None] - s1[None] * st[:, :, None]
    s = s1[None] * ct[:, :, None] + c1[None] * st[:, :, None]
    m_filt = jnp.concatenate([c, -s], axis=1).astype(BF16)
    ch, sh = c[:, :, :H], s[:, :, :H]
    m_fwd = jnp.concatenate([jnp.concatenate([ch, sh], axis=2),
                             jnp.concatenate([-sh, ch], axis=2)], axis=1)
    cT = c1[None, :H] * ct[:, None, :] - s1[None, :H] * st[:, None, :]
    sT = s1[None, :H] * ct[:, None, :] + c1[None, :H] * st[:, None, :]
    m_inv = jnp.concatenate([jnp.concatenate([cT, -sT], axis=2),
                             jnp.concatenate([sT, cT], axis=2)], axis=1)
    k2 = jnp.arange(N2, dtype=jnp.int32)
    ph = ((k2[:, None] * n2[None, :]) % N2).astype(F32) * (2.0 * math.pi / N2)
    c2, s2 = jnp.cos(ph), jnp.sin(ph)
    m2_fwd = jnp.concatenate([jnp.concatenate([c2, s2], axis=1),
                              jnp.concatenate([-s2, c2], axis=1)], axis=0)
    m2_inv = m2_fwd.T
    return dict(N1=N1, N2=N2, m_filt=m_filt, m_fwd=m_fwd.astype(BF16), m_inv=m_inv.astype(BF16),
                m2_fwd=m2_fwd.astype(BF16), m2_inv=m2_inv.astype(BF16))


def _filt1_kernel(w1_ref, b1_ref, w2_ref, b2_ref, w3_ref, b3_ref, w4_ref, a_ref, frl_ref, off_ref,
                  dl_ref, m_ref, o_ref, sabs_ref, *, L, N1, N2, nb):
    j = pl.program_id(0)
    C2 = HY_ORDER * HY_WIDTH
    H = N1 // 2
    rows = nb * N1
    ridx = lax.broadcasted_iota(jnp.int32, (rows, LANES), 0)
    lane = lax.broadcasted_iota(jnp.int32, (rows, LANES), 1)
    n2 = j * nb + ridx // N1
    n1 = ridx % N1
    r = n1 * N2 + n2
    lag = jnp.where(r < L, r, 2 * L - r)
    valid = (r != L).astype(F32)
    lagf = jnp.minimum(lag, L - 1).astype(F32)
    t = lagf / (L - 1)
    wpos = lagf * (2.0 * math.pi / L)
    phase = wpos * frl_ref[...] + off_ref[...]
    zemb = jnp.where(lane == 0, t, jnp.where(lane < HY_EMB_DIM, jnp.cos(phase), 0.0))
    half = rows // 2
    a = a_ref[...]
    h = jnp.concatenate([zemb[:half], zemb[half:]], axis=1)
    h = jnp.sin(a * (_dot3(h, w1_ref[...]) + b1_ref[...]))
    h = jnp.sin(a * (_dot3(h, w2_ref[...]) + b2_ref[...]))
    h = jnp.sin(a * (_dot3(h, w3_ref[...]) + b3_ref[...]))
    h_hi, h_lo = _split_bf16(h)
    decay = jnp.exp(-jnp.tile(t, (1, HY_WIDTH // LANES)) * dl_ref[...])
    decay = decay * jnp.tile(valid, (1, HY_WIDTH // LANES))
    decay2 = jnp.tile(decay, (1, HY_ORDER))

    @pl.when(j == 0)
    def _():
        sabs_ref[...] = jnp.zeros_like(sabs_ref)

    tot = jnp.zeros((8, C2), F32)
    for l in range(nb):
        side = (l * N1) // half
        r0 = l * N1 - side * half
        fs = slice(r0, r0 + H)
        bs = slice(r0 + H, r0 + N1)
        wf = w4_ref[side, :, :C2]
        wb = w4_ref[side, :, C2:]
        gf = _dot(h_hi[fs], wf) + _dot(h_lo[fs], wf)
        gb = _dot(h_hi[bs], wb) + _dot(h_lo[bs], wb)
        g = jnp.concatenate([gf, gb], axis=0) * decay2[l * N1:(l + 1) * N1]
        tot = tot + jnp.sum(jnp.abs(g).reshape(N1 // 8, 8, C2), axis=0)
        y = _dot(m_ref[l], g.astype(BF16))
        _store_tiles(lambda c: o_ref.at[c], _pack_complex(y[:N1], y[N1:]), C2 // LANES, l)
    sabs_ref[...] += tot


def _pack_complex(re, im):
    r = lax.bitcast_convert_type(re.astype(BF16).astype(F32), jnp.uint32)
    i = lax.bitcast_convert_type(im.astype(BF16).astype(F32), jnp.uint32)
    return r | (i >> 16)


def _unpack_complex(w):
    re = lax.bitcast_convert_type(w & jnp.uint32(0xFFFF0000), F32)
    im = lax.bitcast_convert_type(w << 16, F32)
    return re, im


_pack_bf16_pair = _pack_complex
_unpack_bf16_pair = _unpack_complex


def _store_tiles(dst, w, ncols, s):
    for c in range(ncols):
        d = dst(c)
        for t in range(w.shape[0] // SUBLANES):
            d[t, s * SUBLANES:(s + 1) * SUBLANES, :] = (
                w[t * SUBLANES:(t + 1) * SUBLANES, c * LANES:(c + 1) * LANES])


def _rows_strided(ref2d, s):
    return ref2d[pl.ds(s, ref2d.shape[0] // SUBLANES, stride=SUBLANES), :]


def _gather_cols(fn, ncols):
    return jnp.concatenate([fn(c) for c in range(ncols)], axis=1)


def _block_diag2(w):
    z = jnp.zeros_like(w)
    return jnp.concatenate([jnp.concatenate([w, z], axis=1), jnp.concatenate([z, w], axis=1)], axis=0)


def _hyena_filter_spectrum(L, tabs, filt_w1, filt_b1, filt_w2, filt_b2, filt_w3, filt_b3,
                           filt_w4, filt_freq):
    N1, N2 = tabs["N1"], tabs["N2"]
    N = N1 * N2
    C2 = HY_ORDER * HY_WIDTH
    FH = HY_FILTER_HIDDEN
    nb = max(2, 512 // N1)
    bands = (HY_EMB_DIM - 1) // 2
    fr = jnp.linspace(1e-4, bands - 1, bands, dtype=F32)
    frl = jnp.zeros((LANES,), F32).at[1:1 + bands].set(fr).at[1 + bands:1 + 2 * bands].set(fr)[None, :]
    off = jnp.zeros((LANES,), F32).at[1 + bands:1 + 2 * bands].set(0.5 * math.pi)[None, :]
    w1p = jnp.zeros((LANES, FH), F32).at[:HY_EMB_DIM].set(filt_w1)
    w4 = filt_w4.astype(BF16)
    w4z = jnp.zeros_like(w4)
    w4s = jnp.stack([jnp.concatenate([w4, w4z], axis=0), jnp.concatenate([w4z, w4], axis=0)])
    two = lambda v: jnp.tile(v, 2)[None, :]
    max_decay = math.log(HY_DECAY_TARGET) / HY_FAST_DECAY_PCT
    min_decay = math.log(HY_DECAY_TARGET) / HY_SLOW_DECAY_PCT
    deltas = jnp.abs(jnp.linspace(min_decay, max_decay, HY_WIDTH, dtype=F32))[None, :]
    full = lambda shape: pl.BlockSpec(shape, lambda j: (0,) * len(shape))
    a_g, sabs = pl.pallas_call(
        functools.partial(_filt1_kernel, L=L, N1=N1, N2=N2, nb=nb),
        grid=(N2 // nb,),
        in_specs=[
            full((2 * LANES, 2 * FH)), full((1, 2 * FH)),
            full((2 * FH, 2 * FH)), full((1, 2 * FH)),
            full((2 * FH, 2 * FH)), full((1, 2 * FH)),
            full((2, 2 * FH, 2 * C2)), full((1, 2 * FH)),
            full((1, LANES)), full((1, LANES)), full((1, HY_WIDTH)),
            pl.BlockSpec((nb, 2 * N1, N1), lambda j: (j, 0, 0)),
        ],
        out_specs=[
            pl.BlockSpec((C2 // LANES, N1 // SUBLANES, nb * SUBLANES, LANES), lambda j: (0, 0, j, 0)),
            pl.BlockSpec((8, C2), lambda j: (0, 0)),
        ],
        out_shape=[
            jax.ShapeDtypeStruct((C2 // LANES, N1 // SUBLANES, N2 * SUBLANES, LANES), jnp.uint32),
            jax.ShapeDtypeStruct((8, C2), F32),
        ],
        compiler_params=_cparams("arbitrary"),
        name="hyena_filter_stage1",
    )(_block_diag2(w1p), two(filt_b1), _block_diag2(filt_w2), two(filt_b2), _block_diag2(filt_w3),
      two(filt_b3), w4s, two(filt_freq), frl, off, deltas, tabs["m_filt"])
    scale = (1.0 / (jnp.sum(sabs, axis=0) * N))[None, :]
    return a_g, scale


def _fwd1_kernel(x_ref, m_ref, o_ref):
    cb = o_ref.shape[1]
    N1 = m_ref.shape[2]
    for s in range(SUBLANES):
        rows = [_gather_cols(lambda c: _rows_strided(x_ref.at[0, bi, c, 0], s), cb) for bi in range(2)]
        xl = jnp.concatenate(rows, axis=0).astype(BF16)
        y = _dot(m_ref[s], xl)
        _store_tiles(lambda c: o_ref.at[0, c], _pack_complex(y[:N1], y[N1:]), cb, s)


def _hyena_stage1(z, which, tabs):
    N1, N2 = tabs["N1"], tabs["N2"]
    H = N1 // 2
    B, NC = z.shape[1], z.shape[2]
    cb = min(NC, 4)
    x_spec = pl.BlockSpec((1, 2, cb, 1, H * SUBLANES, LANES), lambda p, c, j: (which, p, c, j, 0, 0))
    return pl.pallas_call(
        _fwd1_kernel,
        grid=(B // 2, NC // cb, N2 // SUBLANES),
        in_specs=[x_spec, pl.BlockSpec((SUBLANES, 2 * N1, N1), lambda p, c, j: (j, 0, 0))],
        out_specs=pl.BlockSpec((1, cb, N1 // SUBLANES, SUBLANES * SUBLANES, LANES),
                               lambda p, c, j: (p, c, 0, j, 0)),
        out_shape=jax.ShapeDtypeStruct((B // 2, NC, N1 // SUBLANES, N2 * SUBLANES, LANES), jnp.uint32),
        compiler_params=_cparams("parallel", "parallel", "parallel"),
        name="hyena_stage1",
    )(z, tabs["m_fwd"])


def _mid_kernel(a_ref, ag_ref, sc_ref, mf_ref, mi_ref, o_ref, g_scr):
    cb = a_ref.shape[1]
    N2 = a_ref.shape[3] // SUBLANES

    @pl.when(pl.program_id(2) == 0)
    def _():
        for s in range(SUBLANES):
            re, im = _unpack_complex(_gather_cols(lambda c: _rows_strided(ag_ref.at[c, 0], s), cb))
            ag = jnp.concatenate([re, im], axis=0).astype(BF16)
            g_scr[s] = _dot(mf_ref[...], ag) * sc_ref[...]

    for s in range(SUBLANES):
        re, im = _unpack_complex(_gather_cols(lambda c: _rows_strided(a_ref.at[0, c, 0], s), cb))
        a = jnp.concatenate([re, im], axis=0).astype(BF16)
        x = _dot(mf_ref[...], a)
        xr, xi = x[:N2], x[N2:]
        gr, gi = g_scr[s, :N2], g_scr[s, N2:]
        y = jnp.concatenate([xr * gr - xi * gi, xr * gi + xi * gr], axis=0).astype(BF16)
        b = _dot(mi_ref[...], y)
        _store_tiles(lambda c: o_ref.at[0, c], _pack_complex(b[:N2], b[N2:]), cb, s)


def _hyena_stage2(a, a_g, scale, order, tabs, cb=8):
    N1, N2 = tabs["N1"], tabs["N2"]
    P, NC = a.shape[:2]
    cb = min(cb, NC)
    ncb = NC // cb
    return pl.pallas_call(
        _mid_kernel,
        grid=(N1 // SUBLANES, ncb, P),
        in_specs=[
            pl.BlockSpec((1, cb, 1, N2 * SUBLANES, LANES), lambda i, c, p: (p, c, i, 0, 0)),
            pl.BlockSpec((cb, 1, N2 * SUBLANES, LANES), lambda i, c, p: (order * ncb + c, i, 0, 0)),
            pl.BlockSpec((1, cb * LANES), lambda i, c, p: (0, order * ncb + c)),
            pl.BlockSpec((2 * N2, 2 * N2), lambda i, c, p: (0, 0)),
            pl.BlockSpec((2 * N2, 2 * N2), lambda i, c, p: (0, 0)),
        ],
        out_specs=pl.BlockSpec((1, cb, N2 // SUBLANES, SUBLANES * SUBLANES, LANES),
                               lambda i, c, p: (p, c, 0, i, 0)),
        out_shape=jax.ShapeDtypeStruct((P, NC, N2 // SUBLANES, N1 * SUBLANES, LANES), jnp.uint32),
        scratch_shapes=[pltpu.VMEM((SUBLANES, 2 * N2, cb * LANES), F32)],
        compiler_params=_cparams("parallel", "parallel", "arbitrary"),
        name="hyena_stage2",
    )(a, a_g, scale, tabs["m2_fwd"], tabs["m2_inv"])


def _inv1_kernel(b_ref, m_ref, gate_ref, z_ref, bias_ref, *rest, z_slabbed, final):
    if final:
        (o_ref,) = rest
    else:
        mf_ref, o_ref, a_ref = rest
    cb = b_ref.shape[1]
    N1 = m_ref.shape[1]
    H = N1 // 2
    bias = bias_ref[...]
    for s in range(SUBLANES):
        br, bi_ = _unpack_complex(_gather_cols(lambda c: _rows_strided(b_ref.at[0, c, 0], s), cb))
        b = jnp.concatenate([br, bi_], axis=0).astype(BF16)
        conv = _dot(m_ref[s], b)
        outs = []
        for bi in range(2):
            gate = _gather_cols(lambda c: _rows_strided(gate_ref.at[0, bi, c, 0], s), cb)
            if z_slabbed:
                z = _gather_cols(lambda c: _rows_strided(z_ref.at[0, bi, c, 0], s), cb)
            else:
                z = _gather_cols(lambda c: z_ref[bi, c, s], cb)
            out = gate * (conv[bi * H:(bi + 1) * H] + z * bias)
            outs.append(out)
            for c in range(cb):
                piece = out[:, c * LANES:(c + 1) * LANES]
                if final:
                    o_ref.at[bi, c, 0][pl.ds(s, H, stride=SUBLANES), :] = piece
                else:
                    o_ref[bi, c, s] = piece
        if not final:
            xl = jnp.concatenate(outs, axis=0).astype(BF16)
            y = _dot(mf_ref[s], xl)
            _store_tiles(lambda c: a_ref.at[0, c], _pack_complex(y[:N1], y[N1:]), cb, s)


def _hyena_stage3(bq, ucl, gate_idx, z, z_idx, bias, tabs, final):
    N1, N2 = tabs["N1"], tabs["N2"]
    H = N1 // 2
    P, NC = bq.shape[:2]
    cb = min(NC, 4 if final else max(1, 512 // N1))
    z_slabbed = z_idx is not None
    slab_blk = (1, 2, cb, 1, H * SUBLANES, LANES)
    n2m_blk = (2, cb, SUBLANES, H, LANES)
    if z_slabbed:
        z_spec = pl.BlockSpec(slab_blk, lambda p, c, j: (z_idx, p, c, j, 0, 0))
    else:
        z_spec = pl.BlockSpec(n2m_blk, lambda p, c, j: (p, c, j, 0, 0))
    in_specs = [
        pl.BlockSpec((1, cb, 1, N1 * SUBLANES, LANES), lambda p, c, j: (p, c, j, 0, 0)),
        pl.BlockSpec((SUBLANES, N1, 2 * N1), lambda p, c, j: (j, 0, 0)),
        pl.BlockSpec(slab_blk, lambda p, c, j: (gate_idx, p, c, j, 0, 0)),
        z_spec,
        pl.BlockSpec((1, cb * LANES), lambda p, c, j: (0, c)),
    ]
    args = [bq, tabs["m_inv"], ucl, z, bias[None, :]]
    if final:
        out_specs = pl.BlockSpec((2, cb, 1, H * SUBLANES, LANES), lambda p, c, j: (p, c, j, 0, 0))
        out_shape = jax.ShapeDtypeStruct((2 * P, NC, N2 // SUBLANES, H * SUBLANES, LANES), F32)
    else:
        in_specs.append(pl.BlockSpec((SUBLANES, 2 * N1, N1), lambda p, c, j: (j, 0, 0)))
        args.append(tabs["m_fwd"])
        out_specs = [
            pl.BlockSpec(n2m_blk, lambda p, c, j: (p, c, j, 0, 0)),
            pl.BlockSpec((1, cb, N1 // SUBLANES, SUBLANES * SUBLANES, LANES),
                         lambda p, c, j: (p, c, 0, j, 0)),
        ]
        out_shape = [
            jax.ShapeDtypeStruct((2 * P, NC, N2, H, LANES), F32),
            jax.ShapeDtypeStruct((P, NC, N1 // SUBLANES, N2 * SUBLANES, LANES), jnp.uint32),
        ]
    return pl.pallas_call(
        functools.partial(_inv1_kernel, z_slabbed=z_slabbed, final=final),
        grid=(P, NC // cb, N2 // SUBLANES),
        in_specs=in_specs,
        out_specs=out_specs,
        out_shape=out_shape,
        compiler_params=_cparams("parallel", "parallel", "parallel"),
        name="hyena_stage3",
    )(*args)


def _hyena_mixer(ucl, g_spec, hy_bias, tabs):
    a_g, scale = g_spec
    z, z_idx = ucl, 2
    a = _hyena_stage1(z, z_idx, tabs)
    for order in range(HY_ORDER):
        bq = _hyena_stage2(a, a_g, scale, order, tabs)
        if order == HY_ORDER - 1:
            return _hyena_stage3(bq, ucl, order, z, z_idx, hy_bias[order], tabs, final=True)
        z, a = _hyena_stage3(bq, ucl, order, z, z_idx, hy_bias[order], tabs, final=False)
        z_idx = None


def _ride_along_specs(riders, steps, index):
    specs = []
    for r in riders:
        rows = r.shape[0] // steps
        specs.append(pl.BlockSpec((rows, r.shape[1]), lambda *g, _i=index: (_i(*g), 0)))
    return specs


def _ride_along_cast(in_refs, out_refs):
    for x_ref, o_ref in zip(in_refs, out_refs):
        o_ref[...] = x_ref[...].astype(o_ref.dtype)


def _attn_kernel(sink_ref, q_ref, kp_ref, kc_ref, kn_ref, vp_ref, vc_ref, vn_ref, nw_ref,
                 *rest, L, n_ride):
    ride_in, o_ref, ride_out, y_scr = rest[:n_ride], rest[n_ride], rest[n_ride + 1:-1], rest[-1]
    _ride_along_cast(ride_in, ride_out)
    i = pl.program_id(1)
    tq = q_ref.shape[1]
    QB = WINDOW
    span = QB + 2 * WINDOW
    k = jnp.concatenate([kp_ref[0], kc_ref[0], kn_ref[0]], axis=0)
    vt = jnp.concatenate([vp_ref[0], vc_ref[0], vn_ref[0]], axis=1)
    key = lax.broadcasted_iota(jnp.int32, (span, QB), 0)
    qry = lax.broadcasted_iota(jnp.int32, (span, QB), 1)
    rel = key - qry
    band = (rel >= 0) & (rel <= 2 * WINDOW)
    for sub in range(tq // QB):
        r0 = sub * QB
        kpos = i * tq + r0 - WINDOW + key
        ok = band & (kpos >= 0) & (kpos < L)
        bias = jnp.where(ok, 0.0, NEG_BIG).astype(F32)
        for h in range(N_KV_HEADS):
            kh = k[r0:r0 + span, h * HEAD_DIM:(h + 1) * HEAD_DIM]
            vht = vt[h * HEAD_DIM:(h + 1) * HEAD_DIM, r0:r0 + span]
            for g0 in range(0, Q_PER_KV, HEADS_PER_PASS):
                heads = [h * Q_PER_KV + g0 + g for g in range(HEADS_PER_PASS)]
                qh = jnp.concatenate(
                    [q_ref[0, r0:r0 + QB, a * HEAD_DIM:(a + 1) * HEAD_DIM] for a in heads], axis=0)
                s = lax.dot_general(kh, qh, (((1,), (1,)), ((), ())), preferred_element_type=F32)
                s = s + jnp.tile(bias, (1, HEADS_PER_PASS))
                sink = jnp.concatenate([jnp.full((1, QB), sink_ref[a], F32) for a in heads], axis=1)
                m = jnp.maximum(jnp.max(s, axis=0, keepdims=True), sink)
                p = jnp.exp(s - m)
                denom = jnp.sum(p, axis=0, keepdims=True) + jnp.exp(sink - m)
                ot = _dot(vht, p.astype(BF16)) / denom
                for g, a in enumerate(heads):
                    y_scr[a * HEAD_DIM:(a + 1) * HEAD_DIM, r0:r0 + QB] = ot[:, g * QB:(g + 1) * QB]
    yt = y_scr[...]
    inv = lax.rsqrt(jnp.mean(yt * yt, axis=0, keepdims=True) + EPS)
    o_ref[0] = ((yt * inv).T * nw_ref[...]).astype(o_ref.dtype)


def _windowed_attention(q, k, vt, sink, out_norm_w, riders=(), tq=256):
    B, L, _ = q.shape
    nq = L // tq
    ride_specs = _ride_along_specs(riders, B * nq, lambda b, i, s: b * nq + i)
    rq = tq // WINDOW
    nwb = L // WINDOW
    cur = lambda b, i, s: (b, i, 0)
    prv = lambda b, i, s: (b, jnp.maximum(i * rq - 1, 0), 0)
    nxt = lambda b, i, s: (b, jnp.minimum((i + 1) * rq, nwb - 1), 0)
    cur_t = lambda b, i, s: (b, 0, i)
    prv_t = lambda b, i, s: (b, 0, jnp.maximum(i * rq - 1, 0))
    nxt_t = lambda b, i, s: (b, 0, jnp.minimum((i + 1) * rq, nwb - 1))
    grid_spec = pltpu.PrefetchScalarGridSpec(
        num_scalar_prefetch=1,
        grid=(B, L // tq),
        in_specs=[
            pl.BlockSpec((1, tq, ATTN_WIDTH), cur),
            pl.BlockSpec((1, WINDOW, KV_WIDTH), prv),
            pl.BlockSpec((1, tq, KV_WIDTH), cur),
            pl.BlockSpec((1, WINDOW, KV_WIDTH), nxt),
            pl.BlockSpec((1, KV_WIDTH, WINDOW), prv_t),
            pl.BlockSpec((1, KV_WIDTH, tq), cur_t),
            pl.BlockSpec((1, KV_WIDTH, WINDOW), nxt_t),
            pl.BlockSpec((1, ATTN_WIDTH), lambda b, i, s: (0, 0)),
        ] + ride_specs,
        out_specs=[pl.BlockSpec((1, tq, ATTN_WIDTH), cur)] + ride_specs,
        scratch_shapes=[pltpu.VMEM((ATTN_WIDTH, tq), F32)],
    )
    outs = pl.pallas_call(
        functools.partial(_attn_kernel, L=L, n_ride=len(riders)),
        grid_spec=grid_spec,
        out_shape=[jax.ShapeDtypeStruct((B, L, ATTN_WIDTH), BF16)]
        + [jax.ShapeDtypeStruct(r.shape, BF16) for r in riders],
        compiler_params=_cparams("parallel", "parallel"),
        name="banded_attention",
    )(sink.astype(F32), q, k, k, k, vt, vt, vt, out_norm_w[None, :], *riders)
    return outs[0], outs[1:]


def _outproj_kernel(x_ref, a_ref, hy_ref, hw_ref, wa_ref, wh_ref, nw_ref, wr_ref, rb_ref,
                    *rest, n_ride):
    ride_in, (x1_ref, h2_ref, rt_ref), ride_out = rest[:n_ride], rest[n_ride:n_ride + 3], rest[n_ride + 3:]
    _ride_along_cast(ride_in, ride_out)
    nslab = hy_ref.shape[2]
    hy = jnp.concatenate(
        [_gather_cols(lambda c: hy_ref[0, c, j, n1 * SUBLANES:(n1 + 1) * SUBLANES, :], hy_ref.shape[1])
         for n1 in range(hy_ref.shape[3] // SUBLANES) for j in range(nslab)],
        axis=0)
    hinv = lax.rsqrt(jnp.mean(hy * hy, axis=-1, keepdims=True) + EPS)
    hy_n = (hy * hinv * hw_ref[...]).astype(BF16)
    x1 = x_ref[...] + _dot(a_ref[...], wa_ref[...]) + _dot(hy_n, wh_ref[...])
    x1_ref[...] = x1
    inv = lax.rsqrt(jnp.mean(x1 * x1, axis=-1, keepdims=True) + EPS)
    h2 = x1 * inv * nw_ref[...]
    h2_ref[...] = _pack_bf16_pair(h2[:, :D_MODEL // 2], h2[:, D_MODEL // 2:])
    h_hi, h_lo = _split_bf16(h2)
    both = _dot(h_hi, wr_ref[...])
    lg = (both[:, :ROUTER_PAD] + both[:, ROUTER_PAD:] + _dot(h_lo, wr_ref[:, :ROUTER_PAD])
          + rb_ref[...])
    lane = lax.broadcasted_iota(jnp.int32, lg.shape, 1)
    neg = jnp.float32(-jnp.inf)
    first = lambda mask: jnp.min(jnp.where(mask, lane, ROUTER_PAD), axis=-1, keepdims=True)
    gl = jnp.where(lane < N_EXPERT_GROUPS, lg, neg)
    gmax = jnp.max(gl, axis=-1, keepdims=True)
    g_idx = first(gl == gmax)
    g_gate = 1.0 / jnp.sum(jnp.exp(gl - gmax), axis=-1, keepdims=True)
    e_lane = lane - N_EXPERT_GROUPS
    in_grp = (e_lane >= g_idx * EXPERTS_PER_GROUP) & (e_lane < (g_idx + 1) * EXPERTS_PER_GROUP)
    el = jnp.where(in_grp, lg, neg)
    v1 = jnp.max(el, axis=-1, keepdims=True)
    i1 = first(el == v1)
    el2 = jnp.where(lane == i1, neg, el)
    v2 = jnp.max(el2, axis=-1, keepdims=True)
    i2 = first(el2 == v2)
    t = jnp.exp(v2 - v1)
    w1 = g_gate / (1.0 + t)
    w2 = g_gate * t / (1.0 + t)
    rt_ref[...] = jnp.where(lane == 0, (i1 - N_EXPERT_GROUPS).astype(F32),
                            jnp.where(lane == 1, (i2 - N_EXPERT_GROUPS).astype(F32),
                                      jnp.where(lane == 2, w1, jnp.where(lane == 3, w2, 0.0))))


def _out_proj(x2d, attn_n, hy, hy_norm_w, w_out_a, w_out_h, norm2_w, w_router, b_router,
              riders=(), tm=256):
    T = x2d.shape[0]
    _, NC, J, LJ, _ = hy.shape
    lt = (LJ * J) // tm
    row = lambda i: (i, 0)
    const = lambda i: (0, 0)
    ride_specs = _ride_along_specs(riders, T // tm, lambda i: i)
    outs = pl.pallas_call(
        functools.partial(_outproj_kernel, n_ride=len(riders)),
        grid=(T // tm,),
        in_specs=[
            pl.BlockSpec((tm, D_MODEL), row),
            pl.BlockSpec((tm, ATTN_WIDTH), row),
            pl.BlockSpec((1, NC, J, tm // J, LANES), lambda i: (i // lt, 0, 0, i % lt, 0)),
            pl.BlockSpec((1, HY_WIDTH), const),
            pl.BlockSpec((ATTN_WIDTH, D_MODEL), const),
            pl.BlockSpec((HY_WIDTH, D_MODEL), const),
            pl.BlockSpec((1, D_MODEL), const),
            pl.BlockSpec((D_MODEL, 2 * ROUTER_PAD), const),
            pl.BlockSpec((1, ROUTER_PAD), const),
        ] + ride_specs,
        out_specs=[
            pl.BlockSpec((tm, D_MODEL), row),
            pl.BlockSpec((tm, D_MODEL // 2), row),
            pl.BlockSpec((tm, ROUTER_PAD), row),
        ] + ride_specs,
        out_shape=[
            jax.ShapeDtypeStruct((T, D_MODEL), F32),
            jax.ShapeDtypeStruct((T, D_MODEL // 2), jnp.uint32),
            jax.ShapeDtypeStruct((T, ROUTER_PAD), F32),
        ] + [jax.ShapeDtypeStruct(r.shape, BF16) for r in riders],
        compiler_params=_cparams("parallel"),
        name="out_proj",
    )(x2d, attn_n, hy, hy_norm_w[None, :], w_out_a, w_out_h, norm2_w[None, :], w_router, b_router,
      *riders)
    return outs[0], outs[1], outs[2], outs[3:]


def _moe_kernel(be_ref, nu_ref, x_ref, wg_ref, wu_ref, wd_ref, o_ref):
    @pl.when(pl.program_id(0) < nu_ref[0])
    def _():
        half = D_MODEL // 2
        xa, xb = _unpack_bf16_pair(x_ref[...])
        xa, xb = xa.astype(BF16), xb.astype(BF16)
        g = _dot(xa, wg_ref[0, :half, :]) + _dot(xb, wg_ref[0, half:, :])
        u = _dot(xa, wu_ref[0, :half, :]) + _dot(xb, wu_ref[0, half:, :])
        hid = (g * jax.nn.sigmoid(g)) * u
        out = _dot(hid.astype(BF16), wd_ref[0])
        o_ref[...] = _pack_bf16_pair(out[:, :half], out[:, half:])


def _expert_blocks(xs, block_expert, n_used, w_gate, w_up, w_down):
    P = xs.shape[0]
    bm = MOE_BLOCK_ROWS
    blk = lambda b, be, nu: (jnp.minimum(b, nu[0] - 1), 0)
    wsel = lambda b, be, nu: (be[jnp.minimum(b, nu[0] - 1)], 0, 0)
    grid_spec = pltpu.PrefetchScalarGridSpec(
        num_scalar_prefetch=2,
        grid=(P // bm,),
        in_specs=[
            pl.BlockSpec((bm, D_MODEL // 2), blk),
            pl.BlockSpec((1, D_MODEL, D_EXPERT), wsel),
            pl.BlockSpec((1, D_MODEL, D_EXPERT), wsel),
            pl.BlockSpec((1, D_EXPERT, D_MODEL), wsel),
        ],
        out_specs=pl.BlockSpec((bm, D_MODEL // 2), blk),
    )
    return pl.pallas_call(
        _moe_kernel,
        grid_spec=grid_spec,
        out_shape=jax.ShapeDtypeStruct((P, D_MODEL // 2), jnp.uint32),
        compiler_params=_cparams("arbitrary"),
        name="moe_experts",
    )(block_expert, n_used, xs, w_gate, w_up, w_down)


def _hier_moe(x1, h2, route, w_gate, w_up, w_down):
    T = h2.shape[0]
    experts = route[:, :TOP_K].astype(jnp.int32)
    weights = route[:, TOP_K:2 * TOP_K]

    A = T * TOP_K
    bm = MOE_BLOCK_ROWS
    flat_e = experts.reshape(-1)
    onehot = (flat_e[:, None] == jnp.arange(N_EXPERTS, dtype=jnp.int32)[None, :]).astype(jnp.int32)
    csum = jnp.cumsum(onehot, axis=0)
    counts = csum[-1]
    rank = jnp.sum((csum - onehot) * onehot, axis=1)
    padded = ((counts + bm - 1) // bm) * bm
    pend = jnp.cumsum(padded)
    pstart = pend - padded
    seg_start = jnp.cumsum(counts) - counts
    dest = (pstart[flat_e] + rank).astype(jnp.int32)
    n_blocks = (A + bm - 1) // bm + N_EXPERTS
    P = n_blocks * bm
    order = jnp.argsort(flat_e, stable=True).astype(jnp.int32)
    q = jnp.arange(P, dtype=jnp.int32)
    e_q = jnp.minimum(jnp.sum((pend[None, :] <= q[:, None]).astype(jnp.int32), axis=1), N_EXPERTS - 1)
    r_q = jnp.minimum(q - pstart[e_q], jnp.maximum(counts[e_q] - 1, 0))
    src = jnp.clip(seg_start[e_q] + r_q, 0, A - 1)
    buf_tok = order.at[src].get(mode="promise_in_bounds") // TOP_K
    block_expert = e_q.reshape(n_blocks, bm)[:, 0]
    n_used = (pend[-1] // bm).astype(jnp.int32)[None]

    xs = h2.at[buf_tok].get(mode="promise_in_bounds")
    yb = _expert_blocks(xs, block_expert, n_used, w_gate, w_up, w_down)
    d = dest.reshape(T, TOP_K)
    y0 = jnp.concatenate(_unpack_bf16_pair(yb.at[d[:, 0]].get(mode="promise_in_bounds")), axis=1)
    y1 = jnp.concatenate(_unpack_bf16_pair(yb.at[d[:, 1]].get(mode="promise_in_bounds")), axis=1)
    return x1 + (weights[:, 0:1] * y0 + weights[:, 1:2] * y1)


def _encoder_layer(x, p):
    B, L, D = x.shape
    T = B * L
    x2d = x.reshape(T, D)
    need_cast = "experts_bf16" not in p
    ride_a, ride_o = (), ()
    if need_cast:
        wg, wu, wd = p["experts_f32"]
        ride_a = (wg.reshape(-1, D_EXPERT), wu.reshape(-1, D_EXPERT))
        ride_o = (wd.reshape(-1, D_MODEL),)
    q, k, v = _qkv_proj(x2d, L, p["norm1_w"], p["w_qkv"], p["q_norm_w"], p["k_norm_w"])
    attn_n, cast_a = _windowed_attention(q.reshape(B, L, ATTN_WIDTH), k.reshape(B, L, KV_WIDTH),
                                         v, p["attn_sink"], p["attn_out_norm_w"], riders=ride_a)
    uc = _u_proj_conv(x2d, L, p["norm1_w"], p["w_u"], p["conv_w"], p["conv_b"])
    tabs = _fft_tables(L)
    g_spec = _hyena_filter_spectrum(L, tabs, p["filt_w1"], p["filt_b1"], p["filt_w2"], p["filt_b2"],
                                    p["filt_w3"], p["filt_b3"], p["filt_w4"], p["filt_freq"])
    hy = _hyena_mixer(uc, g_spec, p["hy_bias"], tabs)
    x1, h2, route, cast_o = _out_proj(x2d, attn_n.reshape(T, ATTN_WIDTH), hy, p["hy_out_norm_w"],
                                      p["w_out_a"], p["w_out_h"], p["norm2_w"], p["w_router"],
                                      p["b_router"], riders=ride_o)
    if need_cast:
        p["experts_bf16"] = (cast_a[0].reshape(wg.shape), cast_a[1].reshape(wu.shape),
                             cast_o[0].reshape(wd.shape))
    out = _hier_moe(x1, h2, route, *p["experts_bf16"])
    return out.reshape(B, L, D)


def kernel(x_prompt, x_sample, norm1_w, w_in, q_norm_w, k_norm_w, attn_sink, conv_w, conv_b, filt_w1, filt_b1, filt_w2, filt_b2, filt_w3, filt_b3, filt_w4, filt_freq, hy_bias, attn_out_norm_w, hy_out_norm_w, w_out, norm2_w, w_route_group, b_route_group, w_route_expert, b_route_expert, w_gate, w_up, w_down):
    depth = norm1_w.shape[0]

    def layer_params(l):
        w_r = jnp.zeros((D_MODEL, ROUTER_PAD), F32)
        w_r = w_r.at[:, :N_EXPERT_GROUPS].set(w_route_group[l])
        w_r = w_r.at[:, N_EXPERT_GROUPS:N_EXPERT_GROUPS + N_EXPERTS].set(w_route_expert[l])
        r_hi, r_lo = _split_bf16(w_r)
        b_r = jnp.zeros((1, ROUTER_PAD), F32)
        b_r = b_r.at[0, :N_EXPERT_GROUPS].set(b_route_group[l])
        b_r = b_r.at[0, N_EXPERT_GROUPS:N_EXPERT_GROUPS + N_EXPERTS].set(b_route_expert[l])
        return dict(
            norm1_w=norm1_w[l], w_qkv=w_in[l][:, :QKV_WIDTH].astype(BF16),
            w_u=w_in[l][:, QKV_WIDTH:].astype(BF16), q_norm_w=q_norm_w[l], k_norm_w=k_norm_w[l],
            attn_sink=attn_sink[l], conv_w=conv_w[l], conv_b=conv_b[l],
            filt_w1=filt_w1[l], filt_b1=filt_b1[l], filt_w2=filt_w2[l], filt_b2=filt_b2[l],
            filt_w3=filt_w3[l], filt_b3=filt_b3[l], filt_w4=filt_w4[l], filt_freq=filt_freq[l],
            hy_bias=hy_bias[l], attn_out_norm_w=attn_out_norm_w[l], hy_out_norm_w=hy_out_norm_w[l],
            w_out_a=w_out[l][:ATTN_WIDTH].astype(BF16), w_out_h=w_out[l][ATTN_WIDTH:].astype(BF16),
            norm2_w=norm2_w[l], w_router=jnp.concatenate([r_hi, r_lo], axis=1), b_router=b_r,
            experts_f32=(w_gate[l], w_up[l], w_down[l]))

    params = [layer_params(l) for l in range(depth)]

    def trunk(x):
        for p in params:
            x = _encoder_layer(x, p)
        return x

    return (trunk(x_prompt), trunk(x_sample))
```

```python
import functools
import math

import jax
import jax.numpy as jnp
from jax import lax
from jax.experimental import pallas as pl
from jax.experimental.pallas import tpu as pltpu

F32 = jnp.float32
BF16 = jnp.bfloat16

D_MODEL = 2048
HEAD_DIM = 64
N_Q_HEADS = 16
N_KV_HEADS = 4
Q_PER_KV = N_Q_HEADS // N_KV_HEADS
ATTN_WIDTH = N_Q_HEADS * HEAD_DIM
KV_WIDTH = N_KV_HEADS * HEAD_DIM
QK_WIDTH = ATTN_WIDTH + KV_WIDTH
QKV_WIDTH = ATTN_WIDTH + 2 * KV_WIDTH
WINDOW = 128
ROT_DIM = HEAD_DIM // 4
ROPE_THETA = 500000.0
HY_WIDTH = D_MODEL - ATTN_WIDTH
HY_ORDER = 2
HY_EMB_DIM = 33
HY_FILTER_HIDDEN = 64
HY_DECAY_TARGET = 1e-2
HY_FAST_DECAY_PCT = 0.3
HY_SLOW_DECAY_PCT = 1.5
N_EXPERT_GROUPS = 4
EXPERTS_PER_GROUP = 8
N_EXPERTS = N_EXPERT_GROUPS * EXPERTS_PER_GROUP
TOP_K = 2
D_EXPERT = 1024
EPS = 1e-6

LANES = 128
SUBLANES = 8
FFT_N2 = 128
MOE_BLOCK_ROWS = 256
ROUTER_PAD = 128
VMEM_LIMIT = 56 * 1024 * 1024
NEG_BIG = -1e30
HEADS_PER_PASS = Q_PER_KV


def _cparams(*sem):
    return pltpu.CompilerParams(dimension_semantics=sem, vmem_limit_bytes=VMEM_LIMIT)


def _split_bf16(a):
    hi = a.astype(BF16)
    lo = (a - hi.astype(F32)).astype(BF16)
    return hi, lo


def _dot(a, b):
    return jnp.dot(a, b, preferred_element_type=F32)


def _dot3(a, b):
    a_hi, a_lo = _split_bf16(a)
    b_hi, b_lo = _split_bf16(b)
    return _dot(a_hi, b_hi) + _dot(a_lo, b_hi) + _dot(a_hi, b_lo)


def _qkv_kernel(x_ref, nw_ref, w_ref, seg_ref, hw_ref, rc_ref, rs1_ref, rs2_ref,
                q_ref, k_ref, v_ref):
    x = x_ref[...]
    inv = lax.rsqrt(jnp.mean(x * x, axis=-1, keepdims=True) + EPS)
    h = (x * inv * nw_ref[...]).astype(BF16)
    acc = _dot(h, w_ref[...])
    qk = acc[:, :QK_WIDTH]
    hi, lo = _split_bf16(qk * qk)
    seg = seg_ref[...]
    w = seg.shape[0]
    ms = jnp.concatenate(
        [_dot(hi[:, c * w:(c + 1) * w], seg) + _dot(lo[:, c * w:(c + 1) * w], seg)
         for c in range(QK_WIDTH // w)], axis=-1)
    xn = qk * lax.rsqrt(ms + EPS) * hw_ref[...]
    reps = QK_WIDTH // LANES
    rc = jnp.tile(rc_ref[...], (1, reps))
    rs1 = jnp.tile(rs1_ref[...], (1, reps))
    rs2 = jnp.tile(rs2_ref[...], (1, reps))
    half = ROT_DIM // 2
    y = xn * rc + pltpu.roll(xn, QK_WIDTH - half, 1) * rs1 + pltpu.roll(xn, half, 1) * rs2
    q_ref[...] = (y[:, :ATTN_WIDTH] * (HEAD_DIM ** -0.5)).astype(BF16)
    k_ref[...] = y[:, ATTN_WIDTH:].astype(BF16)
    v_ref[0] = acc[:, QK_WIDTH:].T.astype(BF16)


def _rope_tables(L):
    half = ROT_DIM // 2
    inv_freq = jnp.power(ROPE_THETA, -jnp.arange(half, dtype=F32) * 2.0 / ROT_DIM)
    ang = jnp.arange(L, dtype=F32)[:, None] * inv_freq[None, :]
    d = jnp.arange(LANES) % HEAD_DIM
    ang_l = ang[:, d % half]
    cos, sin = jnp.cos(ang_l), jnp.sin(ang_l)
    rc = jnp.where(d < ROT_DIM, cos, 1.0)
    rs1 = jnp.where(d < half, -sin, 0.0)
    rs2 = jnp.where((d >= half) & (d < ROT_DIM), sin, 0.0)
    return rc.astype(F32), rs1.astype(F32), rs2.astype(F32)


def _qkv_proj(x2d, L, norm1_w, w_qkv, q_norm_w, k_norm_w, tm=512):
    T = x2d.shape[0]
    seg_w = 256
    seg = (jnp.arange(seg_w)[:, None] // HEAD_DIM == jnp.arange(seg_w)[None, :] // HEAD_DIM)
    seg = (seg.astype(F32) / HEAD_DIM).astype(BF16)
    hw = jnp.concatenate([jnp.tile(q_norm_w, N_Q_HEADS), jnp.tile(k_norm_w, N_KV_HEADS)])[None, :]
    rc, rs1, rs2 = _rope_tables(L)
    lt = L // tm
    row = lambda i: (i, 0)
    const = lambda i: (0, 0)
    pos = lambda i: (i % lt, 0)
    return pl.pallas_call(
        _qkv_kernel,
        grid=(T // tm,),
        in_specs=[
            pl.BlockSpec((tm, D_MODEL), row),
            pl.BlockSpec((1, D_MODEL), const),
            pl.BlockSpec((D_MODEL, QKV_WIDTH), const),
            pl.BlockSpec((seg_w, seg_w), const),
            pl.BlockSpec((1, QK_WIDTH), const),
            pl.BlockSpec((tm, LANES), pos),
            pl.BlockSpec((tm, LANES), pos),
            pl.BlockSpec((tm, LANES), pos),
        ],
        out_specs=[
            pl.BlockSpec((tm, ATTN_WIDTH), row),
            pl.BlockSpec((tm, KV_WIDTH), row),
            pl.BlockSpec((1, KV_WIDTH, tm), lambda i: (i // lt, 0, i % lt)),
        ],
        out_shape=[
            jax.ShapeDtypeStruct((T, ATTN_WIDTH), BF16),
            jax.ShapeDtypeStruct((T, KV_WIDTH), BF16),
            jax.ShapeDtypeStruct((T // L, KV_WIDTH, L), BF16),
        ],
        compiler_params=_cparams("parallel"),
        name="qkv_proj",
    )(x2d, norm1_w[None, :], w_qkv, seg, hw, rc, rs1, rs2)


HALO = 16

def _uproj_kernel(x_ref, xp_ref, xn_ref, nw_ref, w_ref, cw_ref, cb_ref, o_ref, h_scr, *, lt):
    i = pl.program_id(0)
    tm = x_ref.shape[0]

    @pl.when(pl.program_id(1) == 0)
    def _():
        def normed(x):
            inv = lax.rsqrt(jnp.mean(x * x, axis=-1, keepdims=True) + EPS)
            return x * inv * nw_ref[...]

        li = i % lt
        hp = jnp.where(li > 0, normed(xp_ref[...]), 0.0)
        hn = jnp.where(li < lt - 1, normed(xn_ref[...]), 0.0)
        h_scr[0:HALO] = hp.astype(BF16)
        h_scr[HALO:HALO + tm] = normed(x_ref[...]).astype(BF16)
        h_scr[HALO + tm:2 * HALO + tm] = hn.astype(BF16)

    u = _dot(h_scr[...], w_ref[...])
    w = cw_ref[...]
    val = (w[0:1] * u[HALO - 1:HALO - 1 + tm] + w[1:2] * u[HALO:HALO + tm]
           + w[2:3] * u[HALO + 1:HALO + 1 + tm] + cb_ref[...])
    for n1 in range(tm // FFT_N2):
        for j in range(FFT_N2 // SUBLANES):
            r0 = n1 * FFT_N2 + j * SUBLANES
            for c in range(val.shape[1] // LANES):
                o_ref[0, 0, c, j, n1 * SUBLANES:(n1 + 1) * SUBLANES, :] = (
                    val[r0:r0 + SUBLANES, c * LANES:(c + 1) * LANES])


def _u_proj_conv(x2d, L, norm1_w, w_u, conv_w, conv_b, tm=1024):
    T = x2d.shape[0]
    C = HY_WIDTH
    nparts = w_u.shape[1] // C
    lt = L // tm
    hb = tm // HALO
    J = FFT_N2 // SUBLANES
    return pl.pallas_call(
        functools.partial(_uproj_kernel, lt=lt),
        grid=(T // tm, nparts),
        in_specs=[
            pl.BlockSpec((tm, D_MODEL), lambda i, j: (i, 0)),
            pl.BlockSpec((HALO, D_MODEL), lambda i, j: (jnp.maximum(i * hb - 1, 0), 0)),
            pl.BlockSpec((HALO, D_MODEL), lambda i, j: (jnp.minimum((i + 1) * hb, T // HALO - 1), 0)),
            pl.BlockSpec((1, D_MODEL), lambda i, j: (0, 0)),
            pl.BlockSpec((D_MODEL, C), lambda i, j: (0, j)),
            pl.BlockSpec((3, C), lambda i, j: (0, j)),
            pl.BlockSpec((1, C), lambda i, j: (0, j)),
        ],
        out_specs=pl.BlockSpec((1, 1, C // LANES, J, tm // J, LANES),
                               lambda i, j: (j, i // lt, 0, 0, i % lt, 0)),
        out_shape=jax.ShapeDtypeStruct((nparts, T // L, C // LANES, J, L // J, LANES), F32),
        scratch_shapes=[pltpu.VMEM((tm + 2 * HALO, D_MODEL), BF16)],
        compiler_params=_cparams("parallel", "arbitrary"),
        name="u_proj_conv",
    )(x2d, x2d, x2d, norm1_w[None, :], w_u, conv_w, conv_b[None, :])


def _fft_tables(L):
    N = 2 * L
    N2 = FFT_N2
    N1 = N // N2
    H = N1 // 2
    k1 = jnp.arange(N1, dtype=jnp.int32)
    n2 = jnp.arange(N2, dtype=jnp.int32)
    a1 = ((k1[:, None] * k1[None, :]) % N1).astype(F32) * (2.0 * math.pi / N1)
    c1, s1 = jnp.cos(a1), jnp.sin(a1)
    at = (n2[:, None] * k1[None, :]).astype(F32) * (2.0 * math.pi / N)
    ct, st = jnp.cos(at), jnp.sin(at)
    c = c1[None] * ct[:, :, None] - s1[None] * st[:, :, None]
    s = s1[None] * ct[:, :, None] + c1[None] * st[:, :, None]
    m_filt = jnp.concatenate([c, -s], axis=1).astype(BF16)
    ch, sh = c[:, :, :H], s[:, :, :H]
    m_fwd = jnp.concatenate([jnp.concatenate([ch, sh], axis=2),
                             jnp.concatenate([-sh, ch], axis=2)], axis=1)
    cT = c1[None, :H] * ct[:, None, :] - s1[None, :H] * st[:, None, :]
    sT = s1[None, :H] * ct[:, None, :] + c1[None, :H] * st[:, None, :]
    m_inv = jnp.concatenate([jnp.concatenate([cT, -sT], axis=2),
                             jnp.concatenate([sT, cT], axis=2)], axis=1)
    k2 = jnp.arange(N2, dtype=jnp.int32)
    ph = ((k2[:, None] * n2[None, :]) % N2).astype(F32) * (2.0 * math.pi / N2)
    c2, s2 = jnp.cos(ph), jnp.sin(ph)
    m2_fwd = jnp.concatenate([jnp.concatenate([c2, s2], axis=1),
                              jnp.concatenate([-s2, c2], axis=1)], axis=0)
    m2_inv = m2_fwd.T
    return dict(N1=N1, N2=N2, m_filt=m_filt, m_fwd=m_fwd.astype(BF16), m_inv=m_inv.astype(BF16),
                m2_fwd=m2_fwd.astype(BF16), m2_inv=m2_inv.astype(BF16))


def _filt1_kernel(w1_ref, b1_ref, w2_ref, b2_ref, w3_ref, b3_ref, w4_ref, a_ref, frl_ref, off_ref,
                  dl_ref, m_ref, o_ref, sabs_ref, *, L, N1, N2, nb):
    j = pl.program_id(0)
    C2 = HY_ORDER * HY_WIDTH
    H = N1 // 2
    rows = nb * N1
    ridx = lax.broadcasted_iota(jnp.int32, (rows, LANES), 0)
    lane = lax.broadcasted_iota(jnp.int32, (rows, LANES), 1)
    n2 = j * nb + ridx // N1
    n1 = ridx % N1
    r = n1 * N2 + n2
    lag = jnp.where(r < L, r, 2 * L - r)
    valid = (r != L).astype(F32)
    lagf = jnp.minimum(lag, L - 1).astype(F32)
    t = lagf / (L - 1)
    wpos = lagf * (2.0 * math.pi / L)
    phase = wpos * frl_ref[...] + off_ref[...]
    zemb = jnp.where(lane == 0, t, jnp.where(lane < HY_EMB_DIM, jnp.cos(phase), 0.0))
    half = rows // 2
    a = a_ref[...]
    h = jnp.concatenate([zemb[:half], zemb[half:]], axis=1)
    h = jnp.sin(a * (_dot3(h, w1_ref[...]) + b1_ref[...]))
    h = jnp.sin(a * (_dot3(h, w2_ref[...]) + b2_ref[...]))
    h = jnp.sin(a * (_dot3(h, w3_ref[...]) + b3_ref[...]))
    h_hi, h_lo = _split_bf16(h)
    decay = jnp.exp(-jnp.tile(t, (1, HY_WIDTH // LANES)) * dl_ref[...])
    decay = decay * jnp.tile(valid, (1, HY_WIDTH // LANES))
    decay2 = jnp.tile(decay, (1, HY_ORDER))

    @pl.when(j == 0)
    def _():
        sabs_ref[...] = jnp.zeros_like(sabs_ref)

    tot = jnp.zeros((8, C2), F32)
    for l in range(nb):
        side = (l * N1) // half
        r0 = l * N1 - side * half
        fs = slice(r0, r0 + H)
        bs = slice(r0 + H, r0 + N1)
        wf = w4_ref[side, :, :C2]
        wb = w4_ref[side, :, C2:]
        gf = _dot(h_hi[fs], wf) + _dot(h_lo[fs], wf)
        gb = _dot(h_hi[bs], wb) + _dot(h_lo[bs], wb)
        g = jnp.concatenate([gf, gb], axis=0) * decay2[l * N1:(l + 1) * N1]
        tot = tot + jnp.sum(jnp.abs(g).reshape(N1 // 8, 8, C2), axis=0)
        y = _dot(m_ref[l], g.astype(BF16))
        _store_tiles(lambda c: o_ref.at[c], _pack_complex(y[:N1], y[N1:]), C2 // LANES, l)
    sabs_ref[...] += tot


def _pack_complex(re, im):
    r = lax.bitcast_convert_type(re.astype(BF16).astype(F32), jnp.uint32)
    i = lax.bitcast_convert_type(im.astype(BF16).astype(F32), jnp.uint32)
    return r | (i >> 16)


def _unpack_complex(w):
    re = lax.bitcast_convert_type(w & jnp.uint32(0xFFFF0000), F32)
    im = lax.bitcast_convert_type(w << 16, F32)
    return re, im


_pack_bf16_pair = _pack_complex
_unpack_bf16_pair = _unpack_complex


def _store_tiles(dst, w, ncols, s):
    for c in range(ncols):
        d = dst(c)
        for t in range(w.shape[0] // SUBLANES):
            d[t, s * SUBLANES:(s + 1) * SUBLANES, :] = (
                w[t * SUBLANES:(t + 1) * SUBLANES, c * LANES:(c + 1) * LANES])


def _rows_strided(ref2d, s):
    return ref2d[pl.ds(s, ref2d.shape[0] // SUBLANES, stride=SUBLANES), :]


def _gather_cols(fn, ncols):
    return jnp.concatenate([fn(c) for c in range(ncols)], axis=1)


def _block_diag2(w):
    z = jnp.zeros_like(w)
    return jnp.concatenate([jnp.concatenate([w, z], axis=1), jnp.concatenate([z, w], axis=1)], axis=0)


def _hyena_filter_spectrum(L, tabs, filt_w1, filt_b1, filt_w2, filt_b2, filt_w3, filt_b3,
                           filt_w4, filt_freq):
    N1, N2 = tabs["N1"], tabs["N2"]
    N = N1 * N2
    C2 = HY_ORDER * HY_WIDTH
    FH = HY_FILTER_HIDDEN
    nb = max(2, 512 // N1)
    bands = (HY_EMB_DIM - 1) // 2
    fr = jnp.linspace(1e-4, bands - 1, bands, dtype=F32)
    frl = jnp.zeros((LANES,), F32).at[1:1 + bands].set(fr).at[1 + bands:1 + 2 * bands].set(fr)[None, :]
    off = jnp.zeros((LANES,), F32).at[1 + bands:1 + 2 * bands].set(0.5 * math.pi)[None, :]
    w1p = jnp.zeros((LANES, FH), F32).at[:HY_EMB_DIM].set(filt_w1)
    w4 = filt_w4.astype(BF16)
    w4z = jnp.zeros_like(w4)
    w4s = jnp.stack([jnp.concatenate([w4, w4z], axis=0), jnp.concatenate([w4z, w4], axis=0)])
    two = lambda v: jnp.tile(v, 2)[None, :]
    max_decay = math.log(HY_DECAY_TARGET) / HY_FAST_DECAY_PCT
    min_decay = math.log(HY_DECAY_TARGET) / HY_SLOW_DECAY_PCT
    deltas = jnp.abs(jnp.linspace(min_decay, max_decay, HY_WIDTH, dtype=F32))[None, :]
    full = lambda shape: pl.BlockSpec(shape, lambda j: (0,) * len(shape))
    a_g, sabs = pl.pallas_call(
        functools.partial(_filt1_kernel, L=L, N1=N1, N2=N2, nb=nb),
        grid=(N2 // nb,),
        in_specs=[
            full((2 * LANES, 2 * FH)), full((1, 2 * FH)),
            full((2 * FH, 2 * FH)), full((1, 2 * FH)),
            full((2 * FH, 2 * FH)), full((1, 2 * FH)),
            full((2, 2 * FH, 2 * C2)), full((1, 2 * FH)),
            full((1, LANES)), full((1, LANES)), full((1, HY_WIDTH)),
            pl.BlockSpec((nb, 2 * N1, N1), lambda j: (j, 0, 0)),
        ],
        out_specs=[
            pl.BlockSpec((C2 // LANES, N1 // SUBLANES, nb * SUBLANES, LANES), lambda j: (0, 0, j, 0)),
            pl.BlockSpec((8, C2), lambda j: (0, 0)),
        ],
        out_shape=[
            jax.ShapeDtypeStruct((C2 // LANES, N1 // SUBLANES, N2 * SUBLANES, LANES), jnp.uint32),
            jax.ShapeDtypeStruct((8, C2), F32),
        ],
        compiler_params=_cparams("arbitrary"),
        name="hyena_filter_stage1",
    )(_block_diag2(w1p), two(filt_b1), _block_diag2(filt_w2), two(filt_b2), _block_diag2(filt_w3),
      two(filt_b3), w4s, two(filt_freq), frl, off, deltas, tabs["m_filt"])
    scale = (1.0 / (jnp.sum(sabs, axis=0) * N))[None, :]
    return a_g, scale


def _fwd1_kernel(x_ref, m_ref, o_ref):
    cb = o_ref.shape[1]
    N1 = m_ref.shape[2]
    for s in range(SUBLANES):
        rows = [_gather_cols(lambda c: _rows_strided(x_ref.at[0, bi, c, 0], s), cb) for bi in range(2)]
        xl = jnp.concatenate(rows, axis=0).astype(BF16)
        y = _dot(m_ref[s], xl)
        _store_tiles(lambda c: o_ref.at[0, c], _pack_complex(y[:N1], y[N1:]), cb, s)


def _hyena_stage1(z, which, tabs):
    N1, N2 = tabs["N1"], tabs["N2"]
    H = N1 // 2
    B, NC = z.shape[1], z.shape[2]
    cb = min(NC, 4)
    x_spec = pl.BlockSpec((1, 2, cb, 1, H * SUBLANES, LANES), lambda p, c, j: (which, p, c, j, 0, 0))
    return pl.pallas_call(
        _fwd1_kernel,
        grid=(B // 2, NC // cb, N2 // SUBLANES),
        in_specs=[x_spec, pl.BlockSpec((SUBLANES, 2 * N1, N1), lambda p, c, j: (j, 0, 0))],
        out_specs=pl.BlockSpec((1, cb, N1 // SUBLANES, SUBLANES * SUBLANES, LANES),
                               lambda p, c, j: (p, c, 0, j, 0)),
        out_shape=jax.ShapeDtypeStruct((B // 2, NC, N1 // SUBLANES, N2 * SUBLANES, LANES), jnp.uint32),
        compiler_params=_cparams("parallel", "parallel", "parallel"),
        name="hyena_stage1",
    )(z, tabs["m_fwd"])


def _mid_kernel(a_ref, ag_ref, sc_ref, mf_ref, mi_ref, o_ref, g_scr):
    cb = a_ref.shape[1]
    N2 = a_ref.shape[3] // SUBLANES

    @pl.when(pl.program_id(2) == 0)
    def _():
        for s in range(SUBLANES):
            re, im = _unpack_complex(_gather_cols(lambda c: _rows_strided(ag_ref.at[c, 0], s), cb))
            ag = jnp.concatenate([re, im], axis=0).astype(BF16)
            g_scr[s] = _dot(mf_ref[...], ag) * sc_ref[...]

    for s in range(SUBLANES):
        re, im = _unpack_complex(_gather_cols(lambda c: _rows_strided(a_ref.at[0, c, 0], s), cb))
        a = jnp.concatenate([re, im], axis=0).astype(BF16)
        x = _dot(mf_ref[...], a)
        xr, xi = x[:N2], x[N2:]
        gr, gi = g_scr[s, :N2], g_scr[s, N2:]
        y = jnp.concatenate([xr * gr - xi * gi, xr * gi + xi * gr], axis=0).astype(BF16)
        b = _dot(mi_ref[...], y)
        _store_tiles(lambda c: o_ref.at[0, c], _pack_complex(b[:N2], b[N2:]), cb, s)


def _hyena_stage2(a, a_g, scale, order, tabs, cb=8):
    N1, N2 = tabs["N1"], tabs["N2"]
    P, NC = a.shape[:2]
    cb = min(cb, NC)
    ncb = NC // cb
    return pl.pallas_call(
        _mid_kernel,
        grid=(N1 // SUBLANES, ncb, P),
        in_specs=[
            pl.BlockSpec((1, cb, 1, N2 * SUBLANES, LANES), lambda i, c, p: (p, c, i, 0, 0)),
            pl.BlockSpec((cb, 1, N2 * SUBLANES, LANES), lambda i, c, p: (order * ncb + c, i, 0, 0)),
            pl.BlockSpec((1, cb * LANES), lambda i, c, p: (0, order * ncb + c)),
            pl.BlockSpec((2 * N2, 2 * N2), lambda i, c, p: (0, 0)),
            pl.BlockSpec((2 * N2, 2 * N2), lambda i, c, p: (0, 0)),
        ],
        out_specs=pl.BlockSpec((1, cb, N2 // SUBLANES, SUBLANES * SUBLANES, LANES),
                               lambda i, c, p: (p, c, 0, i, 0)),
        out_shape=jax.ShapeDtypeStruct((P, NC, N2 // SUBLANES, N1 * SUBLANES, LANES), jnp.uint32),
        scratch_shapes=[pltpu.VMEM((SUBLANES, 2 * N2, cb * LANES), F32)],
        compiler_params=_cparams("parallel", "parallel", "arbitrary"),
        name="hyena_stage2",
    )(a, a_g, scale, tabs["m2_fwd"], tabs["m2_inv"])


def _inv1_kernel(b_ref, m_ref, gate_ref, z_ref, bias_ref, *rest, z_slabbed, final):
    if final:
        (o_ref,) = rest
    else:
        mf_ref, o_ref, a_ref = rest
    cb = b_ref.shape[1]
    N1 = m_ref.shape[1]
    H = N1 // 2
    bias = bias_ref[...]
    for s in range(SUBLANES):
        br, bi_ = _unpack_complex(_gather_cols(lambda c: _rows_strided(b_ref.at[0, c, 0], s), cb))
        b = jnp.concatenate([br, bi_], axis=0).astype(BF16)
        conv = _dot(m_ref[s], b)
        outs = []
        for bi in range(2):
            gate = _gather_cols(lambda c: _rows_strided(gate_ref.at[0, bi, c, 0], s), cb)
            if z_slabbed:
                z = _gather_cols(lambda c: _rows_strided(z_ref.at[0, bi, c, 0], s), cb)
            else:
                z = _gather_cols(lambda c: z_ref[bi, c, s], cb)
            out = gate * (conv[bi * H:(bi + 1) * H] + z * bias)
            outs.append(out)
            for c in range(cb):
                piece = out[:, c * LANES:(c + 1) * LANES]
                if final:
                    o_ref.at[bi, c, 0][pl.ds(s, H, stride=SUBLANES), :] = piece
                else:
                    o_ref[bi, c, s] = piece
        if not final:
            xl = jnp.concatenate(outs, axis=0).astype(BF16)
            y = _dot(mf_ref[s], xl)
            _store_tiles(lambda c: a_ref.at[0, c], _pack_complex(y[:N1], y[N1:]), cb, s)


def _hyena_stage3(bq, ucl, gate_idx, z, z_idx, bias, tabs, final):
    N1, N2 = tabs["N1"], tabs["N2"]
    H = N1 // 2
    P, NC = bq.shape[:2]
    cb = min(NC, 4 if final else max(1, 512 // N1))
    z_slabbed = z_idx is not None
    slab_blk = (1, 2, cb, 1, H * SUBLANES, LANES)
    n2m_blk = (2, cb, SUBLANES, H, LANES)
    if z_slabbed:
        z_spec = pl.BlockSpec(slab_blk, lambda p, c, j: (z_idx, p, c, j, 0, 0))
    else:
        z_spec = pl.BlockSpec(n2m_blk, lambda p, c, j: (p, c, j, 0, 0))
    in_specs = [
        pl.BlockSpec((1, cb, 1, N1 * SUBLANES, LANES), lambda p, c, j: (p, c, j, 0, 0)),
        pl.BlockSpec((SUBLANES, N1, 2 * N1), lambda p, c, j: (j, 0, 0)),
        pl.BlockSpec(slab_blk, lambda p, c, j: (gate_idx, p, c, j, 0, 0)),
        z_spec,
        pl.BlockSpec((1, cb * LANES), lambda p, c, j: (0, c)),
    ]
    args = [bq, tabs["m_inv"], ucl, z, bias[None, :]]
    if final:
        out_specs = pl.BlockSpec((2, cb, 1, H * SUBLANES, LANES), lambda p, c, j: (p, c, j, 0, 0))
        out_shape = jax.ShapeDtypeStruct((2 * P, NC, N2 // SUBLANES, H * SUBLANES, LANES), F32)
    else:
        in_specs.append(pl.BlockSpec((SUBLANES, 2 * N1, N1), lambda p, c, j: (j, 0, 0)))
        args.append(tabs["m_fwd"])
        out_specs = [
            pl.BlockSpec(n2m_blk, lambda p, c, j: (p, c, j, 0, 0)),
            pl.BlockSpec((1, cb, N1 // SUBLANES, SUBLANES * SUBLANES, LANES),
                         lambda p, c, j: (p, c, 0, j, 0)),
        ]
        out_shape = [
            jax.ShapeDtypeStruct((2 * P, NC, N2, H, LANES), F32),
            jax.ShapeDtypeStruct((P, NC, N1 // SUBLANES, N2 * SUBLANES, LANES), jnp.uint32),
        ]
    return pl.pallas_call(
        functools.partial(_inv1_kernel, z_slabbed=z_slabbed, final=final),
        grid=(P, NC // cb, N2 // SUBLANES),
        in_specs=in_specs,
        out_specs=out_specs,
        out_shape=out_shape,
        compiler_params=_cparams("parallel", "parallel", "parallel"),
        name="hyena_stage3",
    )(*args)


def _hyena_mixer(ucl, g_spec, hy_bias, tabs):
    a_g, scale = g_spec
    z, z_idx = ucl, 2
    a = _hyena_stage1(z, z_idx, tabs)
    for order in range(HY_ORDER):
        bq = _hyena_stage2(a, a_g, scale, order, tabs)
        if order == HY_ORDER - 1:
            return _hyena_stage3(bq, ucl, order, z, z_idx, hy_bias[order], tabs, final=True)
        z, a = _hyena_stage3(bq, ucl, order, z, z_idx, hy_bias[order], tabs, final=False)
        z_idx = None


def _ride_along_specs(riders, steps, index):
    specs = []
    for r in riders:
        rows = r.shape[0] // steps
        specs.append(pl.BlockSpec((rows, r.shape[1]), lambda *g, _i=index: (_i(*g), 0)))
    return specs


def _ride_along_cast(in_refs, out_refs):
    for x_ref, o_ref in zip(in_refs, out_refs):
        o_ref[...] = x_ref[...].astype(o_ref.dtype)


def _attn_kernel(sink_ref, q_ref, kp_ref, kc_ref, kn_ref, vp_ref, vc_ref, vn_ref, nw_ref,
                 *rest, L, n_ride):
    ride_in, o_ref, ride_out, y_scr = rest[:n_ride], rest[n_ride], rest[n_ride + 1:-1], rest[-1]
    _ride_along_cast(ride_in, ride_out)
    i = pl.program_id(1)
    tq = q_ref.shape[1]
    QB = WINDOW
    span = QB + 2 * WINDOW
    k = jnp.concatenate([kp_ref[0], kc_ref[0], kn_ref[0]], axis=0)
    vt = jnp.concatenate([vp_ref[0], vc_ref[0], vn_ref[0]], axis=1)
    key = lax.broadcasted_iota(jnp.int32, (span, QB), 0)
    qry = lax.broadcasted_iota(jnp.int32, (span, QB), 1)
    rel = key - qry
    band = (rel >= 0) & (rel <= 2 * WINDOW)
    for sub in range(tq // QB):
        r0 = sub * QB
        kpos = i * tq + r0 - WINDOW + key
        ok = band & (kpos >= 0) & (kpos < L)
        bias = jnp.where(ok, 0.0, NEG_BIG).astype(F32)
        for h in range(N_KV_HEADS):
            kh = k[r0:r0 + span, h * HEAD_DIM:(h + 1) * HEAD_DIM]
            vht = vt[h * HEAD_DIM:(h + 1) * HEAD_DIM, r0:r0 + span]
            for g0 in range(0, Q_PER_KV, HEADS_PER_PASS):
                heads = [h * Q_PER_KV + g0 + g for g in range(HEADS_PER_PASS)]
                qh = jnp.concatenate(
                    [q_ref[0, r0:r0 + QB, a * HEAD_DIM:(a + 1) * HEAD_DIM] for a in heads], axis=0)
                s = lax.dot_general(kh, qh, (((1,), (1,)), ((), ())), preferred_element_type=F32)
                s = s + jnp.tile(bias, (1, HEADS_PER_PASS))
                sink = jnp.concatenate([jnp.full((1, QB), sink_ref[a], F32) for a in heads], axis=1)
                m = jnp.maximum(jnp.max(s, axis=0, keepdims=True), sink)
                p = jnp.exp(s - m)
                denom = jnp.sum(p, axis=0, keepdims=True) + jnp.exp(sink - m)
                ot = _dot(vht, p.astype(BF16)) / denom
                for g, a in enumerate(heads):
                    y_scr[a * HEAD_DIM:(a + 1) * HEAD_DIM, r0:r0 + QB] = ot[:, g * QB:(g + 1) * QB]
    yt = y_scr[...]
    inv = lax.rsqrt(jnp.mean(yt * yt, axis=0, keepdims=True) + EPS)
    o_ref[0] = ((yt * inv).T * nw_ref[...]).astype(o_ref.dtype)


def _windowed_attention(q, k, vt, sink, out_norm_w, riders=(), tq=256):
    B, L, _ = q.shape
    nq = L // tq
    ride_specs = _ride_along_specs(riders, B * nq, lambda b, i, s: b * nq + i)
    rq = tq // WINDOW
    nwb = L // WINDOW
    cur = lambda b, i, s: (b, i, 0)
    prv = lambda b, i, s: (b, jnp.maximum(i * rq - 1, 0), 0)
    nxt = lambda b, i, s: (b, jnp.minimum((i + 1) * rq, nwb - 1), 0)
    cur_t = lambda b, i, s: (b, 0, i)
    prv_t = lambda b, i, s: (b, 0, jnp.maximum(i * rq - 1, 0))
    nxt_t = lambda b, i, s: (b, 0, jnp.minimum((i + 1) * rq, nwb - 1))
    grid_spec = pltpu.PrefetchScalarGridSpec(
        num_scalar_prefetch=1,
        grid=(B, L // tq),
        in_specs=[
            pl.BlockSpec((1, tq, ATTN_WIDTH), cur),
            pl.BlockSpec((1, WINDOW, KV_WIDTH), prv),
            pl.BlockSpec((1, tq, KV_WIDTH), cur),
            pl.BlockSpec((1, WINDOW, KV_WIDTH), nxt),
            pl.BlockSpec((1, KV_WIDTH, WINDOW), prv_t),
            pl.BlockSpec((1, KV_WIDTH, tq), cur_t),
            pl.BlockSpec((1, KV_WIDTH, WINDOW), nxt_t),
            pl.BlockSpec((1, ATTN_WIDTH), lambda b, i, s: (0, 0)),
        ] + ride_specs,
        out_specs=[pl.BlockSpec((1, tq, ATTN_WIDTH), cur)] + ride_specs,
        scratch_shapes=[pltpu.VMEM((ATTN_WIDTH, tq), F32)],
    )
    outs = pl.pallas_call(
        functools.partial(_attn_kernel, L=L, n_ride=len(riders)),
        grid_spec=grid_spec,
        out_shape=[jax.ShapeDtypeStruct((B, L, ATTN_WIDTH), BF16)]
        + [jax.ShapeDtypeStruct(r.shape, BF16) for r in riders],
        compiler_params=_cparams("parallel", "parallel"),
        name="banded_attention",
    )(sink.astype(F32), q, k, k, k, vt, vt, vt, out_norm_w[None, :], *riders)
    return outs[0], outs[1:]


def _outproj_kernel(x_ref, a_ref, hy_ref, hw_ref, wa_ref, wh_ref, nw_ref, wr_ref, rb_ref,
                    *rest, n_ride):
    ride_in, (x1_ref, h2_ref, rt_ref), ride_out = rest[:n_ride], rest[n_ride:n_ride + 3], rest[n_ride + 3:]
    _ride_along_cast(ride_in, ride_out)
    nslab = hy_ref.shape[2]
    hy = jnp.concatenate(
        [_gather_cols(lambda c: hy_ref[0, c, j, n1 * SUBLANES:(n1 + 1) * SUBLANES, :], hy_ref.shape[1])
         for n1 in range(hy_ref.shape[3] // SUBLANES) for j in range(nslab)],
        axis=0)
    hinv = lax.rsqrt(jnp.mean(hy * hy, axis=-1, keepdims=True) + EPS)
    hy_n = (hy * hinv * hw_ref[...]).astype(BF16)
    x1 = x_ref[...] + _dot(a_ref[...], wa_ref[...]) + _dot(hy_n, wh_ref[...])
    x1_ref[...] = x1
    inv = lax.rsqrt(jnp.mean(x1 * x1, axis=-1, keepdims=True) + EPS)
    h2 = x1 * inv * nw_ref[...]
    h2_ref[...] = _pack_bf16_pair(h2[:, :D_MODEL // 2], h2[:, D_MODEL // 2:])
    h_hi, h_lo = _split_bf16(h2)
    both = _dot(h_hi, wr_ref[...])
    lg = (both[:, :ROUTER_PAD] + both[:, ROUTER_PAD:] + _dot(h_lo, wr_ref[:, :ROUTER_PAD])
          + rb_ref[...])
    lane = lax.broadcasted_iota(jnp.int32, lg.shape, 1)
    neg = jnp.float32(-jnp.inf)
    first = lambda mask: jnp.min(jnp.where(mask, lane, ROUTER_PAD), axis=-1, keepdims=True)
    gl = jnp.where(lane < N_EXPERT_GROUPS, lg, neg)
    gmax = jnp.max(gl, axis=-1, keepdims=True)
    g_idx = first(gl == gmax)
    g_gate = 1.0 / jnp.sum(jnp.exp(gl - gmax), axis=-1, keepdims=True)
    e_lane = lane - N_EXPERT_GROUPS
    in_grp = (e_lane >= g_idx * EXPERTS_PER_GROUP) & (e_lane < (g_idx + 1) * EXPERTS_PER_GROUP)
    el = jnp.where(in_grp, lg, neg)
    v1 = jnp.max(el, axis=-1, keepdims=True)
    i1 = first(el == v1)
    el2 = jnp.where(lane == i1, neg, el)
    v2 = jnp.max(el2, axis=-1, keepdims=True)
    i2 = first(el2 == v2)
    t = jnp.exp(v2 - v1)
    w1 = g_gate / (1.0 + t)
    w2 = g_gate * t / (1.0 + t)
    rt_ref[...] = jnp.where(lane == 0, (i1 - N_EXPERT_GROUPS).astype(F32),
                            jnp.where(lane == 1, (i2 - N_EXPERT_GROUPS).astype(F32),
                                      jnp.where(lane == 2, w1, jnp.where(lane == 3, w2, 0.0))))


def _out_proj(x2d, attn_n, hy, hy_norm_w, w_out_a, w_out_h, norm2_w, w_router, b_router,
              riders=(), tm=256):
    T = x2d.shape[0]
    _, NC, J, LJ, _ = hy.shape
    lt = (LJ * J) // tm
    row = lambda i: (i, 0)
    const = lambda i: (0, 0)
    ride_specs = _ride_along_specs(riders, T // tm, lambda i: i)
    outs = pl.pallas_call(
        functools.partial(_outproj_kernel, n_ride=len(riders)),
        grid=(T // tm,),
        in_specs=[
            pl.BlockSpec((tm, D_MODEL), row),
            pl.BlockSpec((tm, ATTN_WIDTH), row),
            pl.BlockSpec((1, NC, J, tm // J, LANES), lambda i: (i // lt, 0, 0, i % lt, 0)),
            pl.BlockSpec((1, HY_WIDTH), const),
            pl.BlockSpec((ATTN_WIDTH, D_MODEL), const),
            pl.BlockSpec((HY_WIDTH, D_MODEL), const),
            pl.BlockSpec((1, D_MODEL), const),
            pl.BlockSpec((D_MODEL, 2 * ROUTER_PAD), const),
            pl.BlockSpec((1, ROUTER_PAD), const),
        ] + ride_specs,
        out_specs=[
            pl.BlockSpec((tm, D_MODEL), row),
            pl.BlockSpec((tm, D_MODEL // 2), row),
            pl.BlockSpec((tm, ROUTER_PAD), row),
        ] + ride_specs,
        out_shape=[
            jax.ShapeDtypeStruct((T, D_MODEL), F32),
            jax.ShapeDtypeStruct((T, D_MODEL // 2), jnp.uint32),
            jax.ShapeDtypeStruct((T, ROUTER_PAD), F32),
        ] + [jax.ShapeDtypeStruct(r.shape, BF16) for r in riders],
        compiler_params=_cparams("parallel"),
        name="out_proj",
    )(x2d, attn_n, hy, hy_norm_w[None, :], w_out_a, w_out_h, norm2_w[None, :], w_router, b_router,
      *riders)
    return outs[0], outs[1], outs[2], outs[3:]


def _moe_kernel(be_ref, nu_ref, x_ref, wg_ref, wu_ref, wd_ref, o_ref):
    @pl.when(pl.program_id(0) < nu_ref[0])
    def _():
        half = D_MODEL // 2
        xa, xb = _unpack_bf16_pair(x_ref[...])
        xa, xb = xa.astype(BF16), xb.astype(BF16)
        g = _dot(xa, wg_ref[0, :half, :]) + _dot(xb, wg_ref[0, half:, :])
        u = _dot(xa, wu_ref[0, :half, :]) + _dot(xb, wu_ref[0, half:, :])
        hid = (g * jax.nn.sigmoid(g)) * u
        out = _dot(hid.astype(BF16), wd_ref[0])
        o_ref[...] = _pack_bf16_pair(out[:, :half], out[:, half:])


def _expert_blocks(xs, block_expert, n_used, w_gate, w_up, w_down):
    P = xs.shape[0]
    bm = MOE_BLOCK_ROWS
    blk = lambda b, be, nu: (jnp.minimum(b, nu[0] - 1), 0)
    wsel = lambda b, be, nu: (be[jnp.minimum(b, nu[0] - 1)], 0, 0)
    grid_spec = pltpu.PrefetchScalarGridSpec(
        num_scalar_prefetch=2,
        grid=(P // bm,),
        in_specs=[
            pl.BlockSpec((bm, D_MODEL // 2), blk),
            pl.BlockSpec((1, D_MODEL, D_EXPERT), wsel),
            pl.BlockSpec((1, D_MODEL, D_EXPERT), wsel),
            pl.BlockSpec((1, D_EXPERT, D_MODEL), wsel),
        ],
        out_specs=pl.BlockSpec((bm, D_MODEL // 2), blk),
    )
    return pl.pallas_call(
        _moe_kernel,
        grid_spec=grid_spec,
        out_shape=jax.ShapeDtypeStruct((P, D_MODEL // 2), jnp.uint32),
        compiler_params=_cparams("arbitrary"),
        name="moe_experts",
    )(block_expert, n_used, xs, w_gate, w_up, w_down)


def _hier_moe(x1, h2, route, w_gate, w_up, w_down):
    T = h2.shape[0]
    experts = route[:, :TOP_K].astype(jnp.int32)

    A = T * TOP_K
    bm = MOE_BLOCK_ROWS
    flat_e = experts.reshape(-1)
    onehot = (flat_e[:, None] == jnp.arange(N_EXPERTS, dtype=jnp.int32)[None, :]).astype(jnp.int32)
    csum = jnp.cumsum(onehot, axis=0)
    counts = csum[-1]
    rank = jnp.sum((csum - onehot) * onehot, axis=1)
    padded = ((counts + bm - 1) // bm) * bm
    pend = jnp.cumsum(padded)
    pstart = pend - padded
    seg_start = jnp.cumsum(counts) - counts
    dest = (pstart[flat_e] + rank).astype(jnp.int32)
    n_blocks = (A + bm - 1) // bm + N_EXPERTS
    P = n_blocks * bm
    order = jnp.argsort(flat_e, stable=True).astype(jnp.int32)
    q = jnp.arange(P, dtype=jnp.int32)
    e_q = jnp.minimum(jnp.sum((pend[None, :] <= q[:, None]).astype(jnp.int32), axis=1), N_EXPERTS - 1)
    r_q = jnp.minimum(q - pstart[e_q], jnp.maximum(counts[e_q] - 1, 0))
    src = jnp.clip(seg_start[e_q] + r_q, 0, A - 1)
    buf_tok = order.at[src].get(mode="promise_in_bounds") // TOP_K
    block_expert = e_q.reshape(n_blocks, bm)[:, 0]
    n_used = (pend[-1] // bm).astype(jnp.int32)[None]

    xs = h2.at[buf_tok].get(mode="promise_in_bounds")
    yb = _expert_blocks(xs, block_expert, n_used, w_gate, w_up, w_down)
    d = dest.reshape(T, TOP_K)
    y0 = yb.at[d[:, 0]].get(mode="promise_in_bounds")
    y1 = yb.at[d[:, 1]].get(mode="promise_in_bounds")
    return _moe_combine(x1, y0, y1, route)


def _combine_kernel(x1_ref, y0_ref, y1_ref, rt_ref, o_ref):
    half = D_MODEL // 2
    rt = rt_ref[...]
    w0, w1 = rt[:, TOP_K:TOP_K + 1], rt[:, TOP_K + 1:TOP_K + 2]
    a0, b0 = _unpack_bf16_pair(y0_ref[...])
    a1, b1 = _unpack_bf16_pair(y1_ref[...])
    o_ref[:, :half] = x1_ref[:, :half] + (w0 * a0 + w1 * a1)
    o_ref[:, half:] = x1_ref[:, half:] + (w0 * b0 + w1 * b1)


def _moe_combine(x1, y0, y1, route, tm=512):
    T = x1.shape[0]
    row = lambda i: (i, 0)
    return pl.pallas_call(
        _combine_kernel,
        grid=(T // tm,),
        in_specs=[
            pl.BlockSpec((tm, D_MODEL), row),
            pl.BlockSpec((tm, D_MODEL // 2), row),
            pl.BlockSpec((tm, D_MODEL // 2), row),
            pl.BlockSpec((tm, ROUTER_PAD), row),
        ],
        out_specs=pl.BlockSpec((tm, D_MODEL), row),
        out_shape=jax.ShapeDtypeStruct((T, D_MODEL), F32),
        compiler_params=_cparams("parallel"),
        name="moe_combine",
    )(x1, y0, y1, route)


def _encoder_layer(x, p):
    B, L, D = x.shape
    T = B * L
    x2d = x.reshape(T, D)
    need_cast = "experts_bf16" not in p
    ride_a, ride_o = (), ()
    if need_cast:
        wg, wu, wd = p["experts_f32"]
        ride_a = (wg.reshape(-1, D_EXPERT), wu.reshape(-1, D_EXPERT))
        ride_o = (wd.reshape(-1, D_MODEL),)
    q, k, v = _qkv_proj(x2d, L, p["norm1_w"], p["w_qkv"], p["q_norm_w"], p["k_norm_w"])
    attn_n, cast_a = _windowed_attention(q.reshape(B, L, ATTN_WIDTH), k.reshape(B, L, KV_WIDTH),
                                         v, p["attn_sink"], p["attn_out_norm_w"], riders=ride_a)
    uc = _u_proj_conv(x2d, L, p["norm1_w"], p["w_u"], p["conv_w"], p["conv_b"])
    tabs = _fft_tables(L)
    g_spec = _hyena_filter_spectrum(L, tabs, p["filt_w1"], p["filt_b1"], p["filt_w2"], p["filt_b2"],
                                    p["filt_w3"], p["filt_b3"], p["filt_w4"], p["filt_freq"])
    hy = _hyena_mixer(uc, g_spec, p["hy_bias"], tabs)
    x1, h2, route, cast_o = _out_proj(x2d, attn_n.reshape(T, ATTN_WIDTH), hy, p["hy_out_norm_w"],
                                      p["w_out_a"], p["w_out_h"], p["norm2_w"], p["w_router"],
                                      p["b_router"], riders=ride_o)
    if need_cast:
        p["experts_bf16"] = (cast_a[0].reshape(wg.shape), cast_a[1].reshape(wu.shape),
                             cast_o[0].reshape(wd.shape))
    out = _hier_moe(x1, h2, route, *p["experts_bf16"])
    return out.reshape(B, L, D)


def kernel(x_prompt, x_sample, norm1_w, w_in, q_norm_w, k_norm_w, attn_sink, conv_w, conv_b, filt_w1, filt_b1, filt_w2, filt_b2, filt_w3, filt_b3, filt_w4, filt_freq, hy_bias, attn_out_norm_w, hy_out_norm_w, w_out, norm2_w, w_route_group, b_route_group, w_route_expert, b_route_expert, w_gate, w_up, w_down):
    depth = norm1_w.shape[0]

    def layer_params(l):
        w_r = jnp.zeros((D_MODEL, ROUTER_PAD), F32)
        w_r = w_r.at[:, :N_EXPERT_GROUPS].set(w_route_group[l])
        w_r = w_r.at[:, N_EXPERT_GROUPS:N_EXPERT_GROUPS + N_EXPERTS].set(w_route_expert[l])
        r_hi, r_lo = _split_bf16(w_r)
        b_r = jnp.zeros((1, ROUTER_PAD), F32)
        b_r = b_r.at[0, :N_EXPERT_GROUPS].set(b_route_group[l])
        b_r = b_r.at[0, N_EXPERT_GROUPS:N_EXPERT_GROUPS + N_EXPERTS].set(b_route_expert[l])
        return dict(
            norm1_w=norm1_w[l], w_qkv=w_in[l][:, :QKV_WIDTH].astype(BF16),
            w_u=w_in[l][:, QKV_WIDTH:].astype(BF16), q_norm_w=q_norm_w[l], k_norm_w=k_norm_w[l],
            attn_sink=attn_sink[l], conv_w=conv_w[l], conv_b=conv_b[l],
            filt_w1=filt_w1[l], filt_b1=filt_b1[l], filt_w2=filt_w2[l], filt_b2=filt_b2[l],
            filt_w3=filt_w3[l], filt_b3=filt_b3[l], filt_w4=filt_w4[l], filt_freq=filt_freq[l],
            hy_bias=hy_bias[l], attn_out_norm_w=attn_out_norm_w[l], hy_out_norm_w=hy_out_norm_w[l],
            w_out_a=w_out[l][:ATTN_WIDTH].astype(BF16), w_out_h=w_out[l][ATTN_WIDTH:].astype(BF16),
            norm2_w=norm2_w[l], w_router=jnp.concatenate([r_hi, r_lo], axis=1), b_router=b_r,
            experts_f32=(w_gate[l], w_up[l], w_down[l]))

    params = [layer_params(l) for l in range(depth)]

    def trunk(x):
        for p in params:
            x = _encoder_layer(x, p)
        return x

    return (trunk(x_prompt), trunk(x_sample))
```

```python
import functools
import math

import jax
import jax.numpy as jnp
from jax import lax
from jax.experimental import pallas as pl
from jax.experimental.pallas import tpu as pltpu

F32 = jnp.float32
BF16 = jnp.bfloat16

D_MODEL = 2048
HEAD_DIM = 64
N_Q_HEADS = 16
N_KV_HEADS = 4
Q_PER_KV = N_Q_HEADS // N_KV_HEADS
ATTN_WIDTH = N_Q_HEADS * HEAD_DIM
KV_WIDTH = N_KV_HEADS * HEAD_DIM
QK_WIDTH = ATTN_WIDTH + KV_WIDTH
QKV_WIDTH = ATTN_WIDTH + 2 * KV_WIDTH
WINDOW = 128
ROT_DIM = HEAD_DIM // 4
ROPE_THETA = 500000.0
HY_WIDTH = D_MODEL - ATTN_WIDTH
HY_ORDER = 2
HY_EMB_DIM = 33
HY_FILTER_HIDDEN = 64
HY_DECAY_TARGET = 1e-2
HY_FAST_DECAY_PCT = 0.3
HY_SLOW_DECAY_PCT = 1.5
N_EXPERT_GROUPS = 4
EXPERTS_PER_GROUP = 8
N_EXPERTS = N_EXPERT_GROUPS * EXPERTS_PER_GROUP
TOP_K = 2
D_EXPERT = 1024
EPS = 1e-6

LANES = 128
SUBLANES = 8
FFT_N2 = 128
MOE_BLOCK_ROWS = 256
ROUTER_PAD = 128
VMEM_LIMIT = 56 * 1024 * 1024
NEG_BIG = -1e30
HEADS_PER_PASS = Q_PER_KV


def _cparams(*sem):
    return pltpu.CompilerParams(dimension_semantics=sem, vmem_limit_bytes=VMEM_LIMIT)


def _split_bf16(a):
    hi = a.astype(BF16)
    lo = (a - hi.astype(F32)).astype(BF16)
    return hi, lo


def _dot(a, b):
    return jnp.dot(a, b, preferred_element_type=F32)


def _dot3(a, b):
    a_hi, a_lo = _split_bf16(a)
    b_hi, b_lo = _split_bf16(b)
    return _dot(a_hi, b_hi) + _dot(a_lo, b_hi) + _dot(a_hi, b_lo)


def _qkv_kernel(x_ref, nw_ref, w_ref, seg_ref, hw_ref, rc_ref, rs1_ref, rs2_ref,
                q_ref, k_ref, v_ref):
    x = x_ref[...]
    inv = lax.rsqrt(jnp.mean(x * x, axis=-1, keepdims=True) + EPS)
    h = (x * inv * nw_ref[...]).astype(BF16)
    acc = _dot(h, w_ref[...])
    qk = acc[:, :QK_WIDTH]
    hi, lo = _split_bf16(qk * qk)
    seg = seg_ref[...]
    w = seg.shape[0]
    ms = jnp.concatenate(
        [_dot(hi[:, c * w:(c + 1) * w], seg) + _dot(lo[:, c * w:(c + 1) * w], seg)
         for c in range(QK_WIDTH // w)], axis=-1)
    xn = qk * lax.rsqrt(ms + EPS) * hw_ref[...]
    reps = QK_WIDTH // LANES
    rc = jnp.tile(rc_ref[...], (1, reps))
    rs1 = jnp.tile(rs1_ref[...], (1, reps))
    rs2 = jnp.tile(rs2_ref[...], (1, reps))
    half = ROT_DIM // 2
    y = xn * rc + pltpu.roll(xn, QK_WIDTH - half, 1) * rs1 + pltpu.roll(xn, half, 1) * rs2
    q_ref[...] = (y[:, :ATTN_WIDTH] * (HEAD_DIM ** -0.5)).astype(BF16)
    k_ref[...] = y[:, ATTN_WIDTH:].astype(BF16)
    v_ref[0] = acc[:, QK_WIDTH:].T.astype(BF16)


def _rope_tables(L):
    half = ROT_DIM // 2
    inv_freq = jnp.power(ROPE_THETA, -jnp.arange(half, dtype=F32) * 2.0 / ROT_DIM)
    ang = jnp.arange(L, dtype=F32)[:, None] * inv_freq[None, :]
    d = jnp.arange(LANES) % HEAD_DIM
    ang_l = ang[:, d % half]
    cos, sin = jnp.cos(ang_l), jnp.sin(ang_l)
    rc = jnp.where(d < ROT_DIM, cos, 1.0)
    rs1 = jnp.where(d < half, -sin, 0.0)
    rs2 = jnp.where((d >= half) & (d < ROT_DIM), sin, 0.0)
    return rc.astype(F32), rs1.astype(F32), rs2.astype(F32)


def _qkv_proj(x2d, L, norm1_w, w_qkv, q_norm_w, k_norm_w, tm=512):
    T = x2d.shape[0]
    seg_w = 256
    seg = (jnp.arange(seg_w)[:, None] // HEAD_DIM == jnp.arange(seg_w)[None, :] // HEAD_DIM)
    seg = (seg.astype(F32) / HEAD_DIM).astype(BF16)
    hw = jnp.concatenate([jnp.tile(q_norm_w, N_Q_HEADS), jnp.tile(k_norm_w, N_KV_HEADS)])[None, :]
    rc, rs1, rs2 = _rope_tables(L)
    lt = L // tm
    row = lambda i: (i, 0)
    const = lambda i: (0, 0)
    pos = lambda i: (i % lt, 0)
    return pl.pallas_call(
        _qkv_kernel,
        grid=(T // tm,),
        in_specs=[
            pl.BlockSpec((tm, D_MODEL), row),
            pl.BlockSpec((1, D_MODEL), const),
            pl.BlockSpec((D_MODEL, QKV_WIDTH), const),
            pl.BlockSpec((seg_w, seg_w), const),
            pl.BlockSpec((1, QK_WIDTH), const),
            pl.BlockSpec((tm, LANES), pos),
            pl.BlockSpec((tm, LANES), pos),
            pl.BlockSpec((tm, LANES), pos),
        ],
        out_specs=[
            pl.BlockSpec((tm, ATTN_WIDTH), row),
            pl.BlockSpec((tm, KV_WIDTH), row),
            pl.BlockSpec((1, KV_WIDTH, tm), lambda i: (i // lt, 0, i % lt)),
        ],
        out_shape=[
            jax.ShapeDtypeStruct((T, ATTN_WIDTH), BF16),
            jax.ShapeDtypeStruct((T, KV_WIDTH), BF16),
            jax.ShapeDtypeStruct((T // L, KV_WIDTH, L), BF16),
        ],
        compiler_params=_cparams("parallel"),
        name="qkv_proj",
    )(x2d, norm1_w[None, :], w_qkv, seg, hw, rc, rs1, rs2)


HALO = 16

def _uproj_kernel(x_ref, xp_ref, xn_ref, nw_ref, w_ref, cw_ref, cb_ref, o_ref, h_scr, *, lt):
    i = pl.program_id(0)
    tm = x_ref.shape[0]

    @pl.when(pl.program_id(1) == 0)
    def _():
        def normed(x):
            inv = lax.rsqrt(jnp.mean(x * x, axis=-1, keepdims=True) + EPS)
            return x * inv * nw_ref[...]

        li = i % lt
        hp = jnp.where(li > 0, normed(xp_ref[...]), 0.0)
        hn = jnp.where(li < lt - 1, normed(xn_ref[...]), 0.0)
        h_scr[0:HALO] = hp.astype(BF16)
        h_scr[HALO:HALO + tm] = normed(x_ref[...]).astype(BF16)
        h_scr[HALO + tm:2 * HALO + tm] = hn.astype(BF16)

    u = _dot(h_scr[...], w_ref[...])
    w = cw_ref[...]
    val = (w[0:1] * u[HALO - 1:HALO - 1 + tm] + w[1:2] * u[HALO:HALO + tm]
           + w[2:3] * u[HALO + 1:HALO + 1 + tm] + cb_ref[...])
    for n1 in range(tm // FFT_N2):
        for j in range(FFT_N2 // SUBLANES):
            r0 = n1 * FFT_N2 + j * SUBLANES
            for c in range(val.shape[1] // LANES):
                o_ref[0, 0, c, j, n1 * SUBLANES:(n1 + 1) * SUBLANES, :] = (
                    val[r0:r0 + SUBLANES, c * LANES:(c + 1) * LANES])


def _u_proj_conv(x2d, L, norm1_w, w_u, conv_w, conv_b, tm=1024):
    T = x2d.shape[0]
    C = HY_WIDTH
    nparts = w_u.shape[1] // C
    lt = L // tm
    hb = tm // HALO
    J = FFT_N2 // SUBLANES
    return pl.pallas_call(
        functools.partial(_uproj_kernel, lt=lt),
        grid=(T // tm, nparts),
        in_specs=[
            pl.BlockSpec((tm, D_MODEL), lambda i, j: (i, 0)),
            pl.BlockSpec((HALO, D_MODEL), lambda i, j: (jnp.maximum(i * hb - 1, 0), 0)),
            pl.BlockSpec((HALO, D_MODEL), lambda i, j: (jnp.minimum((i + 1) * hb, T // HALO - 1), 0)),
            pl.BlockSpec((1, D_MODEL), lambda i, j: (0, 0)),
            pl.BlockSpec((D_MODEL, C), lambda i, j: (0, j)),
            pl.BlockSpec((3, C), lambda i, j: (0, j)),
            pl.BlockSpec((1, C), lambda i, j: (0, j)),
        ],
        out_specs=pl.BlockSpec((1, 1, C // LANES, J, tm // J, LANES),
                               lambda i, j: (j, i // lt, 0, 0, i % lt, 0)),
        out_shape=jax.ShapeDtypeStruct((nparts, T // L, C // LANES, J, L // J, LANES), F32),
        scratch_shapes=[pltpu.VMEM((tm + 2 * HALO, D_MODEL), BF16)],
        compiler_params=_cparams("parallel", "arbitrary"),
        name="u_proj_conv",
    )(x2d, x2d, x2d, norm1_w[None, :], w_u, conv_w, conv_b[None, :])


def _fft_tables(L):
    N = 2 * L
    N2 = FFT_N2
    N1 = N // N2
    H = N1 // 2
    k1 = jnp.arange(N1, dtype=jnp.int32)
    n2 = jnp.arange(N2, dtype=jnp.int32)
    a1 = ((k1[:, None] * k1[None, :]) % N1).astype(F32) * (2.0 * math.pi / N1)
    c1, s1 = jnp.cos(a1), jnp.sin(a1)
    at = (n2[:, None] * k1[None, :]).astype(F32) * (2.0 * math.pi / N)
    ct, st = jnp.cos(at), jnp.sin(at)
    c = c1[None] * ct[:, :, None] - s1[None] * st[:, :, None]
    s = s1[None] * ct[:, :, None] + c1[None] * st[:, :, None]
    m_filt = jnp.concatenate([c, -s], axis=1).astype(BF16)
    ch, sh = c[:, :, :H], s[:, :, :H]
    m_fwd = jnp.concatenate([jnp.concatenate([ch, sh], axis=2),
                             jnp.concatenate([-sh, ch], axis=2)], axis=1)
    cT = c1[None, :H] * ct[:, None, :] - s1[None, :H] * st[:, None, :]
    sT = s1[None, :H] * ct[:, None, :] + c1[None, :H] * st[:, None, :]
    m_inv = jnp.concatenate([jnp.concatenate([cT, -sT], axis=2),
                             jnp.concatenate([sT, cT], axis=2)], axis=1)
    k2 = jnp.arange(N2, dtype=jnp.int32)
    ph = ((k2[:, None] * n2[None, :]) % N2).astype(F32) * (2.0 * math.pi / N2)
    c2, s2 = jnp.cos(ph), jnp.sin(ph)
    m2_fwd = jnp.concatenate([jnp.concatenate([c2, s2], axis=1),
                              jnp.concatenate([-s2, c2], axis=1)], axis=0)
    m2_inv = m2_fwd.T
    return dict(N1=N1, N2=N2, m_filt=m_filt, m_fwd=m_fwd.astype(BF16), m_inv=m_inv.astype(BF16),
                m2_fwd=m2_fwd.astype(BF16), m2_inv=m2_inv.astype(BF16))


def _filt1_kernel(w1_ref, b1_ref, w2_ref, b2_ref, w3_ref, b3_ref, w4_ref, a_ref, frl_ref, off_ref,
                  dl_ref, m_ref, o_ref, sabs_ref, *, L, N1, N2, nb):
    j = pl.program_id(0)
    C2 = HY_ORDER * HY_WIDTH
    H = N1 // 2
    rows = nb * N1
    ridx = lax.broadcasted_iota(jnp.int32, (rows, LANES), 0)
    lane = lax.broadcasted_iota(jnp.int32, (rows, LANES), 1)
    n2 = j * nb + ridx // N1
    n1 = ridx % N1
    r = n1 * N2 + n2
    lag = jnp.where(r < L, r, 2 * L - r)
    valid = (r != L).astype(F32)
    lagf = jnp.minimum(lag, L - 1).astype(F32)
    t = lagf / (L - 1)
    wpos = lagf * (2.0 * math.pi / L)
    phase = wpos * frl_ref[...] + off_ref[...]
    zemb = jnp.where(lane == 0, t, jnp.where(lane < HY_EMB_DIM, jnp.cos(phase), 0.0))
    half = rows // 2
    a = a_ref[...]
    h = jnp.concatenate([zemb[:half], zemb[half:]], axis=1)
    h = jnp.sin(a * (_dot3(h, w1_ref[...]) + b1_ref[...]))
    h = jnp.sin(a * (_dot3(h, w2_ref[...]) + b2_ref[...]))
    h = jnp.sin(a * (_dot3(h, w3_ref[...]) + b3_ref[...]))
    h_hi, h_lo = _split_bf16(h)
    decay = jnp.exp(-jnp.tile(t, (1, HY_WIDTH // LANES)) * dl_ref[...])
    decay = decay * jnp.tile(valid, (1, HY_WIDTH // LANES))
    decay2 = jnp.tile(decay, (1, HY_ORDER))

    @pl.when(j == 0)
    def _():
        sabs_ref[...] = jnp.zeros_like(sabs_ref)

    tot = jnp.zeros((8, C2), F32)
    for l in range(nb):
        side = (l * N1) // half
        r0 = l * N1 - side * half
        fs = slice(r0, r0 + H)
        bs = slice(r0 + H, r0 + N1)
        wf = w4_ref[side, :, :C2]
        wb = w4_ref[side, :, C2:]
        gf = _dot(h_hi[fs], wf) + _dot(h_lo[fs], wf)
        gb = _dot(h_hi[bs], wb) + _dot(h_lo[bs], wb)
        g = jnp.concatenate([gf, gb], axis=0) * decay2[l * N1:(l + 1) * N1]
        tot = tot + jnp.sum(jnp.abs(g).reshape(N1 // 8, 8, C2), axis=0)
        y = _dot(m_ref[l], g.astype(BF16))
        _store_tiles(lambda c: o_ref.at[c], _pack_complex(y[:N1], y[N1:]), C2 // LANES, l)
    sabs_ref[...] += tot


def _pack_complex(re, im):
    r = lax.bitcast_convert_type(re.astype(BF16).astype(F32), jnp.uint32)
    i = lax.bitcast_convert_type(im.astype(BF16).astype(F32), jnp.uint32)
    return r | (i >> 16)


def _unpack_complex(w):
    re = lax.bitcast_convert_type(w & jnp.uint32(0xFFFF0000), F32)
    im = lax.bitcast_convert_type(w << 16, F32)
    return re, im


_pack_bf16_pair = _pack_complex
_unpack_bf16_pair = _unpack_complex


def _store_tiles(dst, w, ncols, s):
    for c in range(ncols):
        d = dst(c)
        for t in range(w.shape[0] // SUBLANES):
            d[t, s * SUBLANES:(s + 1) * SUBLANES, :] = (
                w[t * SUBLANES:(t + 1) * SUBLANES, c * LANES:(c + 1) * LANES])


def _rows_strided(ref2d, s):
    return ref2d[pl.ds(s, ref2d.shape[0] // SUBLANES, stride=SUBLANES), :]


def _gather_cols(fn, ncols):
    return jnp.concatenate([fn(c) for c in range(ncols)], axis=1)


def _block_diag2(w):
    z = jnp.zeros_like(w)
    return jnp.concatenate([jnp.concatenate([w, z], axis=1), jnp.concatenate([z, w], axis=1)], axis=0)


def _hyena_filter_spectrum(L, tabs, filt_w1, filt_b1, filt_w2, filt_b2, filt_w3, filt_b3,
                           filt_w4, filt_freq):
    N1, N2 = tabs["N1"], tabs["N2"]
    N = N1 * N2
    C2 = HY_ORDER * HY_WIDTH
    FH = HY_FILTER_HIDDEN
    nb = max(2, 512 // N1)
    bands = (HY_EMB_DIM - 1) // 2
    fr = jnp.linspace(1e-4, bands - 1, bands, dtype=F32)
    frl = jnp.zeros((LANES,), F32).at[1:1 + bands].set(fr).at[1 + bands:1 + 2 * bands].set(fr)[None, :]
    off = jnp.zeros((LANES,), F32).at[1 + bands:1 + 2 * bands].set(0.5 * math.pi)[None, :]
    w1p = jnp.zeros((LANES, FH), F32).at[:HY_EMB_DIM].set(filt_w1)
    w4 = filt_w4.astype(BF16)
    w4z = jnp.zeros_like(w4)
    w4s = jnp.stack([jnp.concatenate([w4, w4z], axis=0), jnp.concatenate([w4z, w4], axis=0)])
    two = lambda v: jnp.tile(v, 2)[None, :]
    max_decay = math.log(HY_DECAY_TARGET) / HY_FAST_DECAY_PCT
    min_decay = math.log(HY_DECAY_TARGET) / HY_SLOW_DECAY_PCT
    deltas = jnp.abs(jnp.linspace(min_decay, max_decay, HY_WIDTH, dtype=F32))[None, :]
    full = lambda shape: pl.BlockSpec(shape, lambda j: (0,) * len(shape))
    a_g, sabs = pl.pallas_call(
        functools.partial(_filt1_kernel, L=L, N1=N1, N2=N2, nb=nb),
        grid=(N2 // nb,),
        in_specs=[
            full((2 * LANES, 2 * FH)), full((1, 2 * FH)),
            full((2 * FH, 2 * FH)), full((1, 2 * FH)),
            full((2 * FH, 2 * FH)), full((1, 2 * FH)),
            full((2, 2 * FH, 2 * C2)), full((1, 2 * FH)),
            full((1, LANES)), full((1, LANES)), full((1, HY_WIDTH)),
            pl.BlockSpec((nb, 2 * N1, N1), lambda j: (j, 0, 0)),
        ],
        out_specs=[
            pl.BlockSpec((C2 // LANES, N1 // SUBLANES, nb * SUBLANES, LANES), lambda j: (0, 0, j, 0)),
            pl.BlockSpec((8, C2), lambda j: (0, 0)),
        ],
        out_shape=[
            jax.ShapeDtypeStruct((C2 // LANES, N1 // SUBLANES, N2 * SUBLANES, LANES), jnp.uint32),
            jax.ShapeDtypeStruct((8, C2), F32),
        ],
        compiler_params=_cparams("arbitrary"),
        name="hyena_filter_stage1",
    )(_block_diag2(w1p), two(filt_b1), _block_diag2(filt_w2), two(filt_b2), _block_diag2(filt_w3),
      two(filt_b3), w4s, two(filt_freq), frl, off, deltas, tabs["m_filt"])
    scale = (1.0 / (jnp.sum(sabs, axis=0) * N))[None, :]
    return a_g, scale


def _fwd1_kernel(x_ref, m_ref, o_ref):
    cb = o_ref.shape[1]
    N1 = m_ref.shape[2]
    for s in range(SUBLANES):
        rows = [_gather_cols(lambda c: _rows_strided(x_ref.at[0, bi, c, 0], s), cb) for bi in range(2)]
        xl = jnp.concatenate(rows, axis=0).astype(BF16)
        y = _dot(m_ref[s], xl)
        _store_tiles(lambda c: o_ref.at[0, c], _pack_complex(y[:N1], y[N1:]), cb, s)


def _hyena_stage1(z, which, tabs):
    N1, N2 = tabs["N1"], tabs["N2"]
    H = N1 // 2
    B, NC = z.shape[1], z.shape[2]
    cb = min(NC, 4)
    x_spec = pl.BlockSpec((1, 2, cb, 1, H * SUBLANES, LANES), lambda p, c, j: (which, p, c, j, 0, 0))
    return pl.pallas_call(
        _fwd1_kernel,
        grid=(B // 2, NC // cb, N2 // SUBLANES),
        in_specs=[x_spec, pl.BlockSpec((SUBLANES, 2 * N1, N1), lambda p, c, j: (j, 0, 0))],
        out_specs=pl.BlockSpec((1, cb, N1 // SUBLANES, SUBLANES * SUBLANES, LANES),
                               lambda p, c, j: (p, c, 0, j, 0)),
        out_shape=jax.ShapeDtypeStruct((B // 2, NC, N1 // SUBLANES, N2 * SUBLANES, LANES), jnp.uint32),
        compiler_params=_cparams("parallel", "parallel", "parallel"),
        name="hyena_stage1",
    )(z, tabs["m_fwd"])


def _mid_kernel(a_ref, ag_ref, sc_ref, mf_ref, mi_ref, o_ref, g_scr):
    cb = a_ref.shape[1]
    N2 = a_ref.shape[3] // SUBLANES

    @pl.when(pl.program_id(2) == 0)
    def _():
        for s in range(SUBLANES):
            re, im = _unpack_complex(_gather_cols(lambda c: _rows_strided(ag_ref.at[c, 0], s), cb))
            ag = jnp.concatenate([re, im], axis=0).astype(BF16)
            g_scr[s] = _dot(mf_ref[...], ag) * sc_ref[...]

    for s in range(SUBLANES):
        re, im = _unpack_complex(_gather_cols(lambda c: _rows_strided(a_ref.at[0, c, 0], s), cb))
        a = jnp.concatenate([re, im], axis=0).astype(BF16)
        x = _dot(mf_ref[...], a)
        xr, xi = x[:N2], x[N2:]
        gr, gi = g_scr[s, :N2], g_scr[s, N2:]
        y = jnp.concatenate([xr * gr - xi * gi, xr * gi + xi * gr], axis=0).astype(BF16)
        b = _dot(mi_ref[...], y)
        _store_tiles(lambda c: o_ref.at[0, c], _pack_complex(b[:N2], b[N2:]), cb, s)


def _hyena_stage2(a, a_g, scale, order, tabs, cb=8):
    N1, N2 = tabs["N1"], tabs["N2"]
    P, NC = a.shape[:2]
    cb = min(cb, NC)
    ncb = NC // cb
    return pl.pallas_call(
        _mid_kernel,
        grid=(N1 // SUBLANES, ncb, P),
        in_specs=[
            pl.BlockSpec((1, cb, 1, N2 * SUBLANES, LANES), lambda i, c, p: (p, c, i, 0, 0)),
            pl.BlockSpec((cb, 1, N2 * SUBLANES, LANES), lambda i, c, p: (order * ncb + c, i, 0, 0)),
            pl.BlockSpec((1, cb * LANES), lambda i, c, p: (0, order * ncb + c)),
            pl.BlockSpec((2 * N2, 2 * N2), lambda i, c, p: (0, 0)),
            pl.BlockSpec((2 * N2, 2 * N2), lambda i, c, p: (0, 0)),
        ],
        out_specs=pl.BlockSpec((1, cb, N2 // SUBLANES, SUBLANES * SUBLANES, LANES),
                               lambda i, c, p: (p, c, 0, i, 0)),
        out_shape=jax.ShapeDtypeStruct((P, NC, N2 // SUBLANES, N1 * SUBLANES, LANES), jnp.uint32),
        scratch_shapes=[pltpu.VMEM((SUBLANES, 2 * N2, cb * LANES), F32)],
        compiler_params=_cparams("parallel", "parallel", "arbitrary"),
        name="hyena_stage2",
    )(a, a_g, scale, tabs["m2_fwd"], tabs["m2_inv"])


def _inv1_kernel(b_ref, m_ref, gate_ref, z_ref, bias_ref, *rest, z_slabbed, final):
    if final:
        (o_ref,) = rest
    else:
        mf_ref, o_ref, a_ref = rest
    cb = b_ref.shape[1]
    N1 = m_ref.shape[1]
    H = N1 // 2
    bias = bias_ref[...]
    for s in range(SUBLANES):
        br, bi_ = _unpack_complex(_gather_cols(lambda c: _rows_strided(b_ref.at[0, c, 0], s), cb))
        b = jnp.concatenate([br, bi_], axis=0).astype(BF16)
        conv = _dot(m_ref[s], b)
        outs = []
        for bi in range(2):
            gate = _gather_cols(lambda c: _rows_strided(gate_ref.at[0, bi, c, 0], s), cb)
            if z_slabbed:
                z = _gather_cols(lambda c: _rows_strided(z_ref.at[0, bi, c, 0], s), cb)
            else:
                z = _gather_cols(lambda c: z_ref[bi, c, s], cb)
            out = gate * (conv[bi * H:(bi + 1) * H] + z * bias)
            outs.append(out)
            for c in range(cb):
                piece = out[:, c * LANES:(c + 1) * LANES]
                if final:
                    o_ref.at[bi, c, 0][pl.ds(s, H, stride=SUBLANES), :] = piece
                else:
                    o_ref[bi, c, s] = piece
        if not final:
            xl = jnp.concatenate(outs, axis=0).astype(BF16)
            y = _dot(mf_ref[s], xl)
            _store_tiles(lambda c: a_ref.at[0, c], _pack_complex(y[:N1], y[N1:]), cb, s)


def _hyena_stage3(bq, ucl, gate_idx, z, z_idx, bias, tabs, final):
    N1, N2 = tabs["N1"], tabs["N2"]
    H = N1 // 2
    P, NC = bq.shape[:2]
    cb = min(NC, 4 if final else max(1, 512 // N1))
    z_slabbed = z_idx is not None
    slab_blk = (1, 2, cb, 1, H * SUBLANES, LANES)
    n2m_blk = (2, cb, SUBLANES, H, LANES)
    if z_slabbed:
        z_spec = pl.BlockSpec(slab_blk, lambda p, c, j: (z_idx, p, c, j, 0, 0))
    else:
        z_spec = pl.BlockSpec(n2m_blk, lambda p, c, j: (p, c, j, 0, 0))
    in_specs = [
        pl.BlockSpec((1, cb, 1, N1 * SUBLANES, LANES), lambda p, c, j: (p, c, j, 0, 0)),
        pl.BlockSpec((SUBLANES, N1, 2 * N1), lambda p, c, j: (j, 0, 0)),
        pl.BlockSpec(slab_blk, lambda p, c, j: (gate_idx, p, c, j, 0, 0)),
        z_spec,
        pl.BlockSpec((1, cb * LANES), lambda p, c, j: (0, c)),
    ]
    args = [bq, tabs["m_inv"], ucl, z, bias[None, :]]
    if final:
        out_specs = pl.BlockSpec((2, cb, 1, H * SUBLANES, LANES), lambda p, c, j: (p, c, j, 0, 0))
        out_shape = jax.ShapeDtypeStruct((2 * P, NC, N2 // SUBLANES, H * SUBLANES, LANES), F32)
    else:
        in_specs.append(pl.BlockSpec((SUBLANES, 2 * N1, N1), lambda p, c, j: (j, 0, 0)))
        args.append(tabs["m_fwd"])
        out_specs = [
            pl.BlockSpec(n2m_blk, lambda p, c, j: (p, c, j, 0, 0)),
            pl.BlockSpec((1, cb, N1 // SUBLANES, SUBLANES * SUBLANES, LANES),
                         lambda p, c, j: (p, c, 0, j, 0)),
        ]
        out_shape = [
            jax.ShapeDtypeStruct((2 * P, NC, N2, H, LANES), F32),
            jax.ShapeDtypeStruct((P, NC, N1 // SUBLANES, N2 * SUBLANES, LANES), jnp.uint32),
        ]
    return pl.pallas_call(
        functools.partial(_inv1_kernel, z_slabbed=z_slabbed, final=final),
        grid=(P, NC // cb, N2 // SUBLANES),
        in_specs=in_specs,
        out_specs=out_specs,
        out_shape=out_shape,
        compiler_params=_cparams("parallel", "parallel", "parallel"),
        name="hyena_stage3",
    )(*args)


def _hyena_mixer(ucl, g_spec, hy_bias, tabs):
    a_g, scale = g_spec
    z, z_idx = ucl, 2
    a = _hyena_stage1(z, z_idx, tabs)
    for order in range(HY_ORDER):
        bq = _hyena_stage2(a, a_g, scale, order, tabs)
        if order == HY_ORDER - 1:
            return _hyena_stage3(bq, ucl, order, z, z_idx, hy_bias[order], tabs, final=True)
        z, a = _hyena_stage3(bq, ucl, order, z, z_idx, hy_bias[order], tabs, final=False)
        z_idx = None


def _ride_along_specs(riders, steps, index):
    specs = []
    for r in riders:
        rows = r.shape[0] // steps
        specs.append(pl.BlockSpec((rows, r.shape[1]), lambda *g, _i=index: (_i(*g), 0)))
    return specs


def _ride_along_cast(in_refs, out_refs):
    for x_ref, o_ref in zip(in_refs, out_refs):
        o_ref[...] = x_ref[...].astype(o_ref.dtype)


def _attn_kernel(sink_ref, q_ref, kp_ref, kc_ref, kn_ref, vp_ref, vc_ref, vn_ref, nw_ref,
                 *rest, L, n_ride):
    ride_in, o_ref, ride_out, y_scr = rest[:n_ride], rest[n_ride], rest[n_ride + 1:-1], rest[-1]
    _ride_along_cast(ride_in, ride_out)
    i = pl.program_id(1)
    tq = q_ref.shape[1]
    QB = WINDOW
    span = QB + 2 * WINDOW
    k = jnp.concatenate([kp_ref[0], kc_ref[0], kn_ref[0]], axis=0)
    vt = jnp.concatenate([vp_ref[0], vc_ref[0], vn_ref[0]], axis=1)
    key = lax.broadcasted_iota(jnp.int32, (span, QB), 0)
    qry = lax.broadcasted_iota(jnp.int32, (span, QB), 1)
    rel = key - qry
    band = (rel >= 0) & (rel <= 2 * WINDOW)
    for sub in range(tq // QB):
        r0 = sub * QB
        kpos = i * tq + r0 - WINDOW + key
        ok = band & (kpos >= 0) & (kpos < L)
        bias = jnp.where(ok, 0.0, NEG_BIG).astype(F32)
        for h in range(N_KV_HEADS):
            kh = k[r0:r0 + span, h * HEAD_DIM:(h + 1) * HEAD_DIM]
            vht = vt[h * HEAD_DIM:(h + 1) * HEAD_DIM, r0:r0 + span]
            for g0 in range(0, Q_PER_KV, HEADS_PER_PASS):
                heads = [h * Q_PER_KV + g0 + g for g in range(HEADS_PER_PASS)]
                qh = jnp.concatenate(
                    [q_ref[0, r0:r0 + QB, a * HEAD_DIM:(a + 1) * HEAD_DIM] for a in heads], axis=0)
                s = lax.dot_general(kh, qh, (((1,), (1,)), ((), ())), preferred_element_type=F32)
                s = s + jnp.tile(bias, (1, HEADS_PER_PASS))
                sink = jnp.concatenate([jnp.full((1, QB), sink_ref[a], F32) for a in heads], axis=1)
                m = jnp.maximum(jnp.max(s, axis=0, keepdims=True), sink)
                p = jnp.exp(s - m)
                denom = jnp.sum(p, axis=0, keepdims=True) + jnp.exp(sink - m)
                ot = _dot(vht, p.astype(BF16)) / denom
                for g, a in enumerate(heads):
                    y_scr[a * HEAD_DIM:(a + 1) * HEAD_DIM, r0:r0 + QB] = ot[:, g * QB:(g + 1) * QB]
    yt = y_scr[...]
    inv = lax.rsqrt(jnp.mean(yt * yt, axis=0, keepdims=True) + EPS)
    o_ref[0] = ((yt * inv).T * nw_ref[...]).astype(o_ref.dtype)


def _windowed_attention(q, k, vt, sink, out_norm_w, riders=(), tq=512):
    B, L, _ = q.shape
    nq = L // tq
    ride_specs = _ride_along_specs(riders, B * nq, lambda b, i, s: b * nq + i)
    rq = tq // WINDOW
    nwb = L // WINDOW
    cur = lambda b, i, s: (b, i, 0)
    prv = lambda b, i, s: (b, jnp.maximum(i * rq - 1, 0), 0)
    nxt = lambda b, i, s: (b, jnp.minimum((i + 1) * rq, nwb - 1), 0)
    cur_t = lambda b, i, s: (b, 0, i)
    prv_t = lambda b, i, s: (b, 0, jnp.maximum(i * rq - 1, 0))
    nxt_t = lambda b, i, s: (b, 0, jnp.minimum((i + 1) * rq, nwb - 1))
    grid_spec = pltpu.PrefetchScalarGridSpec(
        num_scalar_prefetch=1,
        grid=(B, L // tq),
        in_specs=[
            pl.BlockSpec((1, tq, ATTN_WIDTH), cur),
            pl.BlockSpec((1, WINDOW, KV_WIDTH), prv),
            pl.BlockSpec((1, tq, KV_WIDTH), cur),
            pl.BlockSpec((1, WINDOW, KV_WIDTH), nxt),
            pl.BlockSpec((1, KV_WIDTH, WINDOW), prv_t),
            pl.BlockSpec((1, KV_WIDTH, tq), cur_t),
            pl.BlockSpec((1, KV_WIDTH, WINDOW), nxt_t),
            pl.BlockSpec((1, ATTN_WIDTH), lambda b, i, s: (0, 0)),
        ] + ride_specs,
        out_specs=[pl.BlockSpec((1, tq, ATTN_WIDTH), cur)] + ride_specs,
        scratch_shapes=[pltpu.VMEM((ATTN_WIDTH, tq), F32)],
    )
    outs = pl.pallas_call(
        functools.partial(_attn_kernel, L=L, n_ride=len(riders)),
        grid_spec=grid_spec,
        out_shape=[jax.ShapeDtypeStruct((B, L, ATTN_WIDTH), BF16)]
        + [jax.ShapeDtypeStruct(r.shape, BF16) for r in riders],
        compiler_params=_cparams("parallel", "parallel"),
        name="banded_attention",
    )(sink.astype(F32), q, k, k, k, vt, vt, vt, out_norm_w[None, :], *riders)
    return outs[0], outs[1:]


def _outproj_kernel(x_ref, a_ref, hy_ref, hw_ref, wa_ref, wh_ref, nw_ref, wr_ref, rb_ref,
                    *rest, n_ride):
    ride_in, (x1_ref, h2_ref, rt_ref), ride_out = rest[:n_ride], rest[n_ride:n_ride + 3], rest[n_ride + 3:]
    _ride_along_cast(ride_in, ride_out)
    nslab = hy_ref.shape[2]
    hy = jnp.concatenate(
        [_gather_cols(lambda c: hy_ref[0, c, j, n1 * SUBLANES:(n1 + 1) * SUBLANES, :], hy_ref.shape[1])
         for n1 in range(hy_ref.shape[3] // SUBLANES) for j in range(nslab)],
        axis=0)
    hinv = lax.rsqrt(jnp.mean(hy * hy, axis=-1, keepdims=True) + EPS)
    hy_n = (hy * hinv * hw_ref[...]).astype(BF16)
    x1 = x_ref[...] + _dot(a_ref[...], wa_ref[...]) + _dot(hy_n, wh_ref[...])
    x1_ref[...] = x1
    inv = lax.rsqrt(jnp.mean(x1 * x1, axis=-1, keepdims=True) + EPS)
    h2 = x1 * inv * nw_ref[...]
    h2_ref[...] = _pack_bf16_pair(h2[:, :D_MODEL // 2], h2[:, D_MODEL // 2:])
    h_hi, h_lo = _split_bf16(h2)
    both = _dot(h_hi, wr_ref[...])
    lg = (both[:, :ROUTER_PAD] + both[:, ROUTER_PAD:] + _dot(h_lo, wr_ref[:, :ROUTER_PAD])
          + rb_ref[...])
    lane = lax.broadcasted_iota(jnp.int32, lg.shape, 1)
    neg = jnp.float32(-jnp.inf)
    first = lambda mask: jnp.min(jnp.where(mask, lane, ROUTER_PAD), axis=-1, keepdims=True)
    gl = jnp.where(lane < N_EXPERT_GROUPS, lg, neg)
    gmax = jnp.max(gl, axis=-1, keepdims=True)
    g_idx = first(gl == gmax)
    g_gate = 1.0 / jnp.sum(jnp.exp(gl - gmax), axis=-1, keepdims=True)
    e_lane = lane - N_EXPERT_GROUPS
    in_grp = (e_lane >= g_idx * EXPERTS_PER_GROUP) & (e_lane < (g_idx + 1) * EXPERTS_PER_GROUP)
    el = jnp.where(in_grp, lg, neg)
    v1 = jnp.max(el, axis=-1, keepdims=True)
    i1 = first(el == v1)
    el2 = jnp.where(lane == i1, neg, el)
    v2 = jnp.max(el2, axis=-1, keepdims=True)
    i2 = first(el2 == v2)
    t = jnp.exp(v2 - v1)
    w1 = g_gate / (1.0 + t)
    w2 = g_gate * t / (1.0 + t)
    rt_ref[...] = jnp.where(lane == 0, (i1 - N_EXPERT_GROUPS).astype(F32),
                            jnp.where(lane == 1, (i2 - N_EXPERT_GROUPS).astype(F32),
                                      jnp.where(lane == 2, w1, jnp.where(lane == 3, w2, 0.0))))


def _out_proj(x2d, attn_n, hy, hy_norm_w, w_out_a, w_out_h, norm2_w, w_router, b_router,
              riders=(), tm=256):
    T = x2d.shape[0]
    _, NC, J, LJ, _ = hy.shape
    lt = (LJ * J) // tm
    row = lambda i: (i, 0)
    const = lambda i: (0, 0)
    ride_specs = _ride_along_specs(riders, T // tm, lambda i: i)
    outs = pl.pallas_call(
        functools.partial(_outproj_kernel, n_ride=len(riders)),
        grid=(T // tm,),
        in_specs=[
            pl.BlockSpec((tm, D_MODEL), row),
            pl.BlockSpec((tm, ATTN_WIDTH), row),
            pl.BlockSpec((1, NC, J, tm // J, LANES), lambda i: (i // lt, 0, 0, i % lt, 0)),
            pl.BlockSpec((1, HY_WIDTH), const),
            pl.BlockSpec((ATTN_WIDTH, D_MODEL), const),
            pl.BlockSpec((HY_WIDTH, D_MODEL), const),
            pl.BlockSpec((1, D_MODEL), const),
            pl.BlockSpec((D_MODEL, 2 * ROUTER_PAD), const),
            pl.BlockSpec((1, ROUTER_PAD), const),
        ] + ride_specs,
        out_specs=[
            pl.BlockSpec((tm, D_MODEL), row),
            pl.BlockSpec((tm, D_MODEL // 2), row),
            pl.BlockSpec((tm, ROUTER_PAD), row),
        ] + ride_specs,
        out_shape=[
            jax.ShapeDtypeStruct((T, D_MODEL), F32),
            jax.ShapeDtypeStruct((T, D_MODEL // 2), jnp.uint32),
            jax.ShapeDtypeStruct((T, ROUTER_PAD), F32),
        ] + [jax.ShapeDtypeStruct(r.shape, BF16) for r in riders],
        compiler_params=_cparams("parallel"),
        name="out_proj",
    )(x2d, attn_n, hy, hy_norm_w[None, :], w_out_a, w_out_h, norm2_w[None, :], w_router, b_router,
      *riders)
    return outs[0], outs[1], outs[2], outs[3:]


def _moe_kernel(be_ref, nu_ref, x_ref, wg_ref, wu_ref, wd_ref, o_ref):
    @pl.when(pl.program_id(0) < nu_ref[0])
    def _():
        half = D_MODEL // 2
        xa, xb = _unpack_bf16_pair(x_ref[...])
        xa, xb = xa.astype(BF16), xb.astype(BF16)
        g = _dot(xa, wg_ref[0, :half, :]) + _dot(xb, wg_ref[0, half:, :])
        u = _dot(xa, wu_ref[0, :half, :]) + _dot(xb, wu_ref[0, half:, :])
        hid = (g * jax.nn.sigmoid(g)) * u
        out = _dot(hid.astype(BF16), wd_ref[0])
        o_ref[...] = _pack_bf16_pair(out[:, :half], out[:, half:])


def _expert_blocks(xs, block_expert, n_used, w_gate, w_up, w_down):
    P = xs.shape[0]
    bm = MOE_BLOCK_ROWS
    blk = lambda b, be, nu: (jnp.minimum(b, nu[0] - 1), 0)
    wsel = lambda b, be, nu: (be[jnp.minimum(b, nu[0] - 1)], 0, 0)
    grid_spec = pltpu.PrefetchScalarGridSpec(
        num_scalar_prefetch=2,
        grid=(P // bm,),
        in_specs=[
            pl.BlockSpec((bm, D_MODEL // 2), blk),
            pl.BlockSpec((1, D_MODEL, D_EXPERT), wsel),
            pl.BlockSpec((1, D_MODEL, D_EXPERT), wsel),
            pl.BlockSpec((1, D_EXPERT, D_MODEL), wsel),
        ],
        out_specs=pl.BlockSpec((bm, D_MODEL // 2), blk),
    )
    return pl.pallas_call(
        _moe_kernel,
        grid_spec=grid_spec,
        out_shape=jax.ShapeDtypeStruct((P, D_MODEL // 2), jnp.uint32),
        compiler_params=_cparams("arbitrary"),
        name="moe_experts",
    )(block_expert, n_used, xs, w_gate, w_up, w_down)


def _hier_moe(x1, h2, route, w_gate, w_up, w_down):
    T = h2.shape[0]
    experts = route[:, :TOP_K].astype(jnp.int32)

    A = T * TOP_K
    bm = MOE_BLOCK_ROWS
    flat_e = experts.reshape(-1)
    onehot = (flat_e[:, None] == jnp.arange(N_EXPERTS, dtype=jnp.int32)[None, :]).astype(jnp.int32)
    csum = jnp.cumsum(onehot, axis=0)
    counts = csum[-1]
    rank = jnp.sum((csum - onehot) * onehot, axis=1)
    padded = ((counts + bm - 1) // bm) * bm
    pend = jnp.cumsum(padded)
    pstart = pend - padded
    seg_start = jnp.cumsum(counts) - counts
    dest = (pstart[flat_e] + rank).astype(jnp.int32)
    n_blocks = (A + bm - 1) // bm + N_EXPERTS
    P = n_blocks * bm
    order = jnp.argsort(flat_e, stable=True).astype(jnp.int32)
    q = jnp.arange(P, dtype=jnp.int32)
    e_q = jnp.minimum(jnp.sum((pend[None, :] <= q[:, None]).astype(jnp.int32), axis=1), N_EXPERTS - 1)
    r_q = jnp.minimum(q - pstart[e_q], jnp.maximum(counts[e_q] - 1, 0))
    src = jnp.clip(seg_start[e_q] + r_q, 0, A - 1)
    buf_tok = order.at[src].get(mode="promise_in_bounds") // TOP_K
    block_expert = e_q.reshape(n_blocks, bm)[:, 0]
    n_used = (pend[-1] // bm).astype(jnp.int32)[None]

    xs = h2.at[buf_tok].get(mode="promise_in_bounds")
    yb = _expert_blocks(xs, block_expert, n_used, w_gate, w_up, w_down)
    d = dest.reshape(T, TOP_K)
    y0 = yb.at[d[:, 0]].get(mode="promise_in_bounds")
    y1 = yb.at[d[:, 1]].get(mode="promise_in_bounds")
    return _moe_combine(x1, y0, y1, route)


def _combine_kernel(x1_ref, y0_ref, y1_ref, rt_ref, o_ref):
    half = D_MODEL // 2
    rt = rt_ref[...]
    w0, w1 = rt[:, TOP_K:TOP_K + 1], rt[:, TOP_K + 1:TOP_K + 2]
    a0, b0 = _unpack_bf16_pair(y0_ref[...])
    a1, b1 = _unpack_bf16_pair(y1_ref[...])
    o_ref[:, :half] = x1_ref[:, :half] + (w0 * a0 + w1 * a1)
    o_ref[:, half:] = x1_ref[:, half:] + (w0 * b0 + w1 * b1)


def _moe_combine(x1, y0, y1, route, tm=512):
    T = x1.shape[0]
    row = lambda i: (i, 0)
    return pl.pallas_call(
        _combine_kernel,
        grid=(T // tm,),
        in_specs=[
            pl.BlockSpec((tm, D_MODEL), row),
            pl.BlockSpec((tm, D_MODEL // 2), row),
            pl.BlockSpec((tm, D_MODEL // 2), row),
            pl.BlockSpec((tm, ROUTER_PAD), row),
        ],
        out_specs=pl.BlockSpec((tm, D_MODEL), row),
        out_shape=jax.ShapeDtypeStruct((T, D_MODEL), F32),
        compiler_params=_cparams("parallel"),
        name="moe_combine",
    )(x1, y0, y1, route)


def _encoder_layer(x, p):
    B, L, D = x.shape
    T = B * L
    x2d = x.reshape(T, D)
    need_cast = "experts_bf16" not in p
    ride_a, ride_o = (), ()
    if need_cast:
        wg, wu, wd = p["experts_f32"]
        ride_a = (wg.reshape(-1, D_EXPERT), wu.reshape(-1, D_EXPERT))
        ride_o = (wd.reshape(-1, D_MODEL),)
    q, k, v = _qkv_proj(x2d, L, p["norm1_w"], p["w_qkv"], p["q_norm_w"], p["k_norm_w"])
    attn_n, cast_a = _windowed_attention(q.reshape(B, L, ATTN_WIDTH), k.reshape(B, L, KV_WIDTH),
                                         v, p["attn_sink"], p["attn_out_norm_w"], riders=ride_a)
    uc = _u_proj_conv(x2d, L, p["norm1_w"], p["w_u"], p["conv_w"], p["conv_b"])
    tabs = _fft_tables(L)
    g_spec = _hyena_filter_spectrum(L, tabs, p["filt_w1"], p["filt_b1"], p["filt_w2"], p["filt_b2"],
                                    p["filt_w3"], p["filt_b3"], p["filt_w4"], p["filt_freq"])
    hy = _hyena_mixer(uc, g_spec, p["hy_bias"], tabs)
    x1, h2, route, cast_o = _out_proj(x2d, attn_n.reshape(T, ATTN_WIDTH), hy, p["hy_out_norm_w"],
                                      p["w_out_a"], p["w_out_h"], p["norm2_w"], p["w_router"],
                                      p["b_router"], riders=ride_o)
    if need_cast:
        p["experts_bf16"] = (cast_a[0].reshape(wg.shape), cast_a[1].reshape(wu.shape),
                             cast_o[0].reshape(wd.shape))
    out = _hier_moe(x1, h2, route, *p["experts_bf16"])
    return out.reshape(B, L, D)


def kernel(x_prompt, x_sample, norm1_w, w_in, q_norm_w, k_norm_w, attn_sink, conv_w, conv_b, filt_w1, filt_b1, filt_w2, filt_b2, filt_w3, filt_b3, filt_w4, filt_freq, hy_bias, attn_out_norm_w, hy_out_norm_w, w_out, norm2_w, w_route_group, b_route_group, w_route_expert, b_route_expert, w_gate, w_up, w_down):
    depth = norm1_w.shape[0]

    def layer_params(l):
        w_r = jnp.zeros((D_MODEL, ROUTER_PAD), F32)
        w_r = w_r.at[:, :N_EXPERT_GROUPS].set(w_route_group[l])
        w_r = w_r.at[:, N_EXPERT_GROUPS:N_EXPERT_GROUPS + N_EXPERTS].set(w_route_expert[l])
        r_hi, r_lo = _split_bf16(w_r)
        b_r = jnp.zeros((1, ROUTER_PAD), F32)
        b_r = b_r.at[0, :N_EXPERT_GROUPS].set(b_route_group[l])
        b_r = b_r.at[0, N_EXPERT_GROUPS:N_EXPERT_GROUPS + N_EXPERTS].set(b_route_expert[l])
        return dict(
            norm1_w=norm1_w[l], w_qkv=w_in[l][:, :QKV_WIDTH].astype(BF16),
            w_u=w_in[l][:, QKV_WIDTH:].astype(BF16), q_norm_w=q_norm_w[l], k_norm_w=k_norm_w[l],
            attn_sink=attn_sink[l], conv_w=conv_w[l], conv_b=conv_b[l],
            filt_w1=filt_w1[l], filt_b1=filt_b1[l], filt_w2=filt_w2[l], filt_b2=filt_b2[l],
            filt_w3=filt_w3[l], filt_b3=filt_b3[l], filt_w4=filt_w4[l], filt_freq=filt_freq[l],
            hy_bias=hy_bias[l], attn_out_norm_w=attn_out_norm_w[l], hy_out_norm_w=hy_out_norm_w[l],
            w_out_a=w_out[l][:ATTN_WIDTH].astype(BF16), w_out_h=w_out[l][ATTN_WIDTH:].astype(BF16),
            norm2_w=norm2_w[l], w_router=jnp.concatenate([r_hi, r_lo], axis=1), b_router=b_r,
            experts_f32=(w_gate[l], w_up[l], w_down[l]))

    params = [layer_params(l) for l in range(depth)]

    def trunk(x):
        for p in params:
            x = _encoder_layer(x, p)
        return x

    return (trunk(x_prompt), trunk(x_sample))
```
